```python
import math
import jax, jax.numpy as jnp
from jax import lax
import numpy as np

D_MODEL = 2048
BATCH = 4
SEQ = 2048
DEPTH = 2

GRID_W = 64
CTX_LEN = 256
D_FF = 5632
CONV_DIM = 1024
CONV_K = 31
SSM_DIM = 1024
SSM_GROUP = 16
SSM_GROUPS = SSM_DIM // SSM_GROUP
SSM_STATE = 64
N_MOD = 9
N_IN = 2 * CONV_DIM + SSM_DIM + 2 * D_MODEL
CONV_IN_END = 2 * CONV_DIM
SSM_IN_END = 2 * CONV_DIM + SSM_DIM
FFN_RES = 0.5
EPS = 1e-6
DT_MIN = 1e-3
DT_MAX = 1e-1

kernel_name = "hybrid_conformer_s5_dit_prefix"


def rmsnorm(x, g):
    x32 = x.astype(jnp.float32)
    y = x32 * lax.rsqrt(jnp.mean(x32 * x32, axis=-1, keepdims=True) + EPS)
    return (y * g.astype(jnp.float32)).astype(x.dtype)


def layernorm(x, g, b):
    x32 = x.astype(jnp.float32)
    mu = jnp.mean(x32, axis=-1, keepdims=True)
    var = jnp.mean(jnp.square(x32 - mu), axis=-1, keepdims=True)
    y = (x32 - mu) * lax.rsqrt(var + EPS)
    return (y * g.astype(jnp.float32) + b.astype(jnp.float32)).astype(x.dtype)


def modulate(h, shift, scale):
    return h * (1 + scale) + shift


def swiglu(h, w_up, w_down):
    a, g = jnp.split(h @ w_up, 2, axis=-1)
    return (jax.nn.silu(g) * a) @ w_down


def dwconv(u, w, b):
    y = lax.conv_general_dilated(
        u, w[:, None, :], window_strides=(1,),
        padding=[(CONV_K // 2, CONV_K // 2)],
        dimension_numbers=("NWC", "WIO", "NWC"),
        feature_group_count=u.shape[-1])
    return y + b


def conv_grid(u, w, b, rows):
    bn, L, C = u.shape
    half = C // 2
    g = u.reshape(bn, rows, GRID_W, C)
    horiz = dwconv(g[..., :half].reshape(bn * rows, GRID_W, half), w[:, :half], b[:half])
    horiz = horiz.reshape(bn, rows, GRID_W, half)
    vin = g[..., half:].transpose(0, 2, 1, 3).reshape(bn * GRID_W, rows, half)
    vert = dwconv(vin, w[:, half:], b[half:]).reshape(bn, GRID_W, rows, half).transpose(0, 2, 1, 3)
    return jnp.concatenate([horiz, vert], axis=-1).reshape(bn, L, C)


def conv_branch(conv_in, p, conv_fn):
    a, g = jnp.split(conv_in, 2, axis=-1)
    u = a * jax.nn.sigmoid(g)
    u = conv_fn(u)
    u = jax.nn.silu(layernorm(u, p["conv_ln_g"], p["conv_ln_b"]))
    return u @ p["conv_w_out"]


def ssm_discretize(a_re, a_im, log_dt, b_re, b_im):
    lam = lax.complex(a_re.astype(jnp.float32), a_im.astype(jnp.float32))
    dt = jnp.exp(log_dt.astype(jnp.float32))[:, None]
    lam_bar = jnp.exp(lam * dt)
    bmat = lax.complex(b_re.astype(jnp.float32), b_im.astype(jnp.float32))
    b_bar = ((lam_bar - 1) / lam)[..., None] * bmat
    return lam_bar, b_bar


def _lin_combine(left, right):
    a_l, b_l = left
    a_r, b_r = right
    return a_r * a_l, a_r * b_l + b_r


def ssm_scan(lam_bar, b_bar, u, h0, reverse):
    bu = jnp.einsum("gpc,blgc->blgp", b_bar, u.astype(jnp.complex64))
    if h0 is not None:
        idx = -1 if reverse else 0
        bu = bu.at[:, idx].add(lam_bar * h0)
    a = jnp.broadcast_to(lam_bar, bu.shape)
    _, h = lax.associative_scan(_lin_combine, (a, bu), reverse=reverse, axis=1)
    return h


def ssm_readout(hf, hb, cf, cb, u, p):
    bn, L = u.shape[0], u.shape[1]
    y = jnp.real(jnp.einsum("gcp,blgp->blgc", cf, hf) + jnp.einsum("gcp,blgp->blgc", cb, hb))
    y = y.reshape(bn, L, SSM_DIM).astype(u.dtype) + p["ssm_d"] * u
    z = jax.nn.gelu(y) @ p["ssm_w_glu"] + p["ssm_b_glu"]
    a, g = jnp.split(z, 2, axis=-1)
    return a * jax.nn.sigmoid(g)


def ssm_branch(u, uc, p, ctx_out):
    lam_f, bbar_f = ssm_discretize(p["ssm_a_re"][0], p["ssm_a_im"][0], p["ssm_log_dt"][0],
                                   p["ssm_b_re"][0], p["ssm_b_im"][0])
    lam_b, bbar_b = ssm_discretize(p["ssm_a_re"][1], p["ssm_a_im"][1], p["ssm_log_dt"][1],
                                   p["ssm_b_re"][1], p["ssm_b_im"][1])
    cf = lax.complex(p["ssm_c_re"][0].astype(jnp.float32), p["ssm_c_im"][0].astype(jnp.float32))
    cb = lax.complex(p["ssm_c_re"][1].astype(jnp.float32), p["ssm_c_im"][1].astype(jnp.float32))

    def grp(v):
        return v.astype(jnp.float32).reshape(v.shape[0], v.shape[1], SSM_GROUPS, SSM_GROUP)

    ug, ucg = grp(u), grp(uc)
    hcf = ssm_scan(lam_f, bbar_f, ucg, None, False)
    hcb = ssm_scan(lam_b, bbar_b, ucg, None, True)
    hf = ssm_scan(lam_f, bbar_f, ug, hcf[:, -1], False)
    hb = ssm_scan(lam_b, bbar_b, ug, hcb[:, 0], True)
    y = ssm_readout(hf, hb, cf, cb, u, p)
    yc = ssm_readout(hcf, hcb, cf, cb, uc, p) if ctx_out else None
    return y, yc


def mixer(h, hc, p, rows, ctx_out):
    proj = h @ p["w_in"] + p["b_in"]
    conv_in, ssm_in, gate = jnp.split(proj, [CONV_IN_END, SSM_IN_END], axis=-1)
    if ctx_out:
        projc = hc @ p["w_in"] + p["b_in"]
        conv_in_c, ssm_in_c, gate_c = jnp.split(projc, [CONV_IN_END, SSM_IN_END], axis=-1)
    else:
        ssm_in_c = hc @ p["w_in"][:, CONV_IN_END:SSM_IN_END] + p["b_in"][CONV_IN_END:SSM_IN_END]
    y_conv = conv_branch(conv_in, p, lambda v: conv_grid(v, p["conv_dw"], p["conv_db"], rows))
    y_ssm, y_ssm_c = ssm_branch(ssm_in, ssm_in_c, p, ctx_out)
    g_conv, g_ssm = jnp.split(gate, 2, axis=-1)
    y = (jax.nn.sigmoid(g_conv) * y_conv + jax.nn.sigmoid(g_ssm) * y_ssm) @ p["w_out"]
    if ctx_out:
        y_conv_c = conv_branch(conv_in_c, p, lambda v: dwconv(v, p["conv_dw"], p["conv_db"]))
        gc_conv, gc_ssm = jnp.split(gate_c, 2, axis=-1)
        yc = (jax.nn.sigmoid(gc_conv) * y_conv_c + jax.nn.sigmoid(gc_ssm) * y_ssm_c) @ p["w_out"]
    else:
        yc = None
    return y, yc


def layer(x, xc, c, c_ctx, p, rows, ctx_out):
    mod = jax.nn.silu(c)[:, None, :] @ p["ada_w"] + p["ada_b"]
    modc = jax.nn.silu(c_ctx)[None, None, :] @ p["ada_w"] + p["ada_b"]
    sh1, sc1, g1, sh2, sc2, g2, sh3, sc3, g3 = jnp.split(mod, N_MOD, axis=-1)
    csh1, csc1, cg1, csh2, csc2, cg2, csh3, csc3, cg3 = jnp.split(modc, N_MOD, axis=-1)
    x = x + g1 * FFN_RES * swiglu(modulate(rmsnorm(x, p["norm_ffn1"]), sh1, sc1),
                                  p["ffn1_w_up"], p["ffn1_w_down"])
    xc = xc + cg1 * FFN_RES * swiglu(modulate(rmsnorm(xc, p["norm_ffn1"]), csh1, csc1),
                                     p["ffn1_w_up"], p["ffn1_w_down"])
    h = modulate(rmsnorm(x, p["norm_mix"]), sh2, sc2)
    hc = modulate(rmsnorm(xc, p["norm_mix"]), csh2, csc2)
    y, yc = mixer(h, hc, p, rows, ctx_out)
    x = x + g2 * y
    x = x + g3 * FFN_RES * swiglu(modulate(rmsnorm(x, p["norm_ffn2"]), sh3, sc3),
                                  p["ffn2_w_up"], p["ffn2_w_down"])
    if ctx_out:
        xc = xc + cg2 * yc
        xc = xc + cg3 * FFN_RES * swiglu(modulate(rmsnorm(xc, p["norm_ffn2"]), csh3, csc3),
                                         p["ffn2_w_up"], p["ffn2_w_down"])
    else:
        xc = None
    return x, xc


def setup_inputs(seed: int = 0) -> dict:
    key = jax.random.key(seed)
    ks = jax.random.split(key, 40)
    f32 = jnp.float32
    nrm = lambda k, shape, s: jax.random.normal(k, shape, f32) * s
    gain = lambda k, shape: 1.0 + 0.05 * jax.random.normal(k, shape, f32)
    L, G, P, GC = DEPTH, SSM_GROUPS, SSM_STATE, SSM_GROUP
    a_im_base = math.pi * jnp.arange(P, dtype=f32)
    return {
        "x": nrm(ks[0], (BATCH, SEQ, D_MODEL), 1.0),
        "c": nrm(ks[1], (BATCH, D_MODEL), 1.0),
        "ctx": nrm(ks[2], (BATCH, CTX_LEN, D_MODEL), 1.0),
        "c_ctx": nrm(ks[3], (D_MODEL,), 1.0),
        "ada_w": nrm(ks[4], (L, D_MODEL, N_MOD * D_MODEL), 0.5 * D_MODEL ** -0.5),
        "ada_b": nrm(ks[5], (L, N_MOD * D_MODEL), 0.01),
        "norm_ffn1": gain(ks[6], (L, D_MODEL)),
        "ffn1_w_up": nrm(ks[7], (L, D_MODEL, 2 * D_FF), D_MODEL ** -0.5),
        "ffn1_w_down": nrm(ks[8], (L, D_FF, D_MODEL), D_FF ** -0.5),
        "norm_mix": gain(ks[9], (L, D_MODEL)),
        "w_in": nrm(ks[10], (L, D_MODEL, N_IN), D_MODEL ** -0.5),
        "b_in": nrm(ks[11], (L, N_IN), 0.01),
        "conv_dw": nrm(ks[12], (L, CONV_K, CONV_DIM), CONV_K ** -0.5),
        "conv_db": nrm(ks[13], (L, CONV_DIM), 0.01),
        "conv_ln_g": gain(ks[14], (L, CONV_DIM)),
        "conv_ln_b": nrm(ks[15], (L, CONV_DIM), 0.01),
        "conv_w_out": nrm(ks[16], (L, CONV_DIM, D_MODEL), CONV_DIM ** -0.5),
        "ssm_a_re": -0.5 + nrm(ks[17], (L, 2, G, P), 0.01),
        "ssm_a_im": a_im_base + nrm(ks[18], (L, 2, G, P), 0.01),
        "ssm_log_dt": jax.random.uniform(ks[19], (L, 2, G), f32, math.log(DT_MIN), math.log(DT_MAX)),
        "ssm_b_re": nrm(ks[20], (L, 2, G, P, GC), (2.0 * GC) ** -0.5),
        "ssm_b_im": nrm(ks[21], (L, 2, G, P, GC), (2.0 * GC) ** -0.5),
        "ssm_c_re": nrm(ks[22], (L, 2, G, GC, P), (2.0 * P) ** -0.5),
        "ssm_c_im": nrm(ks[23], (L, 2, G, GC, P), (2.0 * P) ** -0.5),
        "ssm_d": nrm(ks[24], (L, SSM_DIM), 1.0),
        "ssm_w_glu": nrm(ks[25], (L, SSM_DIM, 2 * D_MODEL), SSM_DIM ** -0.5),
        "ssm_b_glu": nrm(ks[26], (L, 2 * D_MODEL), 0.01),
        "w_out": nrm(ks[27], (L, D_MODEL, D_MODEL), D_MODEL ** -0.5),
        "norm_ffn2": gain(ks[28], (L, D_MODEL)),
        "ffn2_w_up": nrm(ks[29], (L, D_MODEL, 2 * D_FF), D_MODEL ** -0.5),
        "ffn2_w_down": nrm(ks[30], (L, D_FF, D_MODEL), D_FF ** -0.5),
        "norm_final": gain(ks[31], (D_MODEL,)),
    }


def reference(x, c, ctx, c_ctx, ada_w, ada_b, norm_ffn1, ffn1_w_up, ffn1_w_down,
              norm_mix, w_in, b_in, conv_dw, conv_db, conv_ln_g, conv_ln_b, conv_w_out,
              ssm_a_re, ssm_a_im, ssm_log_dt, ssm_b_re, ssm_b_im, ssm_c_re, ssm_c_im,
              ssm_d, ssm_w_glu, ssm_b_glu, w_out, norm_ffn2, ffn2_w_up, ffn2_w_down,
              norm_final):
    rows = x.shape[1] // GRID_W
    xc = ctx
    for i in range(DEPTH):
        p = dict(
            ada_w=ada_w[i], ada_b=ada_b[i],
            norm_ffn1=norm_ffn1[i], ffn1_w_up=ffn1_w_up[i], ffn1_w_down=ffn1_w_down[i],
            norm_mix=norm_mix[i], w_in=w_in[i], b_in=b_in[i],
            conv_dw=conv_dw[i], conv_db=conv_db[i], conv_ln_g=conv_ln_g[i],
            conv_ln_b=conv_ln_b[i], conv_w_out=conv_w_out[i],
            ssm_a_re=ssm_a_re[i], ssm_a_im=ssm_a_im[i], ssm_log_dt=ssm_log_dt[i],
            ssm_b_re=ssm_b_re[i], ssm_b_im=ssm_b_im[i], ssm_c_re=ssm_c_re[i], ssm_c_im=ssm_c_im[i],
            ssm_d=ssm_d[i], ssm_w_glu=ssm_w_glu[i], ssm_b_glu=ssm_b_glu[i],
            w_out=w_out[i],
            norm_ffn2=norm_ffn2[i], ffn2_w_up=ffn2_w_up[i], ffn2_w_down=ffn2_w_down[i],
        )
        x, xc = layer(x, xc, c, c_ctx, p, rows, i < DEPTH - 1)
    return rmsnorm(x, norm_final)
```

```python
import functools

import jax
import jax.numpy as jnp
from jax import lax
from jax.experimental import pallas as pl
from jax.experimental.pallas import tpu as pltpu

F32 = jnp.float32
BF16 = jnp.bfloat16

D_MODEL = 2048
BATCH = 4
SEQ = 2048
CTX_LEN = 256
DEPTH = 2
GRID_W = 64
GRID_ROWS = SEQ // GRID_W
D_FF = 5632
CONV_DIM = 1024
CONV_K = 31
CONV_PAD = CONV_K // 2
SSM_DIM = 1024
SSM_GROUP = 16
SSM_GROUPS = 64
SSM_STATE = 64
N_MOD = 9
N_IN = 2 * CONV_DIM + SSM_DIM + 2 * D_MODEL
FFN_RES = 0.5
EPS = 1e-6

LANES = 128
SUBLANES = 8
ROWS_CTX = CTX_LEN * BATCH
ROWS_LAT = SEQ * BATCH
ROWS = ROWS_CTX + ROWS_LAT
ROWS_PER_GRID_ROW = GRID_W * BATCH

VMEM_LIMIT = 56 * 1024 * 1024

GROUPS_PER_SLAB = LANES // SSM_GROUP
N_GP = SSM_GROUPS // 2
GP_PER_SLAB = GROUPS_PER_SLAB // 2
SSM_CHUNK_T = 64
SSM_CHUNK_ROWS = SSM_CHUNK_T * BATCH
N_CHUNK_CTX = CTX_LEN // SSM_CHUNK_T
N_CHUNK = (CTX_LEN + SEQ) // SSM_CHUNK_T
GP_BLOCK = 4


def _cparams(sem):
    return pltpu.CompilerParams(dimension_semantics=sem, vmem_limit_bytes=VMEM_LIMIT)


def _rows8(x, m, op):
    n, d = x.shape
    x3 = x.reshape(n // SUBLANES, SUBLANES, d)
    return op(x3, m[None]).reshape(n, d)


def _ada_kernel(c_ref, w_ref, b_ref, o_ref):
    c = c_ref[...]
    s = jax.nn.silu(c).astype(BF16)
    o_ref[0] = jnp.dot(s, w_ref[0].astype(BF16), preferred_element_type=F32) + b_ref[0]


def _ada(c8, ada_w, ada_b):
    tn = 1024
    n = N_MOD * D_MODEL
    return pl.pallas_call(
        _ada_kernel,
        grid=(DEPTH, n // tn),
        in_specs=[
            pl.BlockSpec((SUBLANES, D_MODEL), lambda l, j: (0, 0)),
            pl.BlockSpec((1, D_MODEL, tn), lambda l, j: (l, 0, j)),
            pl.BlockSpec((1, 1, tn), lambda l, j: (l, 0, j)),
        ],
        out_specs=pl.BlockSpec((1, SUBLANES, tn), lambda l, j: (l, 0, j)),
        out_shape=jax.ShapeDtypeStruct((DEPTH, SUBLANES, n), F32),
        compiler_params=_cparams(("arbitrary", "arbitrary")),
        name="ada_mod",
    )(c8, ada_w, ada_b.reshape(DEPTH, 1, n))


def _ffn_kernel(x_ref, mod_ref, gain_ref, wa_ref, wg_ref, wd_ref, *rest, nf, final):
    if final:
        fgain_ref, o_ref, hn_ref, acc_ref = rest
    else:
        o_ref, hn_ref = rest
        acc_ref = o_ref
    j = pl.program_id(1)

    @pl.when(j == 0)
    def _():
        x = x_ref[...]
        y = x * lax.rsqrt(jnp.mean(x * x, axis=-1, keepdims=True) + EPS) * gain_ref[...]
        h = _rows8(y, 1.0 + mod_ref[1], jnp.multiply)
        h = _rows8(h, mod_ref[0], jnp.add)
        hn_ref[...] = h.astype(BF16)
        acc_ref[...] = jnp.zeros_like(acc_ref)

    hn = hn_ref[...]
    a = jnp.dot(hn, wa_ref[...], preferred_element_type=F32)
    g = jnp.dot(hn, wg_ref[...], preferred_element_type=F32)
    act = (jax.nn.silu(g) * a).astype(BF16)
    acc_ref[...] += jnp.dot(act, wd_ref[...], preferred_element_type=F32)

    @pl.when(j == nf - 1)
    def _():
        upd = _rows8(acc_ref[...], mod_ref[2] * FFN_RES, jnp.multiply)
        xn = x_ref[...] + upd
        if final:
            xn = xn * lax.rsqrt(jnp.mean(xn * xn, axis=-1, keepdims=True) + EPS) * fgain_ref[...]
        o_ref[...] = xn


def _ffn(x, modtab, sub, gain, w_up, w_down, *, tm, tf, latent_only=False, final_gain=None):
    nf = D_FF // tf
    n_lat_tiles = ROWS_LAT // tm
    n_tiles = n_lat_tiles if latent_only else ROWS // tm
    final = final_gain is not None
    in_specs = [
        pl.BlockSpec((tm, D_MODEL), lambda i, j: (i, 0)),
        pl.BlockSpec((None, 3, SUBLANES, D_MODEL),
                     lambda i, j: (jnp.where(i < n_lat_tiles, 1, 0), sub, 0, 0)),
        pl.BlockSpec((1, D_MODEL), lambda i, j: (0, 0)),
        pl.BlockSpec((D_MODEL, tf), lambda i, j: (0, j)),
        pl.BlockSpec((D_MODEL, tf), lambda i, j: (0, j + nf)),
        pl.BlockSpec((tf, D_MODEL), lambda i, j: (j, 0)),
    ]
    args = [x, modtab, gain.reshape(1, D_MODEL), w_up, w_up, w_down]
    scratch = [pltpu.VMEM((tm, D_MODEL), BF16)]
    if final:
        in_specs.append(pl.BlockSpec((1, D_MODEL), lambda i, j: (0, 0)))
        args.append(final_gain.reshape(1, D_MODEL))
        scratch.append(pltpu.VMEM((tm, D_MODEL), F32))
        out_rows = n_tiles * tm
        out_spec = pl.BlockSpec((tm, D_MODEL), lambda i, j: (i, 0))
        aliases = {}
    else:
        out_rows = ROWS
        out_spec = pl.BlockSpec((tm, D_MODEL), lambda i, j: (i, 0))
        aliases = {0: 0}
    return pl.pallas_call(
        functools.partial(_ffn_kernel, nf=nf, final=final),
        grid=(n_tiles, nf),
        in_specs=in_specs,
        out_specs=out_spec,
        out_shape=jax.ShapeDtypeStruct((out_rows, D_MODEL), F32),
        scratch_shapes=scratch,
        input_output_aliases=aliases,
        compiler_params=_cparams(("arbitrary", "arbitrary")),
        name="ffn_final" if final else "ffn",
    )(*args)


def _inproj_kernel(x_ref, mod_ref, gain_ref, w_ref, wg_ref, b_ref, bg_ref,
                   uc_ref, us_ref, gt_ref, hn_ref):
    j = pl.program_id(1)

    @pl.when(j == 0)
    def _():
        x = x_ref[...]
        y = x * lax.rsqrt(jnp.mean(x * x, axis=-1, keepdims=True) + EPS) * gain_ref[...]
        h = _rows8(y, 1.0 + mod_ref[1], jnp.multiply)
        h = _rows8(h, mod_ref[0], jnp.add)
        hn_ref[...] = h.astype(BF16)

    hn = hn_ref[...]
    z = jnp.dot(hn, w_ref[...], preferred_element_type=F32) + b_ref[0]

    @pl.when(j == 0)
    def _():
        zg = jnp.dot(hn, wg_ref[...], preferred_element_type=F32) + bg_ref[0]
        uc_ref[...] = z * jax.nn.sigmoid(zg)

    @pl.when(j == 1)
    def _():
        us_ref[...] = z

    @pl.when(j >= 2)
    def _():
        gt_ref[...] = jax.nn.sigmoid(z)


def _inproj(x, modtab, gain, w_in, b_in, *, tm):
    tn = 1024
    n_lat_tiles = ROWS_LAT // tm
    nblk = N_IN // tn
    wblk = lambda j: jnp.where(j == 0, 0, j + 1)
    b3 = b_in.reshape(nblk, 1, tn)
    return pl.pallas_call(
        _inproj_kernel,
        grid=(ROWS // tm, nblk - 1),
        in_specs=[
            pl.BlockSpec((tm, D_MODEL), lambda i, j: (i, 0)),
            pl.BlockSpec((None, 3, SUBLANES, D_MODEL),
                         lambda i, j: (jnp.where(i < n_lat_tiles, 1, 0), 1, 0, 0)),
            pl.BlockSpec((1, D_MODEL), lambda i, j: (0, 0)),
            pl.BlockSpec((D_MODEL, tn), lambda i, j: (0, wblk(j))),
            pl.BlockSpec((D_MODEL, tn), lambda i, j: (0, 1)),
            pl.BlockSpec((1, 1, tn), lambda i, j: (wblk(j), 0, 0)),
            pl.BlockSpec((1, 1, tn), lambda i, j: (1, 0, 0)),
        ],
        out_specs=[
            pl.BlockSpec((tm, tn), lambda i, j: (i, 0)),
            pl.BlockSpec((tm, tn), lambda i, j: (i, 0)),
            pl.BlockSpec((tm, tn), lambda i, j: (i, jnp.maximum(j - 2, 0))),
        ],
        out_shape=[
            jax.ShapeDtypeStruct((ROWS, CONV_DIM), F32),
            jax.ShapeDtypeStruct((ROWS, SSM_DIM), F32),
            jax.ShapeDtypeStruct((ROWS, 2 * D_MODEL), F32),
        ],
        scratch_shapes=[pltpu.VMEM((tm, D_MODEL), BF16)],
        compiler_params=_cparams(("arbitrary", "arbitrary")),
        name="mixer_inproj",
    )(x, modtab, gain.reshape(1, D_MODEL), w_in, w_in, b3, b3)


CONV_ROW_PAD = 64
CONV_SUB_ROWS = 64


def _conv_seq_kernel(u_ref, w_ref, b_ref, o_ref, pad_ref, *, n, cb):
    zeros = jnp.zeros((CONV_ROW_PAD, cb), F32)
    pad_ref[0:CONV_ROW_PAD, :] = zeros
    pad_ref[CONV_ROW_PAD + n:, :] = zeros
    pad_ref[CONV_ROW_PAD:CONV_ROW_PAD + n, :] = u_ref[...]
    base = CONV_ROW_PAD - CONV_PAD * BATCH
    for c in range(cb // LANES):
        cs = slice(c * LANES, (c + 1) * LANES)
        for r in range(n // CONV_SUB_ROWS):
            r0 = r * CONV_SUB_ROWS
            acc = jnp.broadcast_to(b_ref[:, cs], (CONV_SUB_ROWS, LANES))
            for k in range(CONV_K):
                off = base + r0 + k * BATCH
                acc = acc + w_ref[k:k + 1, cs] * pad_ref[off:off + CONV_SUB_ROWS, cs]
            o_ref[r0:r0 + CONV_SUB_ROWS, cs] = acc


def _conv_seq(u, w, b, *, n, cb, row_blk0, n_row_blk, col_blk0, n_col_blk, prev=None):
    in_specs = [
        pl.BlockSpec((n, cb), lambda i, c: (i + row_blk0, c + col_blk0)),
        pl.BlockSpec((CONV_K, cb), lambda i, c: (0, c + col_blk0)),
        pl.BlockSpec((1, cb), lambda i, c: (0, c + col_blk0)),
    ]
    args = [u, w, b.reshape(1, CONV_DIM)]
    aliases = {}
    if prev is not None:
        in_specs.append(pl.BlockSpec(memory_space=pl.ANY))
        args.append(prev)
        aliases = {3: 0}

    def body(u_ref, w_ref, b_ref, *rest):
        o_ref, pad_ref = rest[-2:]
        _conv_seq_kernel(u_ref, w_ref, b_ref, o_ref, pad_ref, n=n, cb=cb)

    return pl.pallas_call(
        body,
        grid=(n_row_blk, n_col_blk),
        in_specs=in_specs,
        out_specs=pl.BlockSpec((n, cb), lambda i, c: (i + row_blk0, c + col_blk0)),
        out_shape=jax.ShapeDtypeStruct((ROWS, CONV_DIM), F32),
        scratch_shapes=[pltpu.VMEM((n + 2 * CONV_ROW_PAD, cb), F32)],
        input_output_aliases=aliases,
        compiler_params=_cparams(("arbitrary", "arbitrary")),
        name="conv_seq",
    )(*args)


def _conv_vert_kernel(u_ref, w_ref, b_ref, prev_ref, o_ref, pad_ref):
    del prev_ref
    rpg = ROWS_PER_GRID_ROW
    zeros = jnp.zeros((CONV_PAD * rpg, LANES), F32)
    pad_ref[0:CONV_PAD * rpg, :] = zeros
    pad_ref[(CONV_PAD + GRID_ROWS) * rpg:, :] = zeros
    pad_ref[CONV_PAD * rpg:(CONV_PAD + GRID_ROWS) * rpg, :] = u_ref[...]
    bias = jnp.broadcast_to(b_ref[...], (CONV_SUB_ROWS, LANES))

    def body(r, carry):
        for s in range(rpg // CONV_SUB_ROWS):
            acc = bias
            for k in range(CONV_K):
                off = pl.multiple_of((r + k) * rpg + s * CONV_SUB_ROWS, CONV_SUB_ROWS)
                acc = acc + w_ref[k:k + 1, :] * pad_ref[pl.ds(off, CONV_SUB_ROWS), :]
            o0 = pl.multiple_of(r * rpg + s * CONV_SUB_ROWS, CONV_SUB_ROWS)
            o_ref[pl.ds(o0, CONV_SUB_ROWS), :] = acc
        return carry

    lax.fori_loop(0, GRID_ROWS, body, 0)


def _conv_vert(u, w, b, prev):
    half_blk = (CONV_DIM // 2) // LANES
    return pl.pallas_call(
        _conv_vert_kernel,
        grid=(half_blk,),
        in_specs=[
            pl.BlockSpec((ROWS_LAT, LANES), lambda c: (0, c + half_blk)),
            pl.BlockSpec((CONV_K, LANES), lambda c: (0, c + half_blk)),
            pl.BlockSpec((1, LANES), lambda c: (0, c + half_blk)),
            pl.BlockSpec(memory_space=pl.ANY),
        ],
        out_specs=pl.BlockSpec((ROWS_LAT, LANES), lambda c: (0, c + half_blk)),
        out_shape=jax.ShapeDtypeStruct((ROWS, CONV_DIM), F32),
        scratch_shapes=[pltpu.VMEM(((GRID_ROWS + 2 * CONV_PAD) * ROWS_PER_GRID_ROW, LANES), F32)],
        input_output_aliases={3: 0},
        compiler_params=_cparams(("arbitrary",)),
        name="conv_vert",
    )(u, w, b.reshape(1, CONV_DIM), prev)


def _ssm_kernel(uf_ref, ub_ref, bm_ref, cm_ref, lam_ref, yf_ref, yb_ref, sf_ref, sb_ref, hst_ref):
    k = pl.program_id(0)

    @pl.when(k == 0)
    def _():
        hst_ref[...] = jnp.zeros_like(hst_ref)

    uf = uf_ref[...].astype(BF16)
    ub = ub_ref[...].astype(BF16)
    for gp in range(N_GP):
        cs = slice((gp // GP_PER_SLAB) * LANES, (gp // GP_PER_SLAB + 1) * LANES)
        sf_ref[gp] = jnp.dot(uf[:, cs], bm_ref[0, gp], preferred_element_type=F32)
        sb_ref[gp] = jnp.dot(ub[:, cs], bm_ref[1, gp], preferred_element_type=F32)

    low = lax.broadcasted_iota(jnp.int32, (SUBLANES, LANES), 0) < BATCH
    n_vreg_rows = SSM_CHUNK_ROWS // SUBLANES
    re = slice(0, LANES)
    im = slice(LANES, 2 * LANES)

    def two_steps(s_ref, gp, r0, lr, li, hre, him, first_low):
        xre = s_ref[gp, pl.ds(r0, SUBLANES), re]
        xim = s_ref[gp, pl.ds(r0, SUBLANES), im]
        are = lr * hre - li * him + xre
        aim = lr * him + li * hre + xim
        rre = pltpu.roll(are, BATCH, 0)
        rim = pltpu.roll(aim, BATCH, 0)
        bre = lr * rre - li * rim + xre
        bim = lr * rim + li * rre + xim
        if first_low:
            s_ref[gp, pl.ds(r0, SUBLANES), re] = jnp.where(low, are, bre)
            s_ref[gp, pl.ds(r0, SUBLANES), im] = jnp.where(low, aim, bim)
        else:
            s_ref[gp, pl.ds(r0, SUBLANES), re] = jnp.where(low, bre, are)
            s_ref[gp, pl.ds(r0, SUBLANES), im] = jnp.where(low, bim, aim)
        return pltpu.roll(bre, BATCH, 0), pltpu.roll(bim, BATCH, 0)

    def gp_block(gb, carry):
        gps = [gb * GP_BLOCK + q for q in range(GP_BLOCK)]
        lam = [(lam_ref[0, 0, gp], lam_ref[0, 1, gp], lam_ref[1, 0, gp], lam_ref[1, 1, gp]) for gp in gps]
        st0 = tuple(hst_ref[d, gp, ri] for gp in gps for d in (0, 1) for ri in (0, 1))

        def step(i, st):
            new = []
            rf = pl.multiple_of(i * SUBLANES, SUBLANES)
            rb = pl.multiple_of((n_vreg_rows - 1 - i) * SUBLANES, SUBLANES)
            for q, gp in enumerate(gps):
                lrf, lif, lrb, lib = lam[q]
                fre, fim, gre, gim = st[4 * q:4 * q + 4]
                fre, fim = two_steps(sf_ref, gp, rf, lrf, lif, fre, fim, True)
                gre, gim = two_steps(sb_ref, gp, rb, lrb, lib, gre, gim, False)
                new += [fre, fim, gre, gim]
            return tuple(new)

        st = lax.fori_loop(0, n_vreg_rows, step, st0)
        for q, gp in enumerate(gps):
            hst_ref[0, gp, 0] = st[4 * q]
            hst_ref[0, gp, 1] = st[4 * q + 1]
            hst_ref[1, gp, 0] = st[4 * q + 2]
            hst_ref[1, gp, 1] = st[4 * q + 3]
        return carry

    lax.fori_loop(0, N_GP // GP_BLOCK, gp_block, 0)

    for j in range(SSM_DIM // LANES):
        accf = None
        accb = None
        for q in range(GP_PER_SLAB):
            gp = j * GP_PER_SLAB + q
            pf = jnp.dot(sf_ref[gp].astype(BF16), cm_ref[0, gp], preferred_element_type=F32)
            pb = jnp.dot(sb_ref[gp].astype(BF16), cm_ref[1, gp], preferred_element_type=F32)
            accf = pf if accf is None else accf + pf
            accb = pb if accb is None else accb + pb
        yf_ref[:, j * LANES:(j + 1) * LANES] = accf
        yb_ref[:, j * LANES:(j + 1) * LANES] = accb


def _ssm(us, bmat, cmat, lamtab):
    n_lat = N_CHUNK - N_CHUNK_CTX

    def fwd_blk(k):
        return jnp.where(k < N_CHUNK_CTX, n_lat + k, k - N_CHUNK_CTX)

    def bwd_blk(k):
        return N_CHUNK - 1 - k

    blk = (SSM_CHUNK_ROWS, SSM_DIM)
    return pl.pallas_call(
        _ssm_kernel,
        grid=(N_CHUNK,),
        in_specs=[
            pl.BlockSpec(blk, lambda k: (fwd_blk(k), 0)),
            pl.BlockSpec(blk, lambda k: (bwd_blk(k), 0)),
            pl.BlockSpec(bmat.shape, lambda k: (0, 0, 0, 0)),
            pl.BlockSpec(cmat.shape, lambda k: (0, 0, 0, 0)),
            pl.BlockSpec(lamtab.shape, lambda k: (0, 0, 0, 0, 0)),
        ],
        out_specs=[
            pl.BlockSpec(blk, lambda k: (fwd_blk(k), 0)),
            pl.BlockSpec(blk, lambda k: (bwd_blk(k), 0)),
        ],
        out_shape=[jax.ShapeDtypeStruct((ROWS, SSM_DIM), F32)] * 2,
        scratch_shapes=[
            pltpu.VMEM((N_GP, SSM_CHUNK_ROWS, 2 * LANES), F32),
            pltpu.VMEM((N_GP, SSM_CHUNK_ROWS, 2 * LANES), F32),
            pltpu.VMEM((2, N_GP, 2, SUBLANES, LANES), F32),
        ],
        compiler_params=_cparams(("arbitrary",)),
        name="s5_scan",
    )(us, us, bmat, cmat, lamtab)


def _ssm_tables(a_re, a_im, log_dt, b_re, b_im, c_re, c_im):
    lam = lax.complex(a_re.astype(F32), a_im.astype(F32))
    dt = jnp.exp(log_dt.astype(F32))[..., None]
    lam_bar = jnp.exp(lam * dt)
    bmat_c = lax.complex(b_re.astype(F32), b_im.astype(F32))
    b_bar = ((lam_bar - 1) / lam)[..., None] * bmat_c
    g = jnp.arange(SSM_GROUPS)
    pos = jax.nn.one_hot(g % GROUPS_PER_SLAB, GROUPS_PER_SLAB, dtype=F32)
    par = jax.nn.one_hot(g % 2, 2, dtype=F32)
    bg = jnp.stack([jnp.real(b_bar), jnp.imag(b_bar)], axis=2)
    bfull = jnp.einsum("gs,gq,dgrpc->dgscrqp", pos, par, bg)
    bfull = bfull.reshape(2, N_GP, 2, LANES, 2 * LANES).sum(axis=2)
    cg = jnp.stack([c_re.astype(F32), -c_im.astype(F32)], axis=2)
    cfull = jnp.einsum("gs,gq,dgrcp->dgrqpsc", pos, par, cg)
    cfull = cfull.reshape(2, N_GP, 2, 2 * LANES, LANES).sum(axis=2)
    lam_ri = jnp.stack([jnp.real(lam_bar), jnp.imag(lam_bar)], axis=1)
    lam_ri = lam_ri.reshape(2, 2, N_GP, 1, LANES)
    lamtab = jnp.broadcast_to(lam_ri, (2, 2, N_GP, SUBLANES, LANES))
    return bfull.astype(BF16), cfull.astype(BF16), lamtab


def _mix_kernel(cv_ref, yf_ref, yb_ref, us_ref, gt_ref, lng_ref, lnb_ref, sd_ref,
                cwo_ref, wglu_ref, bglu_ref, o_ref):
    cv = cv_ref[...]
    mu = jnp.mean(cv, axis=-1, keepdims=True)
    var = jnp.mean(jnp.square(cv - mu), axis=-1, keepdims=True)
    ln = (cv - mu) * lax.rsqrt(var + EPS) * lng_ref[...] + lnb_ref[...]
    y_conv = jnp.dot(jax.nn.silu(ln).astype(BF16), cwo_ref[...], preferred_element_type=F32)
    y = yf_ref[...] + yb_ref[...] + sd_ref[...] * us_ref[...]
    z = jnp.dot(jax.nn.gelu(y).astype(BF16), wglu_ref[...], preferred_element_type=F32) + bglu_ref[...]
    y_ssm = z[:, :D_MODEL] * jax.nn.sigmoid(z[:, D_MODEL:])
    gt = gt_ref[...]
    o_ref[...] = (gt[:, :D_MODEL] * y_conv + gt[:, D_MODEL:] * y_ssm).astype(BF16)


def _mix(cv, yf, yb, us, gt, ln_g, ln_b, ssm_d, cwo, wglu, bglu, *, tm, latent_only):
    n_tiles = (ROWS_LAT if latent_only else ROWS) // tm
    row = lambda w: pl.BlockSpec((tm, w), lambda i: (i, 0))
    const = lambda shape: pl.BlockSpec(shape, lambda i: (0, 0), pipeline_mode=pl.Buffered(1))
    return pl.pallas_call(
        _mix_kernel,
        grid=(n_tiles,),
        in_specs=[
            row(CONV_DIM), row(SSM_DIM), row(SSM_DIM), row(SSM_DIM), row(2 * D_MODEL),
            const((1, CONV_DIM)), const((1, CONV_DIM)), const((1, SSM_DIM)),
            const((CONV_DIM, D_MODEL)), const((SSM_DIM, 2 * D_MODEL)), const((1, 2 * D_MODEL)),
        ],
        out_specs=row(D_MODEL),
        out_shape=jax.ShapeDtypeStruct((ROWS, D_MODEL), BF16),
        compiler_params=_cparams(("arbitrary",)),
        name="mixer_mix",
    )(cv, yf, yb, us, gt, ln_g.reshape(1, -1), ln_b.reshape(1, -1), ssm_d.reshape(1, -1),
      cwo, wglu, bglu.reshape(1, -1))


def _outproj_kernel(x_ref, mix_ref, mod_ref, w_ref, o_ref):
    y = jnp.dot(mix_ref[...], w_ref[...], preferred_element_type=F32)
    o_ref[...] = x_ref[...] + _rows8(y, mod_ref[2], jnp.multiply)


def _outproj(x, mix, modtab, w_out, *, tm, latent_only):
    n_lat_tiles = ROWS_LAT // tm
    n_tiles = n_lat_tiles if latent_only else ROWS // tm
    return pl.pallas_call(
        _outproj_kernel,
        grid=(n_tiles,),
        in_specs=[
            pl.BlockSpec((tm, D_MODEL), lambda i: (i, 0)),
            pl.BlockSpec((tm, D_MODEL), lambda i: (i, 0)),
            pl.BlockSpec((None, 3, SUBLANES, D_MODEL),
                         lambda i: (jnp.where(i < n_lat_tiles, 1, 0), 1, 0, 0)),
            pl.BlockSpec((D_MODEL, D_MODEL), lambda i: (0, 0)),
        ],
        out_specs=pl.BlockSpec((tm, D_MODEL), lambda i: (i, 0)),
        out_shape=jax.ShapeDtypeStruct((ROWS, D_MODEL), F32),
        input_output_aliases={0: 0},
        compiler_params=_cparams(("arbitrary",)),
        name="mixer_outproj",
    )(x, mix, modtab, w_out)


def kernel(x, c, ctx, c_ctx, ada_w, ada_b, norm_ffn1, ffn1_w_up, ffn1_w_down, norm_mix, w_in, b_in, conv_dw, conv_db, conv_ln_g, conv_ln_b, conv_w_out, ssm_a_re, ssm_a_im, ssm_log_dt, ssm_b_re, ssm_b_im, ssm_c_re, ssm_c_im, ssm_d, ssm_w_glu, ssm_b_glu, w_out, norm_ffn2, ffn2_w_up, ffn2_w_down, norm_final):
    xs = jnp.concatenate([
        x.transpose(1, 0, 2).reshape(ROWS_LAT, D_MODEL),
        ctx.transpose(1, 0, 2).reshape(ROWS_CTX, D_MODEL)], axis=0)

    c8 = jnp.zeros((SUBLANES, D_MODEL), F32).at[:BATCH].set(c).at[BATCH].set(c_ctx)
    mod = _ada(c8, ada_w, ada_b).reshape(DEPTH, SUBLANES, N_MOD, D_MODEL)
    lat_rows = jnp.arange(SUBLANES) % BATCH
    mod_lat = mod[:, lat_rows].transpose(0, 2, 1, 3)
    mod_ctx = jnp.broadcast_to(mod[:, BATCH][:, :, None, :], mod_lat.shape)
    modtab = jnp.stack([mod_ctx, mod_lat], axis=1)

    for l in range(DEPTH):
        last = l == DEPTH - 1
        mt = modtab[l]
        xs = _ffn(xs, mt, 0, norm_ffn1[l], ffn1_w_up[l].astype(BF16), ffn1_w_down[l].astype(BF16),
                  tm=512, tf=512)
        uc, us, gt = _inproj(xs, mt, norm_mix[l], w_in[l].astype(BF16), b_in[l], tm=512)

        cv = _conv_seq(uc, conv_dw[l], conv_db[l], n=ROWS_PER_GRID_ROW, cb=CONV_DIM // 2,
                       row_blk0=0, n_row_blk=GRID_ROWS, col_blk0=0, n_col_blk=1)
        cv = _conv_vert(uc, conv_dw[l], conv_db[l], cv)
        if not last:
            cv = _conv_seq(uc, conv_dw[l], conv_db[l], n=ROWS_CTX, cb=LANES,
                           row_blk0=ROWS_LAT // ROWS_CTX, n_row_blk=1, col_blk0=0,
                           n_col_blk=CONV_DIM // LANES, prev=cv)

        bmat, cmat, lamtab = _ssm_tables(ssm_a_re[l], ssm_a_im[l], ssm_log_dt[l],
                                         ssm_b_re[l], ssm_b_im[l], ssm_c_re[l], ssm_c_im[l])
        yf, yb = _ssm(us, bmat, cmat, lamtab)

        mix = _mix(cv, yf, yb, us, gt, conv_ln_g[l], conv_ln_b[l], ssm_d[l],
                   conv_w_out[l].astype(BF16), ssm_w_glu[l].astype(BF16), ssm_b_glu[l],
                   tm=256, latent_only=last)
        xs = _outproj(xs, mix, mt, w_out[l].astype(BF16), tm=512, latent_only=last)
        xs = _ffn(xs, mt, 2, norm_ffn2[l], ffn2_w_up[l].astype(BF16), ffn2_w_down[l].astype(BF16),
                  tm=512, tf=512, latent_only=last, final_gain=norm_final if last else None)

    return xs.reshape(SEQ, BATCH, D_MODEL).transpose(1, 0, 2)
```

```python
import functools

import jax
import jax.numpy as jnp
from jax import lax
from jax.experimental import pallas as pl
from jax.experimental.pallas import tpu as pltpu

F32 = jnp.float32
BF16 = jnp.bfloat16

D_MODEL = 2048
BATCH = 4
SEQ = 2048
CTX_LEN = 256
DEPTH = 2
GRID_W = 64
GRID_ROWS = SEQ // GRID_W
D_FF = 5632
CONV_DIM = 1024
CONV_K = 31
CONV_PAD = CONV_K // 2
SSM_DIM = 1024
SSM_GROUP = 16
SSM_GROUPS = 64
SSM_STATE = 64
N_MOD = 9
N_IN = 2 * CONV_DIM + SSM_DIM + 2 * D_MODEL
FFN_RES = 0.5
EPS = 1e-6

LANES = 128
SUBLANES = 8
ROWS_CTX = CTX_LEN * BATCH
ROWS_LAT = SEQ * BATCH
ROWS = ROWS_CTX + ROWS_LAT
ROWS_PER_GRID_ROW = GRID_W * BATCH

VMEM_LIMIT = 56 * 1024 * 1024

GROUPS_PER_SLAB = LANES // SSM_GROUP
N_GP = SSM_GROUPS // 2
GP_PER_SLAB = GROUPS_PER_SLAB // 2
N_SLAB = SSM_DIM // LANES
SSM_CHUNK_T = 64
SSM_CHUNK_ROWS = SSM_CHUNK_T * BATCH
SSM_STEP_ROWS = 2 * SSM_CHUNK_ROWS
N_CHUNK_CTX = CTX_LEN // SSM_CHUNK_T
N_CHUNK = (CTX_LEN + SEQ) // SSM_CHUNK_T
GP_BLOCK = 8
assert 2 * BATCH == SUBLANES


def _cparams(sem):
    return pltpu.CompilerParams(dimension_semantics=sem, vmem_limit_bytes=VMEM_LIMIT)


def _const_spec(shape):
    nd = len(shape)
    return pl.BlockSpec(shape, lambda *_: (0,) * nd, pipeline_mode=pl.Buffered(1))


def _rows8(x, m, op):
    n, d = x.shape
    x3 = x.reshape(n // SUBLANES, SUBLANES, d)
    return op(x3, m[None]).reshape(n, d)


def _interleave_batch(src_ref, o_ref, slab_ref, tq):
    for c in range(D_MODEL // LANES):
        cs = slice(c * LANES, (c + 1) * LANES)
        for b in range(BATCH):
            slab_ref[c, pl.ds(b, tq, stride=BATCH), :] = src_ref[b, :, cs]
        o_ref[:, cs] = slab_ref[c]


def _to_rows_kernel(x_ref, c_ref, o_ref, slab_ref, *, n_lat_tiles, tq):
    i = pl.program_id(0)

    @pl.when(i < n_lat_tiles)
    def _():
        _interleave_batch(x_ref, o_ref, slab_ref, tq)

    @pl.when(i >= n_lat_tiles)
    def _():
        _interleave_batch(c_ref, o_ref, slab_ref, tq)


def _to_rows(x, ctx):
    tq = 64
    n_lat_tiles = SEQ // tq
    n_tiles = (SEQ + CTX_LEN) // tq
    return pl.pallas_call(
        functools.partial(_to_rows_kernel, n_lat_tiles=n_lat_tiles, tq=tq),
        grid=(n_tiles,),
        in_specs=[
            pl.BlockSpec((BATCH, tq, D_MODEL), lambda i: (0, jnp.minimum(i, n_lat_tiles - 1), 0)),
            pl.BlockSpec((BATCH, tq, D_MODEL), lambda i: (0, jnp.maximum(i - n_lat_tiles, 0), 0)),
        ],
        out_specs=pl.BlockSpec((tq * BATCH, D_MODEL), lambda i: (i, 0)),
        out_shape=jax.ShapeDtypeStruct((ROWS, D_MODEL), F32),
        scratch_shapes=[pltpu.VMEM((D_MODEL // LANES, tq * BATCH, LANES), F32)],
        compiler_params=_cparams(("arbitrary",)),
        name="to_rows",
    )(x, ctx)


def _ada_kernel(c_ref, w_ref, b_ref, o_ref):
    c = c_ref[...]
    s = jax.nn.silu(c).astype(BF16)
    o_ref[0] = jnp.dot(s, w_ref[0].astype(BF16), preferred_element_type=F32) + b_ref[0]


def _ada(c8, ada_w, ada_b):
    tn = 1024
    n = N_MOD * D_MODEL
    return pl.pallas_call(
        _ada_kernel,
        grid=(DEPTH, n // tn),
        in_specs=[
            pl.BlockSpec((SUBLANES, D_MODEL), lambda l, j: (0, 0)),
            pl.BlockSpec((1, D_MODEL, tn), lambda l, j: (l, 0, j)),
            pl.BlockSpec((1, 1, tn), lambda l, j: (l, 0, j)),
        ],
        out_specs=pl.BlockSpec((1, SUBLANES, tn), lambda l, j: (l, 0, j)),
        out_shape=jax.ShapeDtypeStruct((DEPTH, SUBLANES, n), F32),
        compiler_params=_cparams(("arbitrary", "arbitrary")),
        name="ada_mod",
    )(c8, ada_w, ada_b.reshape(DEPTH, 1, n))


def _ffn_kernel(x_ref, mod_ref, gain_ref, wa_ref, wg_ref, wd_ref, *rest, nf, final):
    if final:
        fgain_ref, o_ref, hn_ref, acc_ref, slab_ref = rest
    else:
        o_ref, hn_ref = rest
        acc_ref = o_ref
    j = pl.program_id(1)

    @pl.when(j == 0)
    def _():
        x = x_ref[...]
        y = x * lax.rsqrt(jnp.mean(x * x, axis=-1, keepdims=True) + EPS) * gain_ref[...]
        h = _rows8(y, 1.0 + mod_ref[1], jnp.multiply)
        h = _rows8(h, mod_ref[0], jnp.add)
        hn_ref[...] = h.astype(BF16)
        acc_ref[...] = jnp.zeros_like(acc_ref)

    hn = hn_ref[...]
    a = jnp.dot(hn, wa_ref[...], preferred_element_type=F32)
    g = jnp.dot(hn, wg_ref[...], preferred_element_type=F32)
    act = (jax.nn.silu(g) * a).astype(BF16)
    acc_ref[...] += jnp.dot(act, wd_ref[...], preferred_element_type=F32)

    @pl.when(j == nf - 1)
    def _():
        upd = _rows8(acc_ref[...], mod_ref[2] * FFN_RES, jnp.multiply)
        xn = x_ref[...] + upd
        if final:
            xn = xn * lax.rsqrt(jnp.mean(xn * xn, axis=-1, keepdims=True) + EPS) * fgain_ref[...]
            tq = acc_ref.shape[0] // BATCH
            for c in range(D_MODEL // LANES):
                cs = slice(c * LANES, (c + 1) * LANES)
                slab_ref[c] = xn[:, cs]
                for b in range(BATCH):
                    o_ref[b, :, cs] = slab_ref[c, pl.ds(b, tq, stride=BATCH), :]
        else:
            o_ref[...] = xn


def _ffn(x, modtab, sub, gain, w_up, w_down, layer, *, tm, tf, latent_only=False, final_gain=None):
    nf = D_FF // tf
    n_lat_tiles = ROWS_LAT // tm
    n_tiles = n_lat_tiles if latent_only else ROWS // tm
    final = final_gain is not None
    in_specs = [
        pl.BlockSpec((tm, D_MODEL), lambda i, j: (i, 0)),
        pl.BlockSpec((None, 3, SUBLANES, D_MODEL),
                     lambda i, j: (jnp.where(i < n_lat_tiles, 1, 0), sub, 0, 0)),
        pl.BlockSpec((1, D_MODEL), lambda i, j: (0, 0)),
        pl.BlockSpec((None, D_MODEL, tf), lambda i, j: (layer, 0, j)),
        pl.BlockSpec((None, D_MODEL, tf), lambda i, j: (layer, 0, j + nf)),
        pl.BlockSpec((None, tf, D_MODEL), lambda i, j: (layer, j, 0)),
    ]
    args = [x, modtab, gain.reshape(1, D_MODEL), w_up, w_up, w_down]
    scratch = [pltpu.VMEM((tm, D_MODEL), BF16)]
    if final:
        assert latent_only
        in_specs.append(pl.BlockSpec((1, D_MODEL), lambda i, j: (0, 0)))
        args.append(final_gain.reshape(1, D_MODEL))
        scratch.append(pltpu.VMEM((tm, D_MODEL), F32))
        scratch.append(pltpu.VMEM((D_MODEL // LANES, tm, LANES), F32))
        out_shape = jax.ShapeDtypeStruct((BATCH, SEQ, D_MODEL), F32)
        out_spec = pl.BlockSpec((BATCH, tm // BATCH, D_MODEL), lambda i, j: (0, i, 0))
        aliases = {}
    else:
        out_shape = jax.ShapeDtypeStruct((ROWS, D_MODEL), F32)
        out_spec = pl.BlockSpec((tm, D_MODEL), lambda i, j: (i, 0))
        aliases = {0: 0}
    return pl.pallas_call(
        functools.partial(_ffn_kernel, nf=nf, final=final),
        grid=(n_tiles, nf),
        in_specs=in_specs,
        out_specs=out_spec,
        out_shape=out_shape,
        scratch_shapes=scratch,
        input_output_aliases=aliases,
        compiler_params=_cparams(("arbitrary", "arbitrary")),
        name="ffn_final" if final else "ffn",
    )(*args)


def _inproj_kernel(x_ref, mod_ref, gain_ref, w_ref, wg_ref, b_ref, bg_ref,
                   uc_ref, us_ref, gt_ref, hn_ref):
    j = pl.program_id(1)

    @pl.when(j == 0)
    def _():
        x = x_ref[...]
        y = x * lax.rsqrt(jnp.mean(x * x, axis=-1, keepdims=True) + EPS) * gain_ref[...]
        h = _rows8(y, 1.0 + mod_ref[1], jnp.multiply)
        h = _rows8(h, mod_ref[0], jnp.add)
        hn_ref[...] = h.astype(BF16)

    hn = hn_ref[...]
    z = jnp.dot(hn, w_ref[...], preferred_element_type=F32) + b_ref[0]

    @pl.when(j == 0)
    def _():
        zg = jnp.dot(hn, wg_ref[...], preferred_element_type=F32) + bg_ref[0]
        uc_ref[...] = z * jax.nn.sigmoid(zg)

    @pl.when(j == 1)
    def _():
        us_ref[...] = z

    @pl.when(j >= 2)
    def _():
        gt_ref[...] = jax.nn.sigmoid(z).astype(BF16)


def _inproj(x, modtab, gain, w_in, b_in, layer, *, tm):
    tn = 1024
    n_lat_tiles = ROWS_LAT // tm
    nblk = N_IN // tn
    wblk = lambda j: jnp.where(j == 0, 0, j + 1)
    b3 = b_in.reshape(nblk, 1, tn)
    return pl.pallas_call(
        _inproj_kernel,
        grid=(ROWS // tm, nblk - 1),
        in_specs=[
            pl.BlockSpec((tm, D_MODEL), lambda i, j: (i, 0)),
            pl.BlockSpec((None, 3, SUBLANES, D_MODEL),
                         lambda i, j: (jnp.where(i < n_lat_tiles, 1, 0), 1, 0, 0)),
            pl.BlockSpec((1, D_MODEL), lambda i, j: (0, 0)),
            pl.BlockSpec((None, D_MODEL, tn), lambda i, j: (layer, 0, wblk(j))),
            pl.BlockSpec((None, D_MODEL, tn), lambda i, j: (layer, 0, 1)),
            pl.BlockSpec((1, 1, tn), lambda i, j: (wblk(j), 0, 0)),
            pl.BlockSpec((1, 1, tn), lambda i, j: (1, 0, 0)),
        ],
        out_specs=[
            pl.BlockSpec((tm, tn), lambda i, j: (i, 0)),
            pl.BlockSpec((tm, tn), lambda i, j: (i, 0)),
            pl.BlockSpec((tm, tn), lambda i, j: (i, jnp.maximum(j - 2, 0))),
        ],
        out_shape=[
            jax.ShapeDtypeStruct((ROWS, CONV_DIM), F32),
            jax.ShapeDtypeStruct((ROWS, SSM_DIM), F32),
            jax.ShapeDtypeStruct((ROWS, 2 * D_MODEL), BF16),
        ],
        scratch_shapes=[pltpu.VMEM((tm, D_MODEL), BF16)],
        compiler_params=_cparams(("arbitrary", "arbitrary")),
        name="mixer_inproj",
    )(x, modtab, gain.reshape(1, D_MODEL), w_in, w_in, b3, b3)


CONV_ROW_PAD = 64
CONV_SUB_ROWS = 64


def _conv_seq_kernel(u_ref, w_ref, b_ref, o_ref, pad_ref, *, n, cb):
    zeros = jnp.zeros((CONV_ROW_PAD, cb), F32)
    pad_ref[0:CONV_ROW_PAD, :] = zeros
    pad_ref[CONV_ROW_PAD + n:, :] = zeros
    pad_ref[CONV_ROW_PAD:CONV_ROW_PAD + n, :] = u_ref[...]
    base = CONV_ROW_PAD - CONV_PAD * BATCH
    for c in range(cb // LANES):
        cs = slice(c * LANES, (c + 1) * LANES)
        for r in range(n // CONV_SUB_ROWS):
            r0 = r * CONV_SUB_ROWS
            acc = jnp.broadcast_to(b_ref[:, cs], (CONV_SUB_ROWS, LANES))
            for k in range(CONV_K):
                off = base + r0 + k * BATCH
                acc = acc + w_ref[k:k + 1, cs] * pad_ref[off:off + CONV_SUB_ROWS, cs]
            o_ref[r0:r0 + CONV_SUB_ROWS, cs] = acc


def _conv_seq(u, w, b, *, n, cb, row_blk0, n_row_blk, col_blk0, n_col_blk, prev=None):
    in_specs = [
        pl.BlockSpec((n, cb), lambda i, c: (i + row_blk0, c + col_blk0)),
        pl.BlockSpec((CONV_K, cb), lambda i, c: (0, c + col_blk0)),
        pl.BlockSpec((1, cb), lambda i, c: (0, c + col_blk0)),
    ]
    args = [u, w, b.reshape(1, CONV_DIM)]
    aliases = {}
    if prev is not None:
        in_specs.append(pl.BlockSpec(memory_space=pl.ANY))
        args.append(prev)
        aliases = {3: 0}

    def body(u_ref, w_ref, b_ref, *rest):
        o_ref, pad_ref = rest[-2:]
        _conv_seq_kernel(u_ref, w_ref, b_ref, o_ref, pad_ref, n=n, cb=cb)

    return pl.pallas_call(
        body,
        grid=(n_row_blk, n_col_blk),
        in_specs=in_specs,
        out_specs=pl.BlockSpec((n, cb), lambda i, c: (i + row_blk0, c + col_blk0)),
        out_shape=jax.ShapeDtypeStruct((ROWS, CONV_DIM), F32),
        scratch_shapes=[pltpu.VMEM((n + 2 * CONV_ROW_PAD, cb), F32)],
        input_output_aliases=aliases,
        compiler_params=_cparams(("arbitrary", "arbitrary")),
        name="conv_seq",
    )(*args)


def _conv_vert_kernel(u_ref, w_ref, b_ref, prev_ref, o_ref, pad_ref):
    del prev_ref
    rpg = ROWS_PER_GRID_ROW
    zeros = jnp.zeros((CONV_PAD * rpg, LANES), F32)
    pad_ref[0:CONV_PAD * rpg, :] = zeros
    pad_ref[(CONV_PAD + GRID_ROWS) * rpg:, :] = zeros
    pad_ref[CONV_PAD * rpg:(CONV_PAD + GRID_ROWS) * rpg, :] = u_ref[...]
    bias = jnp.broadcast_to(b_ref[...], (CONV_SUB_ROWS, LANES))

    def body(r, carry):
        for s in range(rpg // CONV_SUB_ROWS):
            acc = bias
            for k in range(CONV_K):
                off = pl.multiple_of((r + k) * rpg + s * CONV_SUB_ROWS, CONV_SUB_ROWS)
                acc = acc + w_ref[k:k + 1, :] * pad_ref[pl.ds(off, CONV_SUB_ROWS), :]
            o0 = pl.multiple_of(r * rpg + s * CONV_SUB_ROWS, CONV_SUB_ROWS)
            o_ref[pl.ds(o0, CONV_SUB_ROWS), :] = acc
        return carry

    lax.fori_loop(0, GRID_ROWS, body, 0)


def _conv_vert(u, w, b, prev):
    half_blk = (CONV_DIM // 2) // LANES
    return pl.pallas_call(
        _conv_vert_kernel,
        grid=(half_blk,),
        in_specs=[
            pl.BlockSpec((ROWS_LAT, LANES), lambda c: (0, c + half_blk)),
            pl.BlockSpec((CONV_K, LANES), lambda c: (0, c + half_blk)),
            pl.BlockSpec((1, LANES), lambda c: (0, c + half_blk)),
            pl.BlockSpec(memory_space=pl.ANY),
        ],
        out_specs=pl.BlockSpec((ROWS_LAT, LANES), lambda c: (0, c + half_blk)),
        out_shape=jax.ShapeDtypeStruct((ROWS, CONV_DIM), F32),
        scratch_shapes=[pltpu.VMEM(((GRID_ROWS + 2 * CONV_PAD) * ROWS_PER_GRID_ROW, LANES), F32)],
        input_output_aliases={3: 0},
        compiler_params=_cparams(("arbitrary",)),
        name="conv_vert",
    )(u, w, b.reshape(1, CONV_DIM), prev)


def _ssm_kernel(uf_ref, ub_ref, b2_ref, c2_ref, lam_ref, yf_ref, yb_ref,
                lhs_ref, s_ref, y2_ref, hst_ref):
    k = pl.program_id(0)

    @pl.when(k == 0)
    def _():
        hst_ref[...] = jnp.zeros_like(hst_ref)

    n_pair = SSM_CHUNK_T // 2
    low = lax.broadcasted_iota(jnp.int32, (SUBLANES, SSM_DIM), 0) < BATCH
    zero = jnp.zeros((SUBLANES, SSM_DIM), F32)
    for m in range(n_pair):
        f = uf_ref[m * SUBLANES:(m + 1) * SUBLANES, :]
        mb = n_pair - 1 - m
        g = ub_ref[mb * SUBLANES:(mb + 1) * SUBLANES, :]
        fr = pltpu.roll(f, BATCH, 0)
        gr = pltpu.roll(g, BATCH, 0)
        fwd = jnp.concatenate([jnp.where(low, f, zero), jnp.where(low, fr, zero)], axis=0).astype(BF16)
        bwd = jnp.concatenate([jnp.where(low, zero, g), jnp.where(low, zero, gr)], axis=0).astype(BF16)
        r0 = m * 2 * SUBLANES
        for j in range(N_SLAB):
            cs = slice(j * LANES, (j + 1) * LANES)
            lhs_ref[r0:r0 + 2 * SUBLANES, 2 * j * LANES:(2 * j + 1) * LANES] = fwd[:, cs]
            lhs_ref[r0:r0 + 2 * SUBLANES, (2 * j + 1) * LANES:(2 * j + 2) * LANES] = bwd[:, cs]

    for gp in range(N_GP):
        j = gp // GP_PER_SLAB
        s_ref[gp] = jnp.dot(lhs_ref[:, 2 * j * LANES:(2 * j + 2) * LANES], b2_ref[gp],
                            preferred_element_type=F32)

    re = slice(0, LANES)
    im = slice(LANES, 2 * LANES)

    def gp_block(gb, carry):
        gps = [gb * GP_BLOCK + q for q in range(GP_BLOCK)]
        lam = [(lam_ref[gp, 0], lam_ref[gp, 1]) for gp in gps]
        st0 = tuple(hst_ref[gp, ri] for gp in gps for ri in (0, 1))

        def step(s, st):
            r0 = pl.multiple_of(s * SUBLANES, SUBLANES)
            new = []
            for q, gp in enumerate(gps):
                lr, li = lam[q]
                hre, him = st[2 * q], st[2 * q + 1]
                nre = lr * hre - li * him + s_ref[gp, pl.ds(r0, SUBLANES), re]
                nim = lr * him + li * hre + s_ref[gp, pl.ds(r0, SUBLANES), im]
                s_ref[gp, pl.ds(r0, SUBLANES), re] = nre
                s_ref[gp, pl.ds(r0, SUBLANES), im] = nim
                new += [nre, nim]
            return tuple(new)

        st = lax.fori_loop(0, SSM_CHUNK_T, step, st0, unroll=2)
        for q, gp in enumerate(gps):
            hst_ref[gp, 0] = st[2 * q]
            hst_ref[gp, 1] = st[2 * q + 1]
        return carry

    lax.fori_loop(0, N_GP // GP_BLOCK, gp_block, 0)

    for j in range(N_SLAB):
        acc = None
        for q in range(GP_PER_SLAB):
            gp = j * GP_PER_SLAB + q
            p = jnp.dot(s_ref[gp].astype(BF16), c2_ref[gp], preferred_element_type=F32)
            acc = p if acc is None else acc + p
        y2_ref[:, 2 * j * LANES:(2 * j + 2) * LANES] = acc

    low1 = lax.broadcasted_iota(jnp.int32, (SUBLANES, LANES), 0) < BATCH
    for m in range(n_pair):
        mb = n_pair - 1 - m
        r0 = m * 2 * SUBLANES
        for j in range(N_SLAB):
            cf = slice(2 * j * LANES, (2 * j + 1) * LANES)
            cb = slice((2 * j + 1) * LANES, (2 * j + 2) * LANES)
            cs = slice(j * LANES, (j + 1) * LANES)
            ef = y2_ref[r0:r0 + SUBLANES, cf]
            of = pltpu.roll(y2_ref[r0 + SUBLANES:r0 + 2 * SUBLANES, cf], BATCH, 0)
            yf_ref[m * SUBLANES:(m + 1) * SUBLANES, cs] = jnp.where(low1, ef, of)
            eb = y2_ref[r0:r0 + SUBLANES, cb]
            ob = pltpu.roll(y2_ref[r0 + SUBLANES:r0 + 2 * SUBLANES, cb], BATCH, 0)
            yb_ref[mb * SUBLANES:(mb + 1) * SUBLANES, cs] = jnp.where(low1, ob, eb)


def _ssm(us, b2, c2, lamtab):
    n_lat = N_CHUNK - N_CHUNK_CTX

    def fwd_blk(k):
        return jnp.where(k < N_CHUNK_CTX, n_lat + k, k - N_CHUNK_CTX)

    def bwd_blk(k):
        return N_CHUNK - 1 - k

    blk = (SSM_CHUNK_ROWS, SSM_DIM)
    return pl.pallas_call(
        _ssm_kernel,
        grid=(N_CHUNK,),
        in_specs=[
            pl.BlockSpec(blk, lambda k: (fwd_blk(k), 0)),
            pl.BlockSpec(blk, lambda k: (bwd_blk(k), 0)),
            _const_spec(b2.shape),
            _const_spec(c2.shape),
            _const_spec(lamtab.shape),
        ],
        out_specs=[
            pl.BlockSpec(blk, lambda k: (fwd_blk(k), 0)),
            pl.BlockSpec(blk, lambda k: (bwd_blk(k), 0)),
        ],
        out_shape=[jax.ShapeDtypeStruct((ROWS, SSM_DIM), F32)] * 2,
        scratch_shapes=[
            pltpu.VMEM((SSM_STEP_ROWS, 2 * SSM_DIM), BF16),
            pltpu.VMEM((N_GP, SSM_STEP_ROWS, 2 * LANES), F32),
            pltpu.VMEM((SSM_STEP_ROWS, 2 * SSM_DIM), F32),
            pltpu.VMEM((N_GP, 2, SUBLANES, LANES), F32),
        ],
        compiler_params=_cparams(("arbitrary",)),
        name="s5_scan",
    )(us, us, b2, c2, lamtab)


def _ssm_tables(a_re, a_im, log_dt, b_re, b_im, c_re, c_im):
    lam = lax.complex(a_re.astype(F32), a_im.astype(F32))
    dt = jnp.exp(log_dt.astype(F32))[..., None]
    lam_bar = jnp.exp(lam * dt)
    bmat_c = lax.complex(b_re.astype(F32), b_im.astype(F32))
    b_bar = ((lam_bar - 1) / lam)[..., None] * bmat_c

    def pair_rows(v):
        v = v.reshape(2, N_GP, 2, SSM_GROUP, SSM_STATE).transpose(1, 0, 2, 3, 4)
        z = jnp.zeros_like(v[:, :, 0])
        q0 = jnp.concatenate([v[:, :, 0], z], axis=-1)
        q1 = jnp.concatenate([z, v[:, :, 1]], axis=-1)
        return jnp.stack([q0, q1], axis=2)

    slab_pos = (jnp.arange(N_GP) % GP_PER_SLAB)[:, None] == jnp.arange(GP_PER_SLAB)[None, :]
    place = slab_pos.astype(F32)[:, None, :, None, None, None]

    def place_rows(re_v, im_v):
        t = jnp.concatenate([pair_rows(re_v), pair_rows(im_v)], axis=-1)
        t = t[:, :, None] * place
        return t.reshape(N_GP, 2 * LANES, 2 * LANES)

    bt = jnp.swapaxes(b_bar, -1, -2)
    b2 = place_rows(jnp.real(bt), jnp.imag(bt))
    c2 = jnp.swapaxes(place_rows(c_re.astype(F32), -c_im.astype(F32)), 1, 2)
    lam_ri = jnp.stack([jnp.real(lam_bar), jnp.imag(lam_bar)], axis=0)
    lam_ri = lam_ri.reshape(2, 2, N_GP, 1, LANES).transpose(2, 0, 1, 3, 4)
    lamtab = jnp.broadcast_to(lam_ri, (N_GP, 2, 2, BATCH, LANES)).reshape(N_GP, 2, SUBLANES, LANES)
    return b2.astype(BF16), c2.astype(BF16), lamtab


def _mix_kernel(cv_ref, yf_ref, yb_ref, us_ref, gt_ref, lng_ref, lnb_ref, sd_ref,
                cwo_ref, wglu_ref, bglu_ref, o_ref):
    cv = cv_ref[...]
    mu = jnp.mean(cv, axis=-1, keepdims=True)
    var = jnp.mean(jnp.square(cv - mu), axis=-1, keepdims=True)
    ln = (cv - mu) * lax.rsqrt(var + EPS) * lng_ref[...] + lnb_ref[...]
    y_conv = jnp.dot(jax.nn.silu(ln).astype(BF16), cwo_ref[...], preferred_element_type=F32)
    y = yf_ref[...] + yb_ref[...] + sd_ref[...] * us_ref[...]
    z = jnp.dot(jax.nn.gelu(y).astype(BF16), wglu_ref[...], preferred_element_type=F32) + bglu_ref[...]
    y_ssm = z[:, :D_MODEL] * jax.nn.sigmoid(z[:, D_MODEL:])
    gt = gt_ref[...].astype(F32)
    o_ref[...] = (gt[:, :D_MODEL] * y_conv + gt[:, D_MODEL:] * y_ssm).astype(BF16)


def _mix(cv, yf, yb, us, gt, ln_g, ln_b, ssm_d, cwo, wglu, bglu, layer, *, tm, latent_only):
    n_tiles = (ROWS_LAT if latent_only else ROWS) // tm
    row = lambda w: pl.BlockSpec((tm, w), lambda i: (i, 0))
    vec = lambda w: pl.BlockSpec((1, w), lambda i: (0, 0), pipeline_mode=pl.Buffered(1))
    wgt = lambda r, c: pl.BlockSpec((None, r, c), lambda i: (layer, 0, 0), pipeline_mode=pl.Buffered(1))
    return pl.pallas_call(
        _mix_kernel,
        grid=(n_tiles,),
        in_specs=[
            row(CONV_DIM), row(SSM_DIM), row(SSM_DIM), row(SSM_DIM), row(2 * D_MODEL),
            vec(CONV_DIM), vec(CONV_DIM), vec(SSM_DIM),
            wgt(CONV_DIM, D_MODEL), wgt(SSM_DIM, 2 * D_MODEL), vec(2 * D_MODEL),
        ],
        out_specs=row(D_MODEL),
        out_shape=jax.ShapeDtypeStruct((ROWS, D_MODEL), BF16),
        compiler_params=_cparams(("arbitrary",)),
        name="mixer_mix",
    )(cv, yf, yb, us, gt, ln_g.reshape(1, -1), ln_b.reshape(1, -1), ssm_d.reshape(1, -1),
      cwo, wglu, bglu.reshape(1, -1))


def _outproj_kernel(x_ref, mix_ref, mod_ref, w_ref, o_ref):
    y = jnp.dot(mix_ref[...], w_ref[...], preferred_element_type=F32)
    o_ref[...] = x_ref[...] + _rows8(y, mod_ref[2], jnp.multiply)


def _outproj(x, mix, modtab, w_out, layer, *, tm, latent_only):
    n_lat_tiles = ROWS_LAT // tm
    n_tiles = n_lat_tiles if latent_only else ROWS // tm
    return pl.pallas_call(
        _outproj_kernel,
        grid=(n_tiles,),
        in_specs=[
            pl.BlockSpec((tm, D_MODEL), lambda i: (i, 0)),
            pl.BlockSpec((tm, D_MODEL), lambda i: (i, 0)),
            pl.BlockSpec((None, 3, SUBLANES, D_MODEL),
                         lambda i: (jnp.where(i < n_lat_tiles, 1, 0), 1, 0, 0)),
            pl.BlockSpec((None, D_MODEL, D_MODEL), lambda i: (layer, 0, 0)),
        ],
        out_specs=pl.BlockSpec((tm, D_MODEL), lambda i: (i, 0)),
        out_shape=jax.ShapeDtypeStruct((ROWS, D_MODEL), F32),
        input_output_aliases={0: 0},
        compiler_params=_cparams(("arbitrary",)),
        name="mixer_outproj",
    )(x, mix, modtab, w_out)


def kernel(x, c, ctx, c_ctx, ada_w, ada_b, norm_ffn1, ffn1_w_up, ffn1_w_down, norm_mix, w_in, b_in, conv_dw, conv_db, conv_ln_g, conv_ln_b, conv_w_out, ssm_a_re, ssm_a_im, ssm_log_dt, ssm_b_re, ssm_b_im, ssm_c_re, ssm_c_im, ssm_d, ssm_w_glu, ssm_b_glu, w_out, norm_ffn2, ffn2_w_up, ffn2_w_down, norm_final):
    xs = _to_rows(x, ctx)

    c8 = jnp.zeros((SUBLANES, D_MODEL), F32).at[:BATCH].set(c).at[BATCH].set(c_ctx)
    mod = _ada(c8, ada_w, ada_b).reshape(DEPTH, SUBLANES, N_MOD, D_MODEL)
    lat_rows = jnp.arange(SUBLANES) % BATCH
    mod_lat = mod[:, lat_rows].transpose(0, 2, 1, 3)
    mod_ctx = jnp.broadcast_to(mod[:, BATCH][:, :, None, :], mod_lat.shape)
    modtab = jnp.stack([mod_ctx, mod_lat], axis=1)

    up1, down1 = ffn1_w_up.astype(BF16), ffn1_w_down.astype(BF16)
    up2, down2 = ffn2_w_up.astype(BF16), ffn2_w_down.astype(BF16)
    win, cwo = w_in.astype(BF16), conv_w_out.astype(BF16)
    wglu, wout = ssm_w_glu.astype(BF16), w_out.astype(BF16)

    for l in range(DEPTH):
        last = l == DEPTH - 1
        mt = modtab[l]
        xs = _ffn(xs, mt, 0, norm_ffn1[l], up1, down1, l, tm=512, tf=512)
        uc, us, gt = _inproj(xs, mt, norm_mix[l], win, b_in[l], l, tm=512)

        cv = _conv_seq(uc, conv_dw[l], conv_db[l], n=ROWS_PER_GRID_ROW, cb=CONV_DIM // 2,
                       row_blk0=0, n_row_blk=GRID_ROWS, col_blk0=0, n_col_blk=1)
        cv = _conv_vert(uc, conv_dw[l], conv_db[l], cv)
        if not last:
            cv = _conv_seq(uc, conv_dw[l], conv_db[l], n=ROWS_CTX, cb=LANES,
                           row_blk0=ROWS_LAT // ROWS_CTX, n_row_blk=1, col_blk0=0,
                           n_col_blk=CONV_DIM // LANES, prev=cv)

        b2, c2, lamtab = _ssm_tables(ssm_a_re[l], ssm_a_im[l], ssm_log_dt[l],
                                     ssm_b_re[l], ssm_b_im[l], ssm_c_re[l], ssm_c_im[l])
        yf, yb = _ssm(us, b2, c2, lamtab)

        mix = _mix(cv, yf, yb, us, gt, conv_ln_g[l], conv_ln_b[l], ssm_d[l], cwo, wglu, ssm_b_glu[l], l,
                   tm=256, latent_only=last)
        xs = _outproj(xs, mix, mt, wout, l, tm=512, latent_only=last)
        xs = _ffn(xs, mt, 2, norm_ffn2[l], up2, down2, l, tm=512, tf=512,
                  latent_only=last, final_gain=norm_final if last else None)

    return xs
```

```python
import functools

import jax
import jax.numpy as jnp
from jax import lax
from jax.experimental import pallas as pl
from jax.experimental.pallas import tpu as pltpu

F32 = jnp.float32
BF16 = jnp.bfloat16

D_MODEL = 2048
BATCH = 4
SEQ = 2048
CTX_LEN = 256
DEPTH = 2
GRID_W = 64
GRID_ROWS = SEQ // GRID_W
D_FF = 5632
CONV_DIM = 1024
CONV_K = 31
CONV_PAD = CONV_K // 2
SSM_DIM = 1024
SSM_GROUP = 16
SSM_GROUPS = 64
SSM_STATE = 64
N_MOD = 9
N_IN = 2 * CONV_DIM + SSM_DIM + 2 * D_MODEL
FFN_RES = 0.5
EPS = 1e-6

LANES = 128
SUBLANES = 8
ROWS_CTX = CTX_LEN * BATCH
ROWS_LAT = SEQ * BATCH
ROWS = ROWS_CTX + ROWS_LAT
ROWS_PER_GRID_ROW = GRID_W * BATCH

VMEM_LIMIT = 60 * 1024 * 1024

GROUPS_PER_SLAB = LANES // SSM_GROUP
N_GP = SSM_GROUPS // 2
GP_PER_SLAB = GROUPS_PER_SLAB // 2
N_SLAB = SSM_DIM // LANES
SSM_CHUNK_T = 64
SSM_CHUNK_ROWS = SSM_CHUNK_T * BATCH
SSM_STEP_ROWS = 2 * SSM_CHUNK_ROWS
N_CHUNK_CTX = CTX_LEN // SSM_CHUNK_T
N_CHUNK = (CTX_LEN + SEQ) // SSM_CHUNK_T
GP_BLOCK = 8
assert 2 * BATCH == SUBLANES


def _cparams(sem):
    return pltpu.CompilerParams(dimension_semantics=sem, vmem_limit_bytes=VMEM_LIMIT)


def _const_spec(shape):
    nd = len(shape)
    return pl.BlockSpec(shape, lambda *_: (0,) * nd, pipeline_mode=pl.Buffered(1))


def _rows8(x, m, op):
    n, d = x.shape
    x3 = x.reshape(n // SUBLANES, SUBLANES, d)
    return op(x3, m[None]).reshape(n, d)


def _interleave_batch(src_ref, o_ref, slab_ref, tq):
    for c in range(D_MODEL // LANES):
        cs = slice(c * LANES, (c + 1) * LANES)
        for b in range(BATCH):
            slab_ref[c, pl.ds(b, tq, stride=BATCH), :] = src_ref[b, :, cs]
        o_ref[:, cs] = slab_ref[c]


def _to_rows_kernel(x_ref, c_ref, o_ref, slab_ref, *, n_lat_tiles, tq):
    i = pl.program_id(0)

    @pl.when(i < n_lat_tiles)
    def _():
        _interleave_batch(x_ref, o_ref, slab_ref, tq)

    @pl.when(i >= n_lat_tiles)
    def _():
        _interleave_batch(c_ref, o_ref, slab_ref, tq)


def _to_rows(x, ctx):
    tq = 64
    n_lat_tiles = SEQ // tq
    n_tiles = (SEQ + CTX_LEN) // tq
    return pl.pallas_call(
        functools.partial(_to_rows_kernel, n_lat_tiles=n_lat_tiles, tq=tq),
        grid=(n_tiles,),
        in_specs=[
            pl.BlockSpec((BATCH, tq, D_MODEL), lambda i: (0, jnp.minimum(i, n_lat_tiles - 1), 0)),
            pl.BlockSpec((BATCH, tq, D_MODEL), lambda i: (0, jnp.maximum(i - n_lat_tiles, 0), 0)),
        ],
        out_specs=pl.BlockSpec((tq * BATCH, D_MODEL), lambda i: (i, 0)),
        out_shape=jax.ShapeDtypeStruct((ROWS, D_MODEL), F32),
        scratch_shapes=[pltpu.VMEM((D_MODEL // LANES, tq * BATCH, LANES), F32)],
        compiler_params=_cparams(("arbitrary",)),
        name="to_rows",
    )(x, ctx)


def _ada_kernel(c_ref, w_ref, b_ref, o_ref):
    c = c_ref[...]
    s = jax.nn.silu(c).astype(BF16)
    o_ref[0] = jnp.dot(s, w_ref[0].astype(BF16), preferred_element_type=F32) + b_ref[0]


def _ada(c8, ada_w, ada_b):
    tn = 1024
    n = N_MOD * D_MODEL
    return pl.pallas_call(
        _ada_kernel,
        grid=(DEPTH, n // tn),
        in_specs=[
            pl.BlockSpec((SUBLANES, D_MODEL), lambda l, j: (0, 0)),
            pl.BlockSpec((1, D_MODEL, tn), lambda l, j: (l, 0, j)),
            pl.BlockSpec((1, 1, tn), lambda l, j: (l, 0, j)),
        ],
        out_specs=pl.BlockSpec((1, SUBLANES, tn), lambda l, j: (l, 0, j)),
        out_shape=jax.ShapeDtypeStruct((DEPTH, SUBLANES, n), F32),
        compiler_params=_cparams(("arbitrary", "arbitrary")),
        name="ada_mod",
    )(c8, ada_w, ada_b.reshape(DEPTH, 1, n))


def _ffn_kernel(x_ref, mod_ref, gain_ref, wa_ref, wg_ref, wd_ref, *rest, nf, final):
    if final:
        fgain_ref, o_ref, hn_ref, acc_ref, slab_ref = rest
    else:
        o_ref, hn_ref = rest
        acc_ref = o_ref
    j = pl.program_id(1)

    @pl.when(j == 0)
    def _():
        x = x_ref[...]
        y = x * lax.rsqrt(jnp.mean(x * x, axis=-1, keepdims=True) + EPS) * gain_ref[...]
        h = _rows8(y, 1.0 + mod_ref[1], jnp.multiply)
        h = _rows8(h, mod_ref[0], jnp.add)
        hn_ref[...] = h.astype(BF16)
        acc_ref[...] = jnp.zeros_like(acc_ref)

    def mxu_operand(w_ref):
        w = w_ref[...]
        return w if w.dtype == BF16 else w.astype(BF16)

    hn = hn_ref[...]
    a = jnp.dot(hn, mxu_operand(wa_ref), preferred_element_type=F32)
    g = jnp.dot(hn, mxu_operand(wg_ref), preferred_element_type=F32)
    act = (jax.nn.silu(g) * a).astype(BF16)
    acc_ref[...] += jnp.dot(act, mxu_operand(wd_ref), preferred_element_type=F32)

    @pl.when(j == nf - 1)
    def _():
        upd = _rows8(acc_ref[...], mod_ref[2] * FFN_RES, jnp.multiply)
        xn = x_ref[...] + upd
        if final:
            xn = xn * lax.rsqrt(jnp.mean(xn * xn, axis=-1, keepdims=True) + EPS) * fgain_ref[...]
            tq = acc_ref.shape[0] // BATCH
            for c in range(D_MODEL // LANES):
                cs = slice(c * LANES, (c + 1) * LANES)
                slab_ref[c] = xn[:, cs]
                for b in range(BATCH):
                    o_ref[b, :, cs] = slab_ref[c, pl.ds(b, tq, stride=BATCH), :]
        else:
            o_ref[...] = xn


def _ffn(x, modtab, sub, gain, w_up, w_down, layer, *, tm, tf, latent_only=False, final_gain=None):
    nf = D_FF // tf
    n_lat_tiles = ROWS_LAT // tm
    n_tiles = n_lat_tiles if latent_only else ROWS // tm
    final = final_gain is not None
    in_specs = [
        pl.BlockSpec((tm, D_MODEL), lambda i, j: (i, 0)),
        pl.BlockSpec((None, 3, SUBLANES, D_MODEL),
                     lambda i, j: (jnp.where(i < n_lat_tiles, 1, 0), sub, 0, 0)),
        pl.BlockSpec((1, D_MODEL), lambda i, j: (0, 0)),
        pl.BlockSpec((None, D_MODEL, tf), lambda i, j: (layer, 0, j)),
        pl.BlockSpec((None, D_MODEL, tf), lambda i, j: (layer, 0, j + nf)),
        pl.BlockSpec((None, tf, D_MODEL), lambda i, j: (layer, j, 0)),
    ]
    args = [x, modtab, gain.reshape(1, D_MODEL), w_up, w_up, w_down]
    scratch = [pltpu.VMEM((tm, D_MODEL), BF16)]
    if final:
        assert latent_only
        in_specs.append(pl.BlockSpec((1, D_MODEL), lambda i, j: (0, 0)))
        args.append(final_gain.reshape(1, D_MODEL))
        scratch.append(pltpu.VMEM((tm, D_MODEL), F32))
        scratch.append(pltpu.VMEM((D_MODEL // LANES, tm, LANES), F32))
        out_shape = jax.ShapeDtypeStruct((BATCH, SEQ, D_MODEL), F32)
        out_spec = pl.BlockSpec((BATCH, tm // BATCH, D_MODEL), lambda i, j: (0, i, 0))
        aliases = {}
    else:
        out_shape = jax.ShapeDtypeStruct((ROWS, D_MODEL), F32)
        out_spec = pl.BlockSpec((tm, D_MODEL), lambda i, j: (i, 0))
        aliases = {0: 0}
    return pl.pallas_call(
        functools.partial(_ffn_kernel, nf=nf, final=final),
        grid=(n_tiles, nf),
        in_specs=in_specs,
        out_specs=out_spec,
        out_shape=out_shape,
        scratch_shapes=scratch,
        input_output_aliases=aliases,
        compiler_params=_cparams(("arbitrary", "arbitrary")),
        name="ffn_final" if final else "ffn",
    )(*args)


def _inproj_kernel(x_ref, mod_ref, gain_ref, w_ref, b_ref, uc_ref, us_ref, gt_ref, hn_ref):
    j = pl.program_id(1)

    @pl.when(j == 0)
    def _():
        x = x_ref[...]
        y = x * lax.rsqrt(jnp.mean(x * x, axis=-1, keepdims=True) + EPS) * gain_ref[...]
        h = _rows8(y, 1.0 + mod_ref[1], jnp.multiply)
        h = _rows8(h, mod_ref[0], jnp.add)
        hn_ref[...] = h.astype(BF16)

    def proj():
        return jnp.dot(hn_ref[...], w_ref[...], preferred_element_type=F32) + b_ref[0]

    @pl.when(j == 0)
    def _():
        uc_ref[...] = proj()

    @pl.when(j == 1)
    def _():
        uc_ref[...] = uc_ref[...] * jax.nn.sigmoid(proj())

    @pl.when(j == 2)
    def _():
        us_ref[...] = proj()

    @pl.when(j >= 3)
    def _():
        gt_ref[...] = jax.nn.sigmoid(proj()).astype(BF16)


def _inproj(x, modtab, gain, w_in, b_in, layer, *, tm):
    tn = 1024
    n_lat_tiles = ROWS_LAT // tm
    nblk = N_IN // tn
    n_head = nblk - 2 * D_MODEL // tn
    return pl.pallas_call(
        _inproj_kernel,
        grid=(ROWS // tm, nblk),
        in_specs=[
            pl.BlockSpec((tm, D_MODEL), lambda i, j: (i, 0)),
            pl.BlockSpec((None, 3, SUBLANES, D_MODEL),
                         lambda i, j: (jnp.where(i < n_lat_tiles, 1, 0), 1, 0, 0)),
            pl.BlockSpec((1, D_MODEL), lambda i, j: (0, 0)),
            pl.BlockSpec((None, D_MODEL, tn), lambda i, j: (layer, 0, j)),
            pl.BlockSpec((1, 1, tn), lambda i, j: (j, 0, 0)),
        ],
        out_specs=[
            pl.BlockSpec((tm, tn), lambda i, j: (i, 0)),
            pl.BlockSpec((tm, tn), lambda i, j: (i, 0)),
            pl.BlockSpec((tm, tn), lambda i, j: (i, jnp.maximum(j - n_head, 0))),
        ],
        out_shape=[
            jax.ShapeDtypeStruct((ROWS, CONV_DIM), F32),
            jax.ShapeDtypeStruct((ROWS, SSM_DIM), F32),
            jax.ShapeDtypeStruct((ROWS, 2 * D_MODEL), BF16),
        ],
        scratch_shapes=[pltpu.VMEM((tm, D_MODEL), BF16)],
        compiler_params=_cparams(("arbitrary", "arbitrary")),
        name="mixer_inproj",
    )(x, modtab, gain.reshape(1, D_MODEL), w_in, b_in.reshape(nblk, 1, tn))


CONV_ROW_PAD = 64
CONV_SUB_ROWS = 64


def _conv_seq_kernel(u_ref, w_ref, b_ref, o_ref, pad_ref, *, n, cb):
    zeros = jnp.zeros((CONV_ROW_PAD, cb), F32)
    pad_ref[0:CONV_ROW_PAD, :] = zeros
    pad_ref[CONV_ROW_PAD + n:, :] = zeros
    pad_ref[CONV_ROW_PAD:CONV_ROW_PAD + n, :] = u_ref[...]
    base = CONV_ROW_PAD - CONV_PAD * BATCH
    for c in range(cb // LANES):
        cs = slice(c * LANES, (c + 1) * LANES)
        for r in range(n // CONV_SUB_ROWS):
            r0 = r * CONV_SUB_ROWS
            acc = jnp.broadcast_to(b_ref[:, cs], (CONV_SUB_ROWS, LANES))
            for k in range(CONV_K):
                off = base + r0 + k * BATCH
                acc = acc + w_ref[k:k + 1, cs] * pad_ref[off:off + CONV_SUB_ROWS, cs]
            o_ref[r0:r0 + CONV_SUB_ROWS, cs] = acc


def _conv_seq(u, w, b, *, n, cb, row_blk0, n_row_blk, col_blk0, n_col_blk, prev=None):
    in_specs = [
        pl.BlockSpec((n, cb), lambda i, c: (i + row_blk0, c + col_blk0)),
        pl.BlockSpec((CONV_K, cb), lambda i, c: (0, c + col_blk0)),
        pl.BlockSpec((1, cb), lambda i, c: (0, c + col_blk0)),
    ]
    args = [u, w, b.reshape(1, CONV_DIM)]
    aliases = {}
    if prev is not None:
        in_specs.append(pl.BlockSpec(memory_space=pl.ANY))
        args.append(prev)
        aliases = {3: 0}

    def body(u_ref, w_ref, b_ref, *rest):
        o_ref, pad_ref = rest[-2:]
        _conv_seq_kernel(u_ref, w_ref, b_ref, o_ref, pad_ref, n=n, cb=cb)

    return pl.pallas_call(
        body,
        grid=(n_row_blk, n_col_blk),
        in_specs=in_specs,
        out_specs=pl.BlockSpec((n, cb), lambda i, c: (i + row_blk0, c + col_blk0)),
        out_shape=jax.ShapeDtypeStruct((ROWS, CONV_DIM), F32),
        scratch_shapes=[pltpu.VMEM((n + 2 * CONV_ROW_PAD, cb), F32)],
        input_output_aliases=aliases,
        compiler_params=_cparams(("arbitrary", "arbitrary")),
        name="conv_seq",
    )(*args)


def _conv_vert_kernel(u_ref, w_ref, b_ref, prev_ref, o_ref, pad_ref):
    del prev_ref
    rpg = ROWS_PER_GRID_ROW
    zeros = jnp.zeros((CONV_PAD * rpg, LANES), F32)
    pad_ref[0:CONV_PAD * rpg, :] = zeros
    pad_ref[(CONV_PAD + GRID_ROWS) * rpg:, :] = zeros
    pad_ref[CONV_PAD * rpg:(CONV_PAD + GRID_ROWS) * rpg, :] = u_ref[...]
    bias = jnp.broadcast_to(b_ref[...], (CONV_SUB_ROWS, LANES))

    def body(r, carry):
        for s in range(rpg // CONV_SUB_ROWS):
            acc = bias
            for k in range(CONV_K):
                off = pl.multiple_of((r + k) * rpg + s * CONV_SUB_ROWS, CONV_SUB_ROWS)
                acc = acc + w_ref[k:k + 1, :] * pad_ref[pl.ds(off, CONV_SUB_ROWS), :]
            o0 = pl.multiple_of(r * rpg + s * CONV_SUB_ROWS, CONV_SUB_ROWS)
            o_ref[pl.ds(o0, CONV_SUB_ROWS), :] = acc
        return carry

    lax.fori_loop(0, GRID_ROWS, body, 0)


def _conv_vert(u, w, b, prev):
    half_blk = (CONV_DIM // 2) // LANES
    return pl.pallas_call(
        _conv_vert_kernel,
        grid=(half_blk,),
        in_specs=[
            pl.BlockSpec((ROWS_LAT, LANES), lambda c: (0, c + half_blk)),
            pl.BlockSpec((CONV_K, LANES), lambda c: (0, c + half_blk)),
            pl.BlockSpec((1, LANES), lambda c: (0, c + half_blk)),
            pl.BlockSpec(memory_space=pl.ANY),
        ],
        out_specs=pl.BlockSpec((ROWS_LAT, LANES), lambda c: (0, c + half_blk)),
        out_shape=jax.ShapeDtypeStruct((ROWS, CONV_DIM), F32),
        scratch_shapes=[pltpu.VMEM(((GRID_ROWS + 2 * CONV_PAD) * ROWS_PER_GRID_ROW, LANES), F32)],
        input_output_aliases={3: 0},
        compiler_params=_cparams(("arbitrary",)),
        name="conv_vert",
    )(u, w, b.reshape(1, CONV_DIM), prev)


def _ssm_kernel(uf_ref, ub_ref, b2_ref, c2_ref, lam_ref, yf_ref, yb_ref,
                lhs_ref, s_ref, y2_ref, hst_ref):
    k = pl.program_id(0)

    @pl.when(k == 0)
    def _():
        hst_ref[...] = jnp.zeros_like(hst_ref)

    n_pair = SSM_CHUNK_T // 2
    low = lax.broadcasted_iota(jnp.int32, (SUBLANES, SSM_DIM), 0) < BATCH
    zero = jnp.zeros((SUBLANES, SSM_DIM), F32)
    for m in range(n_pair):
        f = uf_ref[m * SUBLANES:(m + 1) * SUBLANES, :]
        mb = n_pair - 1 - m
        g = ub_ref[mb * SUBLANES:(mb + 1) * SUBLANES, :]
        fr = pltpu.roll(f, BATCH, 0)
        gr = pltpu.roll(g, BATCH, 0)
        fwd = jnp.concatenate([jnp.where(low, f, zero), jnp.where(low, fr, zero)], axis=0).astype(BF16)
        bwd = jnp.concatenate([jnp.where(low, zero, g), jnp.where(low, zero, gr)], axis=0).astype(BF16)
        r0 = m * 2 * SUBLANES
        for j in range(N_SLAB):
            cs = slice(j * LANES, (j + 1) * LANES)
            lhs_ref[r0:r0 + 2 * SUBLANES, 2 * j * LANES:(2 * j + 1) * LANES] = fwd[:, cs]
            lhs_ref[r0:r0 + 2 * SUBLANES, (2 * j + 1) * LANES:(2 * j + 2) * LANES] = bwd[:, cs]

    for gp in range(N_GP):
        j = gp // GP_PER_SLAB
        s_ref[gp] = jnp.dot(lhs_ref[:, 2 * j * LANES:(2 * j + 2) * LANES], b2_ref[gp],
                            preferred_element_type=F32)

    re = slice(0, LANES)
    im = slice(LANES, 2 * LANES)

    def gp_block(gb, carry):
        gps = [gb * GP_BLOCK + q for q in range(GP_BLOCK)]
        lam = [(lam_ref[gp, 0], lam_ref[gp, 1]) for gp in gps]
        st0 = tuple(hst_ref[gp, ri] for gp in gps for ri in (0, 1))

        def step(s, st):
            r0 = pl.multiple_of(s * SUBLANES, SUBLANES)
            new = []
            for q, gp in enumerate(gps):
                lr, li = lam[q]
                hre, him = st[2 * q], st[2 * q + 1]
                nre = lr * hre - li * him + s_ref[gp, pl.ds(r0, SUBLANES), re]
                nim = lr * him + li * hre + s_ref[gp, pl.ds(r0, SUBLANES), im]
                s_ref[gp, pl.ds(r0, SUBLANES), re] = nre
                s_ref[gp, pl.ds(r0, SUBLANES), im] = nim
                new += [nre, nim]
            return tuple(new)

        st = lax.fori_loop(0, SSM_CHUNK_T, step, st0, unroll=2)
        for q, gp in enumerate(gps):
            hst_ref[gp, 0] = st[2 * q]
            hst_ref[gp, 1] = st[2 * q + 1]
        return carry

    lax.fori_loop(0, N_GP // GP_BLOCK, gp_block, 0)

    for j in range(N_SLAB):
        acc = None
        for q in range(GP_PER_SLAB):
            gp = j * GP_PER_SLAB + q
            p = jnp.dot(s_ref[gp].astype(BF16), c2_ref[gp], preferred_element_type=F32)
            acc = p if acc is None else acc + p
        y2_ref[:, 2 * j * LANES:(2 * j + 2) * LANES] = acc

    low1 = lax.broadcasted_iota(jnp.int32, (SUBLANES, LANES), 0) < BATCH
    for m in range(n_pair):
        mb = n_pair - 1 - m
        r0 = m * 2 * SUBLANES
        for j in range(N_SLAB):
            cf = slice(2 * j * LANES, (2 * j + 1) * LANES)
            cb = slice((2 * j + 1) * LANES, (2 * j + 2) * LANES)
            cs = slice(j * LANES, (j + 1) * LANES)
            ef = y2_ref[r0:r0 + SUBLANES, cf]
            of = pltpu.roll(y2_ref[r0 + SUBLANES:r0 + 2 * SUBLANES, cf], BATCH, 0)
            yf_ref[m * SUBLANES:(m + 1) * SUBLANES, cs] = jnp.where(low1, ef, of)
            eb = y2_ref[r0:r0 + SUBLANES, cb]
            ob = pltpu.roll(y2_ref[r0 + SUBLANES:r0 + 2 * SUBLANES, cb], BATCH, 0)
            yb_ref[mb * SUBLANES:(mb + 1) * SUBLANES, cs] = jnp.where(low1, ob, eb)


def _ssm(us, b2, c2, lamtab):
    n_lat = N_CHUNK - N_CHUNK_CTX

    def fwd_blk(k):
        return jnp.where(k < N_CHUNK_CTX, n_lat + k, k - N_CHUNK_CTX)

    def bwd_blk(k):
        return N_CHUNK - 1 - k

    blk = (SSM_CHUNK_ROWS, SSM_DIM)
    return pl.pallas_call(
        _ssm_kernel,
        grid=(N_CHUNK,),
        in_specs=[
            pl.BlockSpec(blk, lambda k: (fwd_blk(k), 0)),
            pl.BlockSpec(blk, lambda k: (bwd_blk(k), 0)),
            _const_spec(b2.shape),
            _const_spec(c2.shape),
            _const_spec(lamtab.shape),
        ],
        out_specs=[
            pl.BlockSpec(blk, lambda k: (fwd_blk(k), 0)),
            pl.BlockSpec(blk, lambda k: (bwd_blk(k), 0)),
        ],
        out_shape=[jax.ShapeDtypeStruct((ROWS, SSM_DIM), F32)] * 2,
        scratch_shapes=[
            pltpu.VMEM((SSM_STEP_ROWS, 2 * SSM_DIM), BF16),
            pltpu.VMEM((N_GP, SSM_STEP_ROWS, 2 * LANES), F32),
            pltpu.VMEM((SSM_STEP_ROWS, 2 * SSM_DIM), F32),
            pltpu.VMEM((N_GP, 2, SUBLANES, LANES), F32),
        ],
        compiler_params=_cparams(("arbitrary",)),
        name="s5_scan",
    )(us, us, b2, c2, lamtab)


def _ssm_tables(a_re, a_im, log_dt, b_re, b_im, c_re, c_im):
    lam = lax.complex(a_re.astype(F32), a_im.astype(F32))
    dt = jnp.exp(log_dt.astype(F32))[..., None]
    lam_bar = jnp.exp(lam * dt)
    bmat_c = lax.complex(b_re.astype(F32), b_im.astype(F32))
    b_bar = ((lam_bar - 1) / lam)[..., None] * bmat_c

    def pair_rows(v):
        v = v.reshape(2, N_GP, 2, SSM_GROUP, SSM_STATE).transpose(1, 0, 2, 3, 4)
        z = jnp.zeros_like(v[:, :, 0])
        q0 = jnp.concatenate([v[:, :, 0], z], axis=-1)
        q1 = jnp.concatenate([z, v[:, :, 1]], axis=-1)
        return jnp.stack([q0, q1], axis=2)

    slab_pos = (jnp.arange(N_GP) % GP_PER_SLAB)[:, None] == jnp.arange(GP_PER_SLAB)[None, :]
    place = slab_pos.astype(F32)[:, None, :, None, None, None]

    def place_rows(re_v, im_v):
        t = jnp.concatenate([pair_rows(re_v), pair_rows(im_v)], axis=-1)
        t = t[:, :, None] * place
        return t.reshape(N_GP, 2 * LANES, 2 * LANES)

    bt = jnp.swapaxes(b_bar, -1, -2)
    b2 = place_rows(jnp.real(bt), jnp.imag(bt))
    c2 = jnp.swapaxes(place_rows(c_re.astype(F32), -c_im.astype(F32)), 1, 2)
    lam_ri = jnp.stack([jnp.real(lam_bar), jnp.imag(lam_bar)], axis=0)
    lam_ri = lam_ri.reshape(2, 2, N_GP, 1, LANES).transpose(2, 0, 1, 3, 4)
    lamtab = jnp.broadcast_to(lam_ri, (N_GP, 2, 2, BATCH, LANES)).reshape(N_GP, 2, SUBLANES, LANES)
    return b2.astype(BF16), c2.astype(BF16), lamtab


def _mix_kernel(cv_ref, yf_ref, yb_ref, us_ref, gt_ref, lng_ref, lnb_ref, sd_ref,
                cwo_ref, wglu_ref, bglu_ref, o_ref):
    cv = cv_ref[...]
    mu = jnp.mean(cv, axis=-1, keepdims=True)
    var = jnp.mean(jnp.square(cv - mu), axis=-1, keepdims=True)
    ln = (cv - mu) * lax.rsqrt(var + EPS) * lng_ref[...] + lnb_ref[...]
    y_conv = jnp.dot(jax.nn.silu(ln).astype(BF16), cwo_ref[...], preferred_element_type=F32)
    y = yf_ref[...] + yb_ref[...] + sd_ref[...] * us_ref[...]
    z = jnp.dot(jax.nn.gelu(y).astype(BF16), wglu_ref[...], preferred_element_type=F32) + bglu_ref[...]
    y_ssm = z[:, :D_MODEL] * jax.nn.sigmoid(z[:, D_MODEL:])
    gt = gt_ref[...].astype(F32)
    o_ref[...] = (gt[:, :D_MODEL] * y_conv + gt[:, D_MODEL:] * y_ssm).astype(BF16)


def _mix(cv, yf, yb, us, gt, ln_g, ln_b, ssm_d, cwo, wglu, bglu, layer, *, tm, latent_only):
    n_tiles = (ROWS_LAT if latent_only else ROWS) // tm
    row = lambda w: pl.BlockSpec((tm, w), lambda i: (i, 0))
    vec = lambda w: pl.BlockSpec((1, w), lambda i: (0, 0), pipeline_mode=pl.Buffered(1))
    wgt = lambda r, c: pl.BlockSpec((None, r, c), lambda i: (layer, 0, 0), pipeline_mode=pl.Buffered(1))
    return pl.pallas_call(
        _mix_kernel,
        grid=(n_tiles,),
        in_specs=[
            row(CONV_DIM), row(SSM_DIM), row(SSM_DIM), row(SSM_DIM), row(2 * D_MODEL),
            vec(CONV_DIM), vec(CONV_DIM), vec(SSM_DIM),
            wgt(CONV_DIM, D_MODEL), wgt(SSM_DIM, 2 * D_MODEL), vec(2 * D_MODEL),
        ],
        out_specs=row(D_MODEL),
        out_shape=jax.ShapeDtypeStruct((ROWS, D_MODEL), BF16),
        compiler_params=_cparams(("arbitrary",)),
        name="mixer_mix",
    )(cv, yf, yb, us, gt, ln_g.reshape(1, -1), ln_b.reshape(1, -1), ssm_d.reshape(1, -1),
      cwo, wglu, bglu.reshape(1, -1))


def _outproj_kernel(x_ref, mix_ref, mod_ref, w_ref, o_ref):
    y = jnp.dot(mix_ref[...], w_ref[...], preferred_element_type=F32)
    o_ref[...] = x_ref[...] + _rows8(y, mod_ref[2], jnp.multiply)


def _outproj(x, mix, modtab, w_out, layer, *, tm, latent_only):
    n_lat_tiles = ROWS_LAT // tm
    n_tiles = n_lat_tiles if latent_only else ROWS // tm
    return pl.pallas_call(
        _outproj_kernel,
        grid=(n_tiles,),
        in_specs=[
            pl.BlockSpec((tm, D_MODEL), lambda i: (i, 0)),
            pl.BlockSpec((tm, D_MODEL), lambda i: (i, 0)),
            pl.BlockSpec((None, 3, SUBLANES, D_MODEL),
                         lambda i: (jnp.where(i < n_lat_tiles, 1, 0), 1, 0, 0)),
            pl.BlockSpec((None, D_MODEL, D_MODEL), lambda i: (layer, 0, 0)),
        ],
        out_specs=pl.BlockSpec((tm, D_MODEL), lambda i: (i, 0)),
        out_shape=jax.ShapeDtypeStruct((ROWS, D_MODEL), F32),
        input_output_aliases={0: 0},
        compiler_params=_cparams(("arbitrary",)),
        name="mixer_outproj",
    )(x, mix, modtab, w_out)


def kernel(x, c, ctx, c_ctx, ada_w, ada_b, norm_ffn1, ffn1_w_up, ffn1_w_down, norm_mix, w_in, b_in, conv_dw, conv_db, conv_ln_g, conv_ln_b, conv_w_out, ssm_a_re, ssm_a_im, ssm_log_dt, ssm_b_re, ssm_b_im, ssm_c_re, ssm_c_im, ssm_d, ssm_w_glu, ssm_b_glu, w_out, norm_ffn2, ffn2_w_up, ffn2_w_down, norm_final):
    xs = _to_rows(x, ctx)

    c8 = jnp.zeros((SUBLANES, D_MODEL), F32).at[:BATCH].set(c).at[BATCH].set(c_ctx)
    mod = _ada(c8, ada_w, ada_b).reshape(DEPTH, SUBLANES, N_MOD, D_MODEL)
    lat_rows = jnp.arange(SUBLANES) % BATCH
    mod_lat = mod[:, lat_rows].transpose(0, 2, 1, 3)
    mod_ctx = jnp.broadcast_to(mod[:, BATCH][:, :, None, :], mod_lat.shape)
    modtab = jnp.stack([mod_ctx, mod_lat], axis=1)

    up_last = ffn2_w_up[DEPTH - 1:].astype(BF16)
    down_last = ffn2_w_down[DEPTH - 1:].astype(BF16)
    win, cwo = w_in.astype(BF16), conv_w_out.astype(BF16)
    wglu, wout = ssm_w_glu.astype(BF16), w_out.astype(BF16)

    for l in range(DEPTH):
        last = l == DEPTH - 1
        mt = modtab[l]
        xs = _ffn(xs, mt, 0, norm_ffn1[l], ffn1_w_up, ffn1_w_down, l, tm=1024, tf=256)
        uc, us, gt = _inproj(xs, mt, norm_mix[l], win, b_in[l], l, tm=512)

        cv = _conv_seq(uc, conv_dw[l], conv_db[l], n=ROWS_PER_GRID_ROW, cb=CONV_DIM // 2,
                       row_blk0=0, n_row_blk=GRID_ROWS, col_blk0=0, n_col_blk=1)
        cv = _conv_vert(uc, conv_dw[l], conv_db[l], cv)
        if not last:
            cv = _conv_seq(uc, conv_dw[l], conv_db[l], n=ROWS_CTX, cb=LANES,
                           row_blk0=ROWS_LAT // ROWS_CTX, n_row_blk=1, col_blk0=0,
                           n_col_blk=CONV_DIM // LANES, prev=cv)

        b2, c2, lamtab = _ssm_tables(ssm_a_re[l], ssm_a_im[l], ssm_log_dt[l],
                                     ssm_b_re[l], ssm_b_im[l], ssm_c_re[l], ssm_c_im[l])
        yf, yb = _ssm(us, b2, c2, lamtab)

        mix = _mix(cv, yf, yb, us, gt, conv_ln_g[l], conv_ln_b[l], ssm_d[l], cwo, wglu, ssm_b_glu[l], l,
                   tm=256, latent_only=last)
        xs = _outproj(xs, mix, mt, wout, l, tm=512, latent_only=last)
        if last:
            xs = _ffn(xs, mt, 2, norm_ffn2[l], up_last, down_last, 0, tm=512, tf=512,
                      latent_only=True, final_gain=norm_final)
        else:
            xs = _ffn(xs, mt, 2, norm_ffn2[l], ffn2_w_up, ffn2_w_down, l, tm=1024, tf=256)

    return xs
```

```python
import functools

import jax
import jax.numpy as jnp
from jax import lax
from jax.experimental import pallas as pl
from jax.experimental.pallas import tpu as pltpu

F32 = jnp.float32
BF16 = jnp.bfloat16

D_MODEL = 2048
BATCH = 4
SEQ = 2048
CTX_LEN = 256
DEPTH = 2
GRID_W = 64
GRID_ROWS = SEQ // GRID_W
D_FF = 5632
CONV_DIM = 1024
CONV_K = 31
CONV_PAD = CONV_K // 2
SSM_DIM = 1024
SSM_GROUP = 16
SSM_GROUPS = 64
SSM_STATE = 64
N_MOD = 9
N_IN = 2 * CONV_DIM + SSM_DIM + 2 * D_MODEL
FFN_RES = 0.5
EPS = 1e-6

LANES = 128
SUBLANES = 8
ROWS_CTX = CTX_LEN * BATCH
ROWS_LAT = SEQ * BATCH
ROWS = ROWS_CTX + ROWS_LAT
ROWS_PER_GRID_ROW = GRID_W * BATCH

VMEM_LIMIT = 60 * 1024 * 1024

GROUPS_PER_SLAB = LANES // SSM_GROUP
N_GP = SSM_GROUPS // 2
GP_PER_SLAB = GROUPS_PER_SLAB // 2
N_SLAB = SSM_DIM // LANES
SSM_CHUNK_T = 64
SSM_CHUNK_ROWS = SSM_CHUNK_T * BATCH
SSM_STEP_ROWS = 2 * SSM_CHUNK_ROWS
N_CHUNK_CTX = CTX_LEN // SSM_CHUNK_T
N_CHUNK = (CTX_LEN + SEQ) // SSM_CHUNK_T
assert 2 * BATCH == SUBLANES


def _cparams(sem):
    return pltpu.CompilerParams(dimension_semantics=sem, vmem_limit_bytes=VMEM_LIMIT)


def _const_spec(shape):
    nd = len(shape)
    return pl.BlockSpec(shape, lambda *_: (0,) * nd, pipeline_mode=pl.Buffered(1))


def _rows8(x, m, op):
    n, d = x.shape
    x3 = x.reshape(n // SUBLANES, SUBLANES, d)
    return op(x3, m[None]).reshape(n, d)


def _interleave_batch(src_ref, o_ref, slab_ref, tq):
    for c in range(D_MODEL // LANES):
        cs = slice(c * LANES, (c + 1) * LANES)
        for b in range(BATCH):
            slab_ref[c, pl.ds(b, tq, stride=BATCH), :] = src_ref[b, :, cs]
        o_ref[:, cs] = slab_ref[c]


def _to_rows_kernel(x_ref, c_ref, o_ref, slab_ref, *, n_lat_tiles, tq):
    i = pl.program_id(0)

    @pl.when(i < n_lat_tiles)
    def _():
        _interleave_batch(x_ref, o_ref, slab_ref, tq)

    @pl.when(i >= n_lat_tiles)
    def _():
        _interleave_batch(c_ref, o_ref, slab_ref, tq)


def _to_rows(x, ctx):
    tq = 64
    n_lat_tiles = SEQ // tq
    n_tiles = (SEQ + CTX_LEN) // tq
    return pl.pallas_call(
        functools.partial(_to_rows_kernel, n_lat_tiles=n_lat_tiles, tq=tq),
        grid=(n_tiles,),
        in_specs=[
            pl.BlockSpec((BATCH, tq, D_MODEL), lambda i: (0, jnp.minimum(i, n_lat_tiles - 1), 0)),
            pl.BlockSpec((BATCH, tq, D_MODEL), lambda i: (0, jnp.maximum(i - n_lat_tiles, 0), 0)),
        ],
        out_specs=pl.BlockSpec((tq * BATCH, D_MODEL), lambda i: (i, 0)),
        out_shape=jax.ShapeDtypeStruct((ROWS, D_MODEL), F32),
        scratch_shapes=[pltpu.VMEM((D_MODEL // LANES, tq * BATCH, LANES), F32)],
        compiler_params=_cparams(("arbitrary",)),
        name="to_rows",
    )(x, ctx)


def _ada_kernel(c_ref, w_ref, b_ref, o_ref):
    c = c_ref[...]
    s = jax.nn.silu(c).astype(BF16)
    o_ref[0] = jnp.dot(s, w_ref[0].astype(BF16), preferred_element_type=F32) + b_ref[0]


def _ada(c8, ada_w, ada_b):
    tn = 1024
    n = N_MOD * D_MODEL
    return pl.pallas_call(
        _ada_kernel,
        grid=(DEPTH, n // tn),
        in_specs=[
            pl.BlockSpec((SUBLANES, D_MODEL), lambda l, j: (0, 0)),
            pl.BlockSpec((1, D_MODEL, tn), lambda l, j: (l, 0, j)),
            pl.BlockSpec((1, 1, tn), lambda l, j: (l, 0, j)),
        ],
        out_specs=pl.BlockSpec((1, SUBLANES, tn), lambda l, j: (l, 0, j)),
        out_shape=jax.ShapeDtypeStruct((DEPTH, SUBLANES, n), F32),
        compiler_params=_cparams(("arbitrary", "arbitrary")),
        name="ada_mod",
    )(c8, ada_w, ada_b.reshape(DEPTH, 1, n))


def _ffn_kernel(x_ref, mod_ref, gain_ref, wa_ref, wg_ref, wd_ref, *rest, nf, final):
    if final:
        fgain_ref, o_ref, hn_ref, acc_ref, slab_ref = rest
    else:
        o_ref, hn_ref = rest
        acc_ref = o_ref
    j = pl.program_id(1)

    def mxu_operand(w_ref):
        w = w_ref[...]
        return w if w.dtype == BF16 else w.astype(BF16)

    def chunk():
        hn = hn_ref[...]
        a = jnp.dot(hn, mxu_operand(wa_ref), preferred_element_type=F32)
        g = jnp.dot(hn, mxu_operand(wg_ref), preferred_element_type=F32)
        act = (jax.nn.silu(g) * a).astype(BF16)
        return jnp.dot(act, mxu_operand(wd_ref), preferred_element_type=F32)

    @pl.when(j == 0)
    def _():
        x = x_ref[...]
        y = x * lax.rsqrt(jnp.mean(x * x, axis=-1, keepdims=True) + EPS) * gain_ref[...]
        h = _rows8(y, 1.0 + mod_ref[1], jnp.multiply)
        h = _rows8(h, mod_ref[0], jnp.add)
        hn_ref[...] = h.astype(BF16)
        acc_ref[...] = chunk()

    @pl.when((j > 0) & (j < nf - 1))
    def _():
        acc_ref[...] += chunk()

    @pl.when(j == nf - 1)
    def _():
        upd = _rows8(acc_ref[...] + chunk(), mod_ref[2] * FFN_RES, jnp.multiply)
        xn = x_ref[...] + upd
        if final:
            xn = xn * lax.rsqrt(jnp.mean(xn * xn, axis=-1, keepdims=True) + EPS) * fgain_ref[...]
            tq = acc_ref.shape[0] // BATCH
            for c in range(D_MODEL // LANES):
                cs = slice(c * LANES, (c + 1) * LANES)
                slab_ref[c] = xn[:, cs]
                for b in range(BATCH):
                    o_ref[b, :, cs] = slab_ref[c, pl.ds(b, tq, stride=BATCH), :]
        else:
            o_ref[...] = xn


def _ffn(x, modtab, sub, gain, w_up, w_down, layer, *, tm, tf, latent_only=False, final_gain=None):
    nf = D_FF // tf
    n_lat_tiles = ROWS_LAT // tm
    n_tiles = n_lat_tiles if latent_only else ROWS // tm
    final = final_gain is not None
    in_specs = [
        pl.BlockSpec((tm, D_MODEL), lambda i, j: (i, 0)),
        pl.BlockSpec((None, 3, SUBLANES, D_MODEL),
                     lambda i, j: (jnp.where(i < n_lat_tiles, 1, 0), sub, 0, 0)),
        pl.BlockSpec((1, D_MODEL), lambda i, j: (0, 0)),
        pl.BlockSpec((None, D_MODEL, tf), lambda i, j: (layer, 0, j)),
        pl.BlockSpec((None, D_MODEL, tf), lambda i, j: (layer, 0, j + nf)),
        pl.BlockSpec((None, tf, D_MODEL), lambda i, j: (layer, j, 0)),
    ]
    args = [x, modtab, gain.reshape(1, D_MODEL), w_up, w_up, w_down]
    scratch = [pltpu.VMEM((tm, D_MODEL), BF16)]
    if final:
        assert latent_only
        in_specs.append(pl.BlockSpec((1, D_MODEL), lambda i, j: (0, 0)))
        args.append(final_gain.reshape(1, D_MODEL))
        scratch.append(pltpu.VMEM((tm, D_MODEL), F32))
        scratch.append(pltpu.VMEM((D_MODEL // LANES, tm, LANES), F32))
        out_shape = jax.ShapeDtypeStruct((BATCH, SEQ, D_MODEL), F32)
        out_spec = pl.BlockSpec((BATCH, tm // BATCH, D_MODEL), lambda i, j: (0, i, 0))
        aliases = {}
    else:
        out_shape = jax.ShapeDtypeStruct((ROWS, D_MODEL), F32)
        out_spec = pl.BlockSpec((tm, D_MODEL), lambda i, j: (i, 0))
        aliases = {0: 0}
    return pl.pallas_call(
        functools.partial(_ffn_kernel, nf=nf, final=final),
        grid=(n_tiles, nf),
        in_specs=in_specs,
        out_specs=out_spec,
        out_shape=out_shape,
        scratch_shapes=scratch,
        input_output_aliases=aliases,
        compiler_params=_cparams(("arbitrary", "arbitrary")),
        name="ffn_final" if final else "ffn",
    )(*args)


def _inproj_kernel(x_ref, mod_ref, gain_ref, w_ref, b_ref, uc_ref, us_ref, gt_ref, hn_ref):
    j = pl.program_id(1)

    @pl.when(j == 0)
    def _():
        x = x_ref[...]
        y = x * lax.rsqrt(jnp.mean(x * x, axis=-1, keepdims=True) + EPS) * gain_ref[...]
        h = _rows8(y, 1.0 + mod_ref[1], jnp.multiply)
        h = _rows8(h, mod_ref[0], jnp.add)
        hn_ref[...] = h.astype(BF16)

    def proj():
        return jnp.dot(hn_ref[...], w_ref[...], preferred_element_type=F32) + b_ref[0]

    @pl.when(j == 0)
    def _():
        uc_ref[...] = proj()

    @pl.when(j == 1)
    def _():
        uc_ref[...] = uc_ref[...] * jax.nn.sigmoid(proj())

    @pl.when(j == 2)
    def _():
        us_ref[...] = proj()

    @pl.when(j >= 3)
    def _():
        gt_ref[...] = jax.nn.sigmoid(proj()).astype(BF16)


def _inproj(x, modtab, gain, w_in, b_in, layer, *, tm):
    tn = 1024
    n_lat_tiles = ROWS_LAT // tm
    nblk = N_IN // tn
    n_head = nblk - 2 * D_MODEL // tn
    return pl.pallas_call(
        _inproj_kernel,
        grid=(ROWS // tm, nblk),
        in_specs=[
            pl.BlockSpec((tm, D_MODEL), lambda i, j: (i, 0)),
            pl.BlockSpec((None, 3, SUBLANES, D_MODEL),
                         lambda i, j: (jnp.where(i < n_lat_tiles, 1, 0), 1, 0, 0)),
            pl.BlockSpec((1, D_MODEL), lambda i, j: (0, 0)),
            pl.BlockSpec((None, D_MODEL, tn), lambda i, j: (layer, 0, j)),
            pl.BlockSpec((1, 1, tn), lambda i, j: (j, 0, 0)),
        ],
        out_specs=[
            pl.BlockSpec((tm, tn), lambda i, j: (i, 0)),
            pl.BlockSpec((tm, tn), lambda i, j: (i, 0)),
            pl.BlockSpec((tm, tn), lambda i, j: (i, jnp.maximum(j - n_head, 0))),
        ],
        out_shape=[
            jax.ShapeDtypeStruct((ROWS, CONV_DIM), F32),
            jax.ShapeDtypeStruct((ROWS, SSM_DIM), F32),
            jax.ShapeDtypeStruct((ROWS, 2 * D_MODEL), BF16),
        ],
        scratch_shapes=[pltpu.VMEM((tm, D_MODEL), BF16)],
        compiler_params=_cparams(("arbitrary", "arbitrary")),
        name="mixer_inproj",
    )(x, modtab, gain.reshape(1, D_MODEL), w_in, b_in.reshape(nblk, 1, tn))


CONV_ROW_PAD = 64
CONV_SUB_ROWS = 64


CONV_HALF = CONV_DIM // 2


def _conv_seq_taps(u_ref, w_ref, b_ref, o_ref, pad_ref, pad4_ref, *, n, cb):
    zeros = jnp.zeros((CONV_ROW_PAD, cb), F32)
    pad_ref[0:CONV_ROW_PAD, :] = zeros
    pad_ref[CONV_ROW_PAD + n:, :] = zeros
    pad_ref[CONV_ROW_PAD:CONV_ROW_PAD + n, :] = u_ref[...]
    n_pad = n + 2 * CONV_ROW_PAD
    pad4_ref[0:n_pad - SUBLANES, :] = pad_ref[BATCH:n_pad - SUBLANES + BATCH, :]
    base = CONV_ROW_PAD - CONV_PAD * BATCH
    for c in range(cb // LANES):
        cs = slice(c * LANES, (c + 1) * LANES)
        for r in range(n // CONV_SUB_ROWS):
            r0 = r * CONV_SUB_ROWS
            acc = jnp.broadcast_to(b_ref[:, cs], (CONV_SUB_ROWS, LANES))
            for k in range(CONV_K):
                off = base + r0 + k * BATCH
                if off % SUBLANES == 0:
                    tap = pad_ref[off:off + CONV_SUB_ROWS, cs]
                else:
                    tap = pad4_ref[off - BATCH:off - BATCH + CONV_SUB_ROWS, cs]
                acc = acc + w_ref[k:k + 1, cs] * tap
            o_ref[r0:r0 + CONV_SUB_ROWS, cs] = acc


def _conv_rows_kernel(u_ref, w_ref, b_ref, o_ref, pad_ref, pad4_ref):
    i = pl.program_id(0)
    o_ref[:, CONV_HALF:] = jnp.zeros((ROWS_PER_GRID_ROW, CONV_DIM - CONV_HALF), F32)

    @pl.when(i < GRID_ROWS)
    def _():
        _conv_seq_taps(u_ref, w_ref, b_ref, o_ref, pad_ref, pad4_ref, n=ROWS_PER_GRID_ROW, cb=CONV_HALF)

    @pl.when(i >= GRID_ROWS)
    def _():
        o_ref[:, :CONV_HALF] = jnp.zeros((ROWS_PER_GRID_ROW, CONV_HALF), F32)


def _conv_rows(u, w, b, *, rows_out):
    n = ROWS_PER_GRID_ROW
    n_pad = n + 2 * CONV_ROW_PAD
    return pl.pallas_call(
        _conv_rows_kernel,
        grid=(rows_out // n,),
        in_specs=[
            pl.BlockSpec((n, CONV_HALF), lambda i: (jnp.minimum(i, GRID_ROWS - 1), 0)),
            pl.BlockSpec((CONV_K, CONV_HALF), lambda i: (0, 0)),
            pl.BlockSpec((1, CONV_HALF), lambda i: (0, 0)),
        ],
        out_specs=pl.BlockSpec((n, CONV_DIM), lambda i: (i, 0)),
        out_shape=jax.ShapeDtypeStruct((rows_out, CONV_DIM), F32),
        scratch_shapes=[pltpu.VMEM((n_pad, CONV_HALF), F32), pltpu.VMEM((n_pad, CONV_HALF), F32)],
        compiler_params=_cparams(("arbitrary",)),
        name="conv_rows",
    )(u, w, b.reshape(1, CONV_DIM))


def _conv_ctx_kernel(u_ref, w_ref, b_ref, prev_ref, o_ref, pad_ref, pad4_ref):
    del prev_ref
    _conv_seq_taps(u_ref, w_ref, b_ref, o_ref, pad_ref, pad4_ref, n=ROWS_CTX, cb=LANES)


def _conv_ctx(u, w, b, prev):
    blk0 = ROWS_LAT // ROWS_CTX
    n_pad = ROWS_CTX + 2 * CONV_ROW_PAD
    return pl.pallas_call(
        _conv_ctx_kernel,
        grid=(CONV_DIM // LANES,),
        in_specs=[
            pl.BlockSpec((ROWS_CTX, LANES), lambda c: (blk0, c)),
            pl.BlockSpec((CONV_K, LANES), lambda c: (0, c)),
            pl.BlockSpec((1, LANES), lambda c: (0, c)),
            pl.BlockSpec(memory_space=pl.ANY),
        ],
        out_specs=pl.BlockSpec((ROWS_CTX, LANES), lambda c: (blk0, c)),
        out_shape=jax.ShapeDtypeStruct((ROWS, CONV_DIM), F32),
        scratch_shapes=[pltpu.VMEM((n_pad, LANES), F32), pltpu.VMEM((n_pad, LANES), F32)],
        input_output_aliases={3: 0},
        compiler_params=_cparams(("arbitrary",)),
        name="conv_ctx",
    )(u, w, b.reshape(1, CONV_DIM), prev)


def _conv_vert_kernel(u_ref, w_ref, b_ref, prev_ref, o_ref, pad_ref):
    del prev_ref
    rpg = ROWS_PER_GRID_ROW
    zeros = jnp.zeros((CONV_PAD * rpg, LANES), F32)
    pad_ref[0:CONV_PAD * rpg, :] = zeros
    pad_ref[(CONV_PAD + GRID_ROWS) * rpg:, :] = zeros
    pad_ref[CONV_PAD * rpg:(CONV_PAD + GRID_ROWS) * rpg, :] = u_ref[...]
    bias = jnp.broadcast_to(b_ref[...], (CONV_SUB_ROWS, LANES))

    def body(r, carry):
        for s in range(rpg // CONV_SUB_ROWS):
            acc = bias
            for k in range(CONV_K):
                off = pl.multiple_of((r + k) * rpg + s * CONV_SUB_ROWS, CONV_SUB_ROWS)
                acc = acc + w_ref[k:k + 1, :] * pad_ref[pl.ds(off, CONV_SUB_ROWS), :]
            o0 = pl.multiple_of(r * rpg + s * CONV_SUB_ROWS, CONV_SUB_ROWS)
            o_ref[pl.ds(o0, CONV_SUB_ROWS), :] = acc
        return carry

    lax.fori_loop(0, GRID_ROWS, body, 0)


def _conv_vert(u, w, b, prev):
    half_blk = (CONV_DIM // 2) // LANES
    return pl.pallas_call(
        _conv_vert_kernel,
        grid=(half_blk,),
        in_specs=[
            pl.BlockSpec((ROWS_LAT, LANES), lambda c: (0, c + half_blk)),
            pl.BlockSpec((CONV_K, LANES), lambda c: (0, c + half_blk)),
            pl.BlockSpec((1, LANES), lambda c: (0, c + half_blk)),
            pl.BlockSpec(memory_space=pl.ANY),
        ],
        out_specs=pl.BlockSpec((ROWS_LAT, LANES), lambda c: (0, c + half_blk)),
        out_shape=jax.ShapeDtypeStruct(prev.shape, F32),
        scratch_shapes=[pltpu.VMEM(((GRID_ROWS + 2 * CONV_PAD) * ROWS_PER_GRID_ROW, LANES), F32)],
        input_output_aliases={3: 0},
        compiler_params=_cparams(("arbitrary",)),
        name="conv_vert",
    )(u, w, b.reshape(1, CONV_DIM), prev)


def _ssm_kernel(uf_ref, ub_ref, b2_ref, c2_ref, lam_ref, yf_ref, yb_ref,
                lhs_ref, s_ref, y2_ref, hst_ref):
    k = pl.program_id(0)

    @pl.when(k == 0)
    def _():
        hst_ref[...] = jnp.zeros_like(hst_ref)

    n_pair = SSM_CHUNK_T // 2
    low = lax.broadcasted_iota(jnp.int32, (SUBLANES, SSM_DIM), 0) < BATCH
    zero = jnp.zeros((SUBLANES, SSM_DIM), F32)
    for m in range(n_pair):
        f = uf_ref[m * SUBLANES:(m + 1) * SUBLANES, :]
        mb = n_pair - 1 - m
        g = ub_ref[mb * SUBLANES:(mb + 1) * SUBLANES, :]
        fr = pltpu.roll(f, BATCH, 0)
        gr = pltpu.roll(g, BATCH, 0)
        fwd = jnp.concatenate([jnp.where(low, f, zero), jnp.where(low, fr, zero)], axis=0).astype(BF16)
        bwd = jnp.concatenate([jnp.where(low, zero, g), jnp.where(low, zero, gr)], axis=0).astype(BF16)
        r0 = m * 2 * SUBLANES
        for j in range(N_SLAB):
            cs = slice(j * LANES, (j + 1) * LANES)
            lhs_ref[r0:r0 + 2 * SUBLANES, 2 * j * LANES:(2 * j + 1) * LANES] = fwd[:, cs]
            lhs_ref[r0:r0 + 2 * SUBLANES, (2 * j + 1) * LANES:(2 * j + 2) * LANES] = bwd[:, cs]

    re = slice(0, LANES)
    im = slice(LANES, 2 * LANES)

    def project(j):
        for gp in range(j * GP_PER_SLAB, (j + 1) * GP_PER_SLAB):
            s_ref[gp] = jnp.dot(lhs_ref[:, 2 * j * LANES:(2 * j + 2) * LANES], b2_ref[gp],
                                preferred_element_type=F32)

    def scan(j):
        for gp in range(j * GP_PER_SLAB, (j + 1) * GP_PER_SLAB):
            lr, li = lam_ref[gp, 0], lam_ref[gp, 1]
            hre, him = hst_ref[gp, 0], hst_ref[gp, 1]
            for s in range(SSM_CHUNK_T):
                rows = slice(s * SUBLANES, (s + 1) * SUBLANES)
                nre = lr * hre - li * him + s_ref[gp, rows, re]
                nim = lr * him + li * hre + s_ref[gp, rows, im]
                s_ref[gp, rows, re] = nre
                s_ref[gp, rows, im] = nim
                hre, him = nre, nim
            hst_ref[gp, 0] = hre
            hst_ref[gp, 1] = him

    def read_out(j):
        acc = None
        for gp in range(j * GP_PER_SLAB, (j + 1) * GP_PER_SLAB):
            p = jnp.dot(s_ref[gp].astype(BF16), c2_ref[gp], preferred_element_type=F32)
            acc = p if acc is None else acc + p
        y2_ref[:, 2 * j * LANES:(2 * j + 2) * LANES] = acc

    project(0)
    for j in range(N_SLAB):
        if j + 1 < N_SLAB:
            project(j + 1)
        scan(j)
        if j >= 1:
            read_out(j - 1)
    read_out(N_SLAB - 1)

    low1 = lax.broadcasted_iota(jnp.int32, (SUBLANES, LANES), 0) < BATCH
    for m in range(n_pair):
        mb = n_pair - 1 - m
        r0 = m * 2 * SUBLANES
        for j in range(N_SLAB):
            cf = slice(2 * j * LANES, (2 * j + 1) * LANES)
            cb = slice((2 * j + 1) * LANES, (2 * j + 2) * LANES)
            cs = slice(j * LANES, (j + 1) * LANES)
            ef = y2_ref[r0:r0 + SUBLANES, cf]
            of = pltpu.roll(y2_ref[r0 + SUBLANES:r0 + 2 * SUBLANES, cf], BATCH, 0)
            yf_ref[m * SUBLANES:(m + 1) * SUBLANES, cs] = jnp.where(low1, ef, of)
            eb = y2_ref[r0:r0 + SUBLANES, cb]
            ob = pltpu.roll(y2_ref[r0 + SUBLANES:r0 + 2 * SUBLANES, cb], BATCH, 0)
            yb_ref[mb * SUBLANES:(mb + 1) * SUBLANES, cs] = jnp.where(low1, ob, eb)


def _ssm(us, b2, c2, lamtab):
    n_lat = N_CHUNK - N_CHUNK_CTX

    def fwd_blk(k):
        return jnp.where(k < N_CHUNK_CTX, n_lat + k, k - N_CHUNK_CTX)

    def bwd_blk(k):
        return N_CHUNK - 1 - k

    blk = (SSM_CHUNK_ROWS, SSM_DIM)
    return pl.pallas_call(
        _ssm_kernel,
        grid=(N_CHUNK,),
        in_specs=[
            pl.BlockSpec(blk, lambda k: (fwd_blk(k), 0)),
            pl.BlockSpec(blk, lambda k: (bwd_blk(k), 0)),
            _const_spec(b2.shape),
            _const_spec(c2.shape),
            _const_spec(lamtab.shape),
        ],
        out_specs=[
            pl.BlockSpec(blk, lambda k: (fwd_blk(k), 0)),
            pl.BlockSpec(blk, lambda k: (bwd_blk(k), 0)),
        ],
        out_shape=[jax.ShapeDtypeStruct((ROWS, SSM_DIM), F32)] * 2,
        scratch_shapes=[
            pltpu.VMEM((SSM_STEP_ROWS, 2 * SSM_DIM), BF16),
            pltpu.VMEM((N_GP, SSM_STEP_ROWS, 2 * LANES), F32),
            pltpu.VMEM((SSM_STEP_ROWS, 2 * SSM_DIM), F32),
            pltpu.VMEM((N_GP, 2, SUBLANES, LANES), F32),
        ],
        compiler_params=_cparams(("arbitrary",)),
        name="s5_scan",
    )(us, us, b2, c2, lamtab)


def _ssm_tables(a_re, a_im, log_dt, b_re, b_im, c_re, c_im):
    lam = lax.complex(a_re.astype(F32), a_im.astype(F32))
    dt = jnp.exp(log_dt.astype(F32))[..., None]
    lam_bar = jnp.exp(lam * dt)
    bmat_c = lax.complex(b_re.astype(F32), b_im.astype(F32))
    b_bar = ((lam_bar - 1) / lam)[..., None] * bmat_c

    def pair_rows(v):
        v = v.reshape(2, N_GP, 2, SSM_GROUP, SSM_STATE).transpose(1, 0, 2, 3, 4)
        z = jnp.zeros_like(v[:, :, 0])
        q0 = jnp.concatenate([v[:, :, 0], z], axis=-1)
        q1 = jnp.concatenate([z, v[:, :, 1]], axis=-1)
        return jnp.stack([q0, q1], axis=2)

    slab_pos = (jnp.arange(N_GP) % GP_PER_SLAB)[:, None] == jnp.arange(GP_PER_SLAB)[None, :]
    place = slab_pos.astype(F32)[:, None, :, None, None, None]

    def place_rows(re_v, im_v):
        t = jnp.concatenate([pair_rows(re_v), pair_rows(im_v)], axis=-1)
        t = t[:, :, None] * place
        return t.reshape(N_GP, 2 * LANES, 2 * LANES)

    bt = jnp.swapaxes(b_bar, -1, -2)
    b2 = place_rows(jnp.real(bt), jnp.imag(bt))
    c2 = jnp.swapaxes(place_rows(c_re.astype(F32), -c_im.astype(F32)), 1, 2)
    lam_ri = jnp.stack([jnp.real(lam_bar), jnp.imag(lam_bar)], axis=0)
    lam_ri = lam_ri.reshape(2, 2, N_GP, 1, LANES).transpose(2, 0, 1, 3, 4)
    lamtab = jnp.broadcast_to(lam_ri, (N_GP, 2, 2, BATCH, LANES)).reshape(N_GP, 2, SUBLANES, LANES)
    return b2.astype(BF16), c2.astype(BF16), lamtab


def _mix_kernel(cv_ref, yf_ref, yb_ref, us_ref, gt_ref, lng_ref, lnb_ref, sd_ref,
                cwo_ref, wglu_ref, bglu_ref, o_ref):
    cv = cv_ref[...]
    mu = jnp.mean(cv, axis=-1, keepdims=True)
    var = jnp.mean(jnp.square(cv - mu), axis=-1, keepdims=True)
    ln = (cv - mu) * lax.rsqrt(var + EPS) * lng_ref[...] + lnb_ref[...]
    y_conv = jnp.dot(jax.nn.silu(ln).astype(BF16), cwo_ref[...], preferred_element_type=F32)
    y = yf_ref[...] + yb_ref[...] + sd_ref[...] * us_ref[...]
    z = jnp.dot(jax.nn.gelu(y).astype(BF16), wglu_ref[...], preferred_element_type=F32) + bglu_ref[...]
    y_ssm = z[:, :D_MODEL] * jax.nn.sigmoid(z[:, D_MODEL:])
    gt = gt_ref[...].astype(F32)
    o_ref[...] = (gt[:, :D_MODEL] * y_conv + gt[:, D_MODEL:] * y_ssm).astype(BF16)


def _mix(cv, yf, yb, us, gt, ln_g, ln_b, ssm_d, cwo, wglu, bglu, layer, *, tm, latent_only):
    n_tiles = (ROWS_LAT if latent_only else ROWS) // tm
    row = lambda w: pl.BlockSpec((tm, w), lambda i: (i, 0))
    vec = lambda w: pl.BlockSpec((1, w), lambda i: (0, 0), pipeline_mode=pl.Buffered(1))
    wgt = lambda r, c: pl.BlockSpec((None, r, c), lambda i: (layer, 0, 0), pipeline_mode=pl.Buffered(1))
    return pl.pallas_call(
        _mix_kernel,
        grid=(n_tiles,),
        in_specs=[
            row(CONV_DIM), row(SSM_DIM), row(SSM_DIM), row(SSM_DIM), row(2 * D_MODEL),
            vec(CONV_DIM), vec(CONV_DIM), vec(SSM_DIM),
            wgt(CONV_DIM, D_MODEL), wgt(SSM_DIM, 2 * D_MODEL), vec(2 * D_MODEL),
        ],
        out_specs=row(D_MODEL),
        out_shape=jax.ShapeDtypeStruct((n_tiles * tm, D_MODEL), BF16),
        compiler_params=_cparams(("arbitrary",)),
        name="mixer_mix",
    )(cv, yf, yb, us, gt, ln_g.reshape(1, -1), ln_b.reshape(1, -1), ssm_d.reshape(1, -1),
      cwo, wglu, bglu.reshape(1, -1))


def _outproj_kernel(x_ref, mix_ref, mod_ref, w_ref, o_ref):
    y = jnp.dot(mix_ref[...], w_ref[...], preferred_element_type=F32)
    o_ref[...] = x_ref[...] + _rows8(y, mod_ref[2], jnp.multiply)


def _outproj(x, mix, modtab, w_out, layer, *, tm, latent_only):
    n_lat_tiles = ROWS_LAT // tm
    n_tiles = n_lat_tiles if latent_only else ROWS // tm
    return pl.pallas_call(
        _outproj_kernel,
        grid=(n_tiles,),
        in_specs=[
            pl.BlockSpec((tm, D_MODEL), lambda i: (i, 0)),
            pl.BlockSpec((tm, D_MODEL), lambda i: (i, 0)),
            pl.BlockSpec((None, 3, SUBLANES, D_MODEL),
                         lambda i: (jnp.where(i < n_lat_tiles, 1, 0), 1, 0, 0)),
            pl.BlockSpec((None, D_MODEL, D_MODEL), lambda i: (layer, 0, 0)),
        ],
        out_specs=pl.BlockSpec((tm, D_MODEL), lambda i: (i, 0)),
        out_shape=jax.ShapeDtypeStruct((ROWS, D_MODEL), F32),
        input_output_aliases={0: 0},
        compiler_params=_cparams(("arbitrary",)),
        name="mixer_outproj",
    )(x, mix, modtab, w_out)


def kernel(x, c, ctx, c_ctx, ada_w, ada_b, norm_ffn1, ffn1_w_up, ffn1_w_down, norm_mix, w_in, b_in, conv_dw, conv_db, conv_ln_g, conv_ln_b, conv_w_out, ssm_a_re, ssm_a_im, ssm_log_dt, ssm_b_re, ssm_b_im, ssm_c_re, ssm_c_im, ssm_d, ssm_w_glu, ssm_b_glu, w_out, norm_ffn2, ffn2_w_up, ffn2_w_down, norm_final):
    xs = _to_rows(x, ctx)

    c8 = jnp.zeros((SUBLANES, D_MODEL), F32).at[:BATCH].set(c).at[BATCH].set(c_ctx)
    mod = _ada(c8, ada_w, ada_b).reshape(DEPTH, SUBLANES, N_MOD, D_MODEL)
    lat_rows = jnp.arange(SUBLANES) % BATCH
    mod_lat = mod[:, lat_rows].transpose(0, 2, 1, 3)
    mod_ctx = jnp.broadcast_to(mod[:, BATCH][:, :, None, :], mod_lat.shape)
    modtab = jnp.stack([mod_ctx, mod_lat], axis=1)

    up_last = ffn2_w_up[DEPTH - 1:].astype(BF16)
    down_last = ffn2_w_down[DEPTH - 1:].astype(BF16)
    win, cwo = w_in.astype(BF16), conv_w_out.astype(BF16)
    wglu, wout = ssm_w_glu.astype(BF16), w_out.astype(BF16)

    for l in range(DEPTH):
        last = l == DEPTH - 1
        mt = modtab[l]
        xs = _ffn(xs, mt, 0, norm_ffn1[l], ffn1_w_up, ffn1_w_down, l, tm=1024, tf=256)
        uc, us, gt = _inproj(xs, mt, norm_mix[l], win, b_in[l], l, tm=512)

        cv = _conv_rows(uc, conv_dw[l], conv_db[l], rows_out=ROWS_LAT if last else ROWS)
        cv = _conv_vert(uc, conv_dw[l], conv_db[l], cv)
        if not last:
            cv = _conv_ctx(uc, conv_dw[l], conv_db[l], cv)

        b2, c2, lamtab = _ssm_tables(ssm_a_re[l], ssm_a_im[l], ssm_log_dt[l],
                                     ssm_b_re[l], ssm_b_im[l], ssm_c_re[l], ssm_c_im[l])
        yf, yb = _ssm(us, b2, c2, lamtab)

        mix = _mix(cv, yf, yb, us, gt, conv_ln_g[l], conv_ln_b[l], ssm_d[l], cwo, wglu, ssm_b_glu[l], l,
                   tm=256, latent_only=last)
        xs = _outproj(xs, mix, mt, wout, l, tm=512, latent_only=last)
        if last:
            xs = _ffn(xs, mt, 2, norm_ffn2[l], up_last, down_last, 0, tm=512, tf=512,
                      latent_only=True, final_gain=norm_final)
        else:
            xs = _ffn(xs, mt, 2, norm_ffn2[l], ffn2_w_up, ffn2_w_down, l, tm=1024, tf=256)

    return xs
```

```python
import functools

import jax
import jax.numpy as jnp
from jax import lax
from jax.experimental import pallas as pl
from jax.experimental.pallas import tpu as pltpu

F32 = jnp.float32
BF16 = jnp.bfloat16

D_MODEL = 2048
BATCH = 4
SEQ = 2048
CTX_LEN = 256
DEPTH = 2
GRID_W = 64
GRID_ROWS = SEQ // GRID_W
D_FF = 5632
CONV_DIM = 1024
CONV_K = 31
CONV_PAD = CONV_K // 2
SSM_DIM = 1024
SSM_GROUP = 16
SSM_GROUPS = 64
SSM_STATE = 64
N_MOD = 9
N_IN = 2 * CONV_DIM + SSM_DIM + 2 * D_MODEL
FFN_RES = 0.5
EPS = 1e-6

LANES = 128
SUBLANES = 8
ROWS_CTX = CTX_LEN * BATCH
ROWS_LAT = SEQ * BATCH
ROWS = ROWS_CTX + ROWS_LAT
ROWS_PER_GRID_ROW = GRID_W * BATCH

VMEM_LIMIT = 60 * 1024 * 1024

GROUPS_PER_SLAB = LANES // SSM_GROUP
N_GP = SSM_GROUPS // 2
GP_PER_SLAB = GROUPS_PER_SLAB // 2
N_SLAB = SSM_DIM // LANES
SSM_CHUNK_T = 64
SSM_CHUNK_ROWS = SSM_CHUNK_T * BATCH
SSM_STEP_ROWS = 2 * SSM_CHUNK_ROWS
N_CHUNK_CTX = CTX_LEN // SSM_CHUNK_T
N_CHUNK = (CTX_LEN + SEQ) // SSM_CHUNK_T
assert 2 * BATCH == SUBLANES


def _cparams(sem):
    return pltpu.CompilerParams(dimension_semantics=sem, vmem_limit_bytes=VMEM_LIMIT)


def _const_spec(shape):
    nd = len(shape)
    return pl.BlockSpec(shape, lambda *_: (0,) * nd, pipeline_mode=pl.Buffered(1))


def _mxu_operand(w_ref):
    w = w_ref[...]
    return w if w.dtype == BF16 else w.astype(BF16)


def _rows8(x, m, op):
    n, d = x.shape
    x3 = x.reshape(n // SUBLANES, SUBLANES, d)
    return op(x3, m[None]).reshape(n, d)


def _interleave_batch(src_ref, o_ref, slab_ref, tq):
    for c in range(D_MODEL // LANES):
        cs = slice(c * LANES, (c + 1) * LANES)
        for b in range(BATCH):
            slab_ref[c, pl.ds(b, tq, stride=BATCH), :] = src_ref[b, :, cs]
        o_ref[:, cs] = slab_ref[c]


def _to_rows_kernel(x_ref, c_ref, o_ref, slab_ref, *, n_lat_tiles, tq):
    i = pl.program_id(0)

    @pl.when(i < n_lat_tiles)
    def _():
        _interleave_batch(x_ref, o_ref, slab_ref, tq)

    @pl.when(i >= n_lat_tiles)
    def _():
        _interleave_batch(c_ref, o_ref, slab_ref, tq)


def _to_rows(x, ctx):
    tq = 64
    n_lat_tiles = SEQ // tq
    n_tiles = (SEQ + CTX_LEN) // tq
    return pl.pallas_call(
        functools.partial(_to_rows_kernel, n_lat_tiles=n_lat_tiles, tq=tq),
        grid=(n_tiles,),
        in_specs=[
            pl.BlockSpec((BATCH, tq, D_MODEL), lambda i: (0, jnp.minimum(i, n_lat_tiles - 1), 0)),
            pl.BlockSpec((BATCH, tq, D_MODEL), lambda i: (0, jnp.maximum(i - n_lat_tiles, 0), 0)),
        ],
        out_specs=pl.BlockSpec((tq * BATCH, D_MODEL), lambda i: (i, 0)),
        out_shape=jax.ShapeDtypeStruct((ROWS, D_MODEL), F32),
        scratch_shapes=[pltpu.VMEM((D_MODEL // LANES, tq * BATCH, LANES), F32)],
        compiler_params=_cparams(("arbitrary",)),
        name="to_rows",
    )(x, ctx)


def _ada_kernel(c_ref, w_ref, b_ref, o_ref):
    c = c_ref[...]
    s = jax.nn.silu(c).astype(BF16)
    o_ref[0] = jnp.dot(s, w_ref[0].astype(BF16), preferred_element_type=F32) + b_ref[0]


def _ada(c8, ada_w, ada_b):
    tn = 1024
    n = N_MOD * D_MODEL
    return pl.pallas_call(
        _ada_kernel,
        grid=(DEPTH, n // tn),
        in_specs=[
            pl.BlockSpec((SUBLANES, D_MODEL), lambda l, j: (0, 0)),
            pl.BlockSpec((1, D_MODEL, tn), lambda l, j: (l, 0, j)),
            pl.BlockSpec((1, 1, tn), lambda l, j: (l, 0, j)),
        ],
        out_specs=pl.BlockSpec((1, SUBLANES, tn), lambda l, j: (l, 0, j)),
        out_shape=jax.ShapeDtypeStruct((DEPTH, SUBLANES, n), F32),
        compiler_params=_cparams(("arbitrary", "arbitrary")),
        name="ada_mod",
    )(c8, ada_w, ada_b.reshape(DEPTH, 1, n))


def _norm_mod(x, gain_ref, mod_ref):
    y = x * lax.rsqrt(jnp.mean(x * x, axis=-1, keepdims=True) + EPS) * gain_ref[...]
    h = _rows8(y, 1.0 + mod_ref[1], jnp.multiply)
    return _rows8(h, mod_ref[0], jnp.add).astype(BF16)


def _ffn_chunk(hn_ref, wa_ref, wg_ref, wd_ref):
    hn = hn_ref[...]
    a = jnp.dot(hn, _mxu_operand(wa_ref), preferred_element_type=F32)
    g = jnp.dot(hn, _mxu_operand(wg_ref), preferred_element_type=F32)
    act = (jax.nn.silu(g) * a).astype(BF16)
    return jnp.dot(act, _mxu_operand(wd_ref), preferred_element_type=F32)


def _ffn_weight_specs(layer, tf, nf):
    return [
        pl.BlockSpec((None, D_MODEL, tf), lambda i, j: (layer, 0, j)),
        pl.BlockSpec((None, D_MODEL, tf), lambda i, j: (layer, 0, j + nf)),
        pl.BlockSpec((None, tf, D_MODEL), lambda i, j: (layer, j, 0)),
    ]


def _ffn_kernel(x_hbm, mod_ref, gain_ref, wa_ref, wg_ref, wd_ref, o_ref, hn_ref, xbuf_ref, sem,
                *, tm, n_tiles):
    i = pl.program_id(0)
    j = pl.program_id(1)

    def x_copy(tile):
        rows = pl.ds(pl.multiple_of(tile * tm, tm), tm)
        return pltpu.make_async_copy(x_hbm.at[rows, :], xbuf_ref, sem)

    def gated_chunk():
        return _rows8(_ffn_chunk(hn_ref, wa_ref, wg_ref, wd_ref), mod_ref[2] * FFN_RES, jnp.multiply)

    @pl.when(j == 0)
    def _():
        @pl.when(i == 0)
        def _():
            x_copy(0).start()

        x_copy(i).wait()
        x = xbuf_ref[...]
        hn_ref[...] = _norm_mod(x, gain_ref, mod_ref)
        o_ref[...] = x + gated_chunk()

    @pl.when(j == 1)
    def _():
        @pl.when(i + 1 < n_tiles)
        def _():
            x_copy(i + 1).start()

        o_ref[...] += gated_chunk()

    @pl.when(j > 1)
    def _():
        o_ref[...] += gated_chunk()


def _ffn(x, modtab, sub, gain, w_up, w_down, layer, *, tm, tf):
    nf = D_FF // tf
    assert nf >= 2
    n_lat_tiles = ROWS_LAT // tm
    n_tiles = ROWS // tm
    return pl.pallas_call(
        functools.partial(_ffn_kernel, tm=tm, n_tiles=n_tiles),
        grid=(n_tiles, nf),
        in_specs=[
            pl.BlockSpec(memory_space=pl.ANY),
            pl.BlockSpec((None, 3, SUBLANES, D_MODEL),
                         lambda i, j: (jnp.where(i < n_lat_tiles, 1, 0), sub, 0, 0)),
            pl.BlockSpec((1, D_MODEL), lambda i, j: (0, 0)),
        ] + _ffn_weight_specs(layer, tf, nf),
        out_specs=pl.BlockSpec((tm, D_MODEL), lambda i, j: (i, 0)),
        out_shape=jax.ShapeDtypeStruct((ROWS, D_MODEL), F32),
        scratch_shapes=[pltpu.VMEM((tm, D_MODEL), BF16), pltpu.VMEM((tm, D_MODEL), F32),
                        pltpu.SemaphoreType.DMA(())],
        compiler_params=_cparams(("arbitrary", "arbitrary")),
        name="ffn",
    )(x, modtab, gain.reshape(1, D_MODEL), w_up, w_up, w_down)


def _ffn_final_kernel(x_ref, mod_ref, gain_ref, wa_ref, wg_ref, wd_ref, fgain_ref, o_ref,
                      hn_ref, acc_ref, slab_ref, *, nf):
    j = pl.program_id(1)

    @pl.when(j == 0)
    def _():
        hn_ref[...] = _norm_mod(x_ref[...], gain_ref, mod_ref)
        acc_ref[...] = _ffn_chunk(hn_ref, wa_ref, wg_ref, wd_ref)

    @pl.when((j > 0) & (j < nf - 1))
    def _():
        acc_ref[...] += _ffn_chunk(hn_ref, wa_ref, wg_ref, wd_ref)

    @pl.when(j == nf - 1)
    def _():
        acc = acc_ref[...] + _ffn_chunk(hn_ref, wa_ref, wg_ref, wd_ref)
        xn = x_ref[...] + _rows8(acc, mod_ref[2] * FFN_RES, jnp.multiply)
        xn = xn * lax.rsqrt(jnp.mean(xn * xn, axis=-1, keepdims=True) + EPS) * fgain_ref[...]
        tq = acc_ref.shape[0] // BATCH
        for c in range(D_MODEL // LANES):
            cs = slice(c * LANES, (c + 1) * LANES)
            slab_ref[c] = xn[:, cs]
            for b in range(BATCH):
                o_ref[b, :, cs] = slab_ref[c, pl.ds(b, tq, stride=BATCH), :]


def _ffn_final(x, modtab, sub, gain, w_up, w_down, layer, final_gain, *, tm, tf):
    nf = D_FF // tf
    assert nf >= 2
    vec = pl.BlockSpec((1, D_MODEL), lambda i, j: (0, 0))
    return pl.pallas_call(
        functools.partial(_ffn_final_kernel, nf=nf),
        grid=(ROWS_LAT // tm, nf),
        in_specs=[
            pl.BlockSpec((tm, D_MODEL), lambda i, j: (i, 0)),
            pl.BlockSpec((None, 3, SUBLANES, D_MODEL), lambda i, j: (1, sub, 0, 0)),
            vec,
        ] + _ffn_weight_specs(layer, tf, nf) + [vec],
        out_specs=pl.BlockSpec((BATCH, tm // BATCH, D_MODEL), lambda i, j: (0, i, 0)),
        out_shape=jax.ShapeDtypeStruct((BATCH, SEQ, D_MODEL), F32),
        scratch_shapes=[pltpu.VMEM((tm, D_MODEL), BF16), pltpu.VMEM((tm, D_MODEL), F32),
                        pltpu.VMEM((D_MODEL // LANES, tm, LANES), F32)],
        compiler_params=_cparams(("arbitrary", "arbitrary")),
        name="ffn_final",
    )(x, modtab, gain.reshape(1, D_MODEL), w_up, w_up, w_down, final_gain.reshape(1, D_MODEL))


def _cast_kernel(w_ref, o_ref):
    o_ref[...] = w_ref[...].astype(BF16)


def _cast_layer_bf16(w, layer, rows_blk):
    _, r, c = w.shape
    return pl.pallas_call(
        _cast_kernel,
        grid=(r // rows_blk,),
        in_specs=[pl.BlockSpec((None, rows_blk, c), lambda i: (layer, i, 0))],
        out_specs=pl.BlockSpec((None, rows_blk, c), lambda i: (0, i, 0)),
        out_shape=jax.ShapeDtypeStruct((1, r, c), BF16),
        compiler_params=_cparams(("arbitrary",)),
        name="cast_bf16",
    )(w)


def _inproj_kernel(x_ref, mod_ref, gain_ref, w_ref, b_ref, uc_ref, us_ref, gt_ref, hn_ref):
    j = pl.program_id(1)

    @pl.when(j == 0)
    def _():
        hn_ref[...] = _norm_mod(x_ref[...], gain_ref, mod_ref)

    def proj():
        return jnp.dot(hn_ref[...], _mxu_operand(w_ref), preferred_element_type=F32) + b_ref[0]

    @pl.when(j == 0)
    def _():
        uc_ref[...] = proj()

    @pl.when(j == 1)
    def _():
        uc_ref[...] = uc_ref[...] * jax.nn.sigmoid(proj())

    @pl.when(j == 2)
    def _():
        us_ref[...] = proj()

    @pl.when(j >= 3)
    def _():
        gt_ref[...] = jax.nn.sigmoid(proj()).astype(BF16)


def _inproj(x, modtab, gain, w_in, b_in, layer, *, tm):
    tn = 1024
    n_lat_tiles = ROWS_LAT // tm
    nblk = N_IN // tn
    n_head = nblk - 2 * D_MODEL // tn
    return pl.pallas_call(
        _inproj_kernel,
        grid=(ROWS // tm, nblk),
        in_specs=[
            pl.BlockSpec((tm, D_MODEL), lambda i, j: (i, 0)),
            pl.BlockSpec((None, 3, SUBLANES, D_MODEL),
                         lambda i, j: (jnp.where(i < n_lat_tiles, 1, 0), 1, 0, 0)),
            pl.BlockSpec((1, D_MODEL), lambda i, j: (0, 0)),
            pl.BlockSpec((None, D_MODEL, tn), lambda i, j: (layer, 0, j)),
            pl.BlockSpec((1, 1, tn), lambda i, j: (j, 0, 0)),
        ],
        out_specs=[
            pl.BlockSpec((tm, tn), lambda i, j: (i, 0)),
            pl.BlockSpec((tm, tn), lambda i, j: (i, 0)),
            pl.BlockSpec((tm, tn), lambda i, j: (i, jnp.maximum(j - n_head, 0))),
        ],
        out_shape=[
            jax.ShapeDtypeStruct((ROWS, CONV_DIM), F32),
            jax.ShapeDtypeStruct((ROWS, SSM_DIM), F32),
            jax.ShapeDtypeStruct((ROWS, 2 * D_MODEL), BF16),
        ],
        scratch_shapes=[pltpu.VMEM((tm, D_MODEL), BF16)],
        compiler_params=_cparams(("arbitrary", "arbitrary")),
        name="mixer_inproj",
    )(x, modtab, gain.reshape(1, D_MODEL), w_in, b_in.reshape(nblk, 1, tn))


CONV_ROW_PAD = 64
CONV_SUB_ROWS = 64


CONV_HALF = CONV_DIM // 2


def _conv_seq_taps(u_ref, w_ref, b_ref, o_ref, pad_ref, pad4_ref, *, n, cb):
    zeros = jnp.zeros((CONV_ROW_PAD, cb), F32)
    pad_ref[0:CONV_ROW_PAD, :] = zeros
    pad_ref[CONV_ROW_PAD + n:, :] = zeros
    pad_ref[CONV_ROW_PAD:CONV_ROW_PAD + n, :] = u_ref[...]
    n_pad = n + 2 * CONV_ROW_PAD
    pad4_ref[0:n_pad - SUBLANES, :] = pad_ref[BATCH:n_pad - SUBLANES + BATCH, :]
    base = CONV_ROW_PAD - CONV_PAD * BATCH
    for c in range(cb // LANES):
        cs = slice(c * LANES, (c + 1) * LANES)
        for r in range(n // CONV_SUB_ROWS):
            r0 = r * CONV_SUB_ROWS
            acc = jnp.broadcast_to(b_ref[:, cs], (CONV_SUB_ROWS, LANES))
            for k in range(CONV_K):
                off = base + r0 + k * BATCH
                if off % SUBLANES == 0:
                    tap = pad_ref[off:off + CONV_SUB_ROWS, cs]
                else:
                    tap = pad4_ref[off - BATCH:off - BATCH + CONV_SUB_ROWS, cs]
                acc = acc + w_ref[k:k + 1, cs] * tap
            o_ref[r0:r0 + CONV_SUB_ROWS, cs] = acc


def _conv_rows_kernel(u_ref, w_ref, b_ref, o_ref, pad_ref, pad4_ref):
    i = pl.program_id(0)
    o_ref[:, CONV_HALF:] = jnp.zeros((ROWS_PER_GRID_ROW, CONV_DIM - CONV_HALF), F32)

    @pl.when(i < GRID_ROWS)
    def _():
        _conv_seq_taps(u_ref, w_ref, b_ref, o_ref, pad_ref, pad4_ref, n=ROWS_PER_GRID_ROW, cb=CONV_HALF)

    @pl.when(i >= GRID_ROWS)
    def _():
        o_ref[:, :CONV_HALF] = jnp.zeros((ROWS_PER_GRID_ROW, CONV_HALF), F32)


def _conv_rows(u, w, b, *, rows_out):
    n = ROWS_PER_GRID_ROW
    n_pad = n + 2 * CONV_ROW_PAD
    return pl.pallas_call(
        _conv_rows_kernel,
        grid=(rows_out // n,),
        in_specs=[
            pl.BlockSpec((n, CONV_HALF), lambda i: (jnp.minimum(i, GRID_ROWS - 1), 0)),
            pl.BlockSpec((CONV_K, CONV_HALF), lambda i: (0, 0)),
            pl.BlockSpec((1, CONV_HALF), lambda i: (0, 0)),
        ],
        out_specs=pl.BlockSpec((n, CONV_DIM), lambda i: (i, 0)),
        out_shape=jax.ShapeDtypeStruct((rows_out, CONV_DIM), F32),
        scratch_shapes=[pltpu.VMEM((n_pad, CONV_HALF), F32), pltpu.VMEM((n_pad, CONV_HALF), F32)],
        compiler_params=_cparams(("arbitrary",)),
        name="conv_rows",
    )(u, w, b.reshape(1, CONV_DIM))


def _conv_ctx_kernel(u_ref, w_ref, b_ref, prev_ref, o_ref, pad_ref, pad4_ref):
    del prev_ref
    _conv_seq_taps(u_ref, w_ref, b_ref, o_ref, pad_ref, pad4_ref, n=ROWS_CTX, cb=LANES)


def _conv_ctx(u, w, b, prev):
    blk0 = ROWS_LAT // ROWS_CTX
    n_pad = ROWS_CTX + 2 * CONV_ROW_PAD
    return pl.pallas_call(
        _conv_ctx_kernel,
        grid=(CONV_DIM // LANES,),
        in_specs=[
            pl.BlockSpec((ROWS_CTX, LANES), lambda c: (blk0, c)),
            pl.BlockSpec((CONV_K, LANES), lambda c: (0, c)),
            pl.BlockSpec((1, LANES), lambda c: (0, c)),
            pl.BlockSpec(memory_space=pl.ANY),
        ],
        out_specs=pl.BlockSpec((ROWS_CTX, LANES), lambda c: (blk0, c)),
        out_shape=jax.ShapeDtypeStruct((ROWS, CONV_DIM), F32),
        scratch_shapes=[pltpu.VMEM((n_pad, LANES), F32), pltpu.VMEM((n_pad, LANES), F32)],
        input_output_aliases={3: 0},
        compiler_params=_cparams(("arbitrary",)),
        name="conv_ctx",
    )(u, w, b.reshape(1, CONV_DIM), prev)


def _conv_vert_kernel(u_ref, w_ref, b_ref, prev_ref, o_ref, pad_ref):
    del prev_ref
    rpg = ROWS_PER_GRID_ROW
    zeros = jnp.zeros((CONV_PAD * rpg, LANES), F32)
    pad_ref[0:CONV_PAD * rpg, :] = zeros
    pad_ref[(CONV_PAD + GRID_ROWS) * rpg:, :] = zeros
    pad_ref[CONV_PAD * rpg:(CONV_PAD + GRID_ROWS) * rpg, :] = u_ref[...]
    bias = jnp.broadcast_to(b_ref[...], (CONV_SUB_ROWS, LANES))

    def body(r, carry):
        for s in range(rpg // CONV_SUB_ROWS):
            acc = bias
            for k in range(CONV_K):
                off = pl.multiple_of((r + k) * rpg + s * CONV_SUB_ROWS, CONV_SUB_ROWS)
                acc = acc + w_ref[k:k + 1, :] * pad_ref[pl.ds(off, CONV_SUB_ROWS), :]
            o0 = pl.multiple_of(r * rpg + s * CONV_SUB_ROWS, CONV_SUB_ROWS)
            o_ref[pl.ds(o0, CONV_SUB_ROWS), :] = acc
        return carry

    lax.fori_loop(0, GRID_ROWS, body, 0)


def _conv_vert(u, w, b, prev):
    half_blk = (CONV_DIM // 2) // LANES
    return pl.pallas_call(
        _conv_vert_kernel,
        grid=(half_blk,),
        in_specs=[
            pl.BlockSpec((ROWS_LAT, LANES), lambda c: (0, c + half_blk)),
            pl.BlockSpec((CONV_K, LANES), lambda c: (0, c + half_blk)),
            pl.BlockSpec((1, LANES), lambda c: (0, c + half_blk)),
            pl.BlockSpec(memory_space=pl.ANY),
        ],
        out_specs=pl.BlockSpec((ROWS_LAT, LANES), lambda c: (0, c + half_blk)),
        out_shape=jax.ShapeDtypeStruct(prev.shape, F32),
        scratch_shapes=[pltpu.VMEM(((GRID_ROWS + 2 * CONV_PAD) * ROWS_PER_GRID_ROW, LANES), F32)],
        input_output_aliases={3: 0},
        compiler_params=_cparams(("arbitrary",)),
        name="conv_vert",
    )(u, w, b.reshape(1, CONV_DIM), prev)


def _ssm_kernel(uf_ref, ub_ref, b2_ref, c2_ref, lam_ref, yf_ref, yb_ref,
                lhs_ref, s_ref, y2_ref, hst_ref):
    k = pl.program_id(0)

    @pl.when(k == 0)
    def _():
        hst_ref[...] = jnp.zeros_like(hst_ref)

    n_pair = SSM_CHUNK_T // 2
    low = lax.broadcasted_iota(jnp.int32, (SUBLANES, SSM_DIM), 0) < BATCH
    zero = jnp.zeros((SUBLANES, SSM_DIM), F32)
    for m in range(n_pair):
        f = uf_ref[m * SUBLANES:(m + 1) * SUBLANES, :]
        mb = n_pair - 1 - m
        g = ub_ref[mb * SUBLANES:(mb + 1) * SUBLANES, :]
        fr = pltpu.roll(f, BATCH, 0)
        gr = pltpu.roll(g, BATCH, 0)
        fwd = jnp.concatenate([jnp.where(low, f, zero), jnp.where(low, fr, zero)], axis=0).astype(BF16)
        bwd = jnp.concatenate([jnp.where(low, zero, g), jnp.where(low, zero, gr)], axis=0).astype(BF16)
        r0 = m * 2 * SUBLANES
        for j in range(N_SLAB):
            cs = slice(j * LANES, (j + 1) * LANES)
            lhs_ref[r0:r0 + 2 * SUBLANES, 2 * j * LANES:(2 * j + 1) * LANES] = fwd[:, cs]
            lhs_ref[r0:r0 + 2 * SUBLANES, (2 * j + 1) * LANES:(2 * j + 2) * LANES] = bwd[:, cs]

    re = slice(0, LANES)
    im = slice(LANES, 2 * LANES)

    def project(j):
        for gp in range(j * GP_PER_SLAB, (j + 1) * GP_PER_SLAB):
            s_ref[gp] = jnp.dot(lhs_ref[:, 2 * j * LANES:(2 * j + 2) * LANES], b2_ref[gp],
                                preferred_element_type=F32)

    def scan(j):
        for gp in range(j * GP_PER_SLAB, (j + 1) * GP_PER_SLAB):
            lr, li = lam_ref[gp, 0], lam_ref[gp, 1]
            hre, him = hst_ref[gp, 0], hst_ref[gp, 1]
            for s in range(SSM_CHUNK_T):
                rows = slice(s * SUBLANES, (s + 1) * SUBLANES)
                nre = lr * hre - li * him + s_ref[gp, rows, re]
                nim = lr * him + li * hre + s_ref[gp, rows, im]
                s_ref[gp, rows, re] = nre
                s_ref[gp, rows, im] = nim
                hre, him = nre, nim
            hst_ref[gp, 0] = hre
            hst_ref[gp, 1] = him

    def read_out(j):
        acc = None
        for gp in range(j * GP_PER_SLAB, (j + 1) * GP_PER_SLAB):
            p = jnp.dot(s_ref[gp].astype(BF16), c2_ref[gp], preferred_element_type=F32)
            acc = p if acc is None else acc + p
        y2_ref[:, 2 * j * LANES:(2 * j + 2) * LANES] = acc

    project(0)
    for j in range(N_SLAB):
        if j + 1 < N_SLAB:
            project(j + 1)
        scan(j)
        if j >= 1:
            read_out(j - 1)
    read_out(N_SLAB - 1)

    low1 = lax.broadcasted_iota(jnp.int32, (SUBLANES, LANES), 0) < BATCH
    for m in range(n_pair):
        mb = n_pair - 1 - m
        r0 = m * 2 * SUBLANES
        for j in range(N_SLAB):
            cf = slice(2 * j * LANES, (2 * j + 1) * LANES)
            cb = slice((2 * j + 1) * LANES, (2 * j + 2) * LANES)
            cs = slice(j * LANES, (j + 1) * LANES)
            ef = y2_ref[r0:r0 + SUBLANES, cf]
            of = pltpu.roll(y2_ref[r0 + SUBLANES:r0 + 2 * SUBLANES, cf], BATCH, 0)
            yf_ref[m * SUBLANES:(m + 1) * SUBLANES, cs] = jnp.where(low1, ef, of)
            eb = y2_ref[r0:r0 + SUBLANES, cb]
            ob = pltpu.roll(y2_ref[r0 + SUBLANES:r0 + 2 * SUBLANES, cb], BATCH, 0)
            yb_ref[mb * SUBLANES:(mb + 1) * SUBLANES, cs] = jnp.where(low1, ob, eb)


def _ssm(us, b2, c2, lamtab):
    n_lat = N_CHUNK - N_CHUNK_CTX

    def fwd_blk(k):
        return jnp.where(k < N_CHUNK_CTX, n_lat + k, k - N_CHUNK_CTX)

    def bwd_blk(k):
        return N_CHUNK - 1 - k

    blk = (SSM_CHUNK_ROWS, SSM_DIM)
    return pl.pallas_call(
        _ssm_kernel,
        grid=(N_CHUNK,),
        in_specs=[
            pl.BlockSpec(blk, lambda k: (fwd_blk(k), 0)),
            pl.BlockSpec(blk, lambda k: (bwd_blk(k), 0)),
            _const_spec(b2.shape),
            _const_spec(c2.shape),
            _const_spec(lamtab.shape),
        ],
        out_specs=[
            pl.BlockSpec(blk, lambda k: (fwd_blk(k), 0)),
            pl.BlockSpec(blk, lambda k: (bwd_blk(k), 0)),
        ],
        out_shape=[jax.ShapeDtypeStruct((ROWS, SSM_DIM), F32)] * 2,
        scratch_shapes=[
            pltpu.VMEM((SSM_STEP_ROWS, 2 * SSM_DIM), BF16),
            pltpu.VMEM((N_GP, SSM_STEP_ROWS, 2 * LANES), F32),
            pltpu.VMEM((SSM_STEP_ROWS, 2 * SSM_DIM), F32),
            pltpu.VMEM((N_GP, 2, SUBLANES, LANES), F32),
        ],
        compiler_params=_cparams(("arbitrary",)),
        name="s5_scan",
    )(us, us, b2, c2, lamtab)


def _ssm_tables(a_re, a_im, log_dt, b_re, b_im, c_re, c_im):
    lam = lax.complex(a_re.astype(F32), a_im.astype(F32))
    dt = jnp.exp(log_dt.astype(F32))[..., None]
    lam_bar = jnp.exp(lam * dt)
    bmat_c = lax.complex(b_re.astype(F32), b_im.astype(F32))
    b_bar = ((lam_bar - 1) / lam)[..., None] * bmat_c

    def pair_rows(v):
        v = v.reshape(2, N_GP, 2, SSM_GROUP, SSM_STATE).transpose(1, 0, 2, 3, 4)
        z = jnp.zeros_like(v[:, :, 0])
        q0 = jnp.concatenate([v[:, :, 0], z], axis=-1)
        q1 = jnp.concatenate([z, v[:, :, 1]], axis=-1)
        return jnp.stack([q0, q1], axis=2)

    slab_pos = (jnp.arange(N_GP) % GP_PER_SLAB)[:, None] == jnp.arange(GP_PER_SLAB)[None, :]
    place = slab_pos.astype(F32)[:, None, :, None, None, None]

    def place_rows(re_v, im_v):
        t = jnp.concatenate([pair_rows(re_v), pair_rows(im_v)], axis=-1)
        t = t[:, :, None] * place
        return t.reshape(N_GP, 2 * LANES, 2 * LANES)

    bt = jnp.swapaxes(b_bar, -1, -2)
    b2 = place_rows(jnp.real(bt), jnp.imag(bt))
    c2 = jnp.swapaxes(place_rows(c_re.astype(F32), -c_im.astype(F32)), 1, 2)
    lam_ri = jnp.stack([jnp.real(lam_bar), jnp.imag(lam_bar)], axis=0)
    lam_ri = lam_ri.reshape(2, 2, N_GP, 1, LANES).transpose(2, 0, 1, 3, 4)
    lamtab = jnp.broadcast_to(lam_ri, (N_GP, 2, 2, BATCH, LANES)).reshape(N_GP, 2, SUBLANES, LANES)
    return b2.astype(BF16), c2.astype(BF16), lamtab


def _mix_kernel(cv_ref, yf_ref, yb_ref, us_ref, gt_ref, lng_ref, lnb_ref, sd_ref,
                cwo_ref, wglu_ref, bglu_ref, o_ref):
    cv = cv_ref[...]
    mu = jnp.mean(cv, axis=-1, keepdims=True)
    var = jnp.mean(jnp.square(cv - mu), axis=-1, keepdims=True)
    ln = (cv - mu) * lax.rsqrt(var + EPS) * lng_ref[...] + lnb_ref[...]
    y_conv = jnp.dot(jax.nn.silu(ln).astype(BF16), cwo_ref[...], preferred_element_type=F32)
    y = yf_ref[...] + yb_ref[...] + sd_ref[...] * us_ref[...]
    z = jnp.dot(jax.nn.gelu(y).astype(BF16), wglu_ref[...], preferred_element_type=F32) + bglu_ref[...]
    y_ssm = z[:, :D_MODEL] * jax.nn.sigmoid(z[:, D_MODEL:])
    gt = gt_ref[...].astype(F32)
    o_ref[...] = (gt[:, :D_MODEL] * y_conv + gt[:, D_MODEL:] * y_ssm).astype(BF16)


def _mix(cv, yf, yb, us, gt, ln_g, ln_b, ssm_d, cwo, wglu, bglu, layer, *, tm, latent_only):
    n_tiles = (ROWS_LAT if latent_only else ROWS) // tm
    row = lambda w: pl.BlockSpec((tm, w), lambda i: (i, 0))
    vec = lambda w: pl.BlockSpec((1, w), lambda i: (0, 0), pipeline_mode=pl.Buffered(1))
    wgt = lambda r, c: pl.BlockSpec((None, r, c), lambda i: (layer, 0, 0), pipeline_mode=pl.Buffered(1))
    return pl.pallas_call(
        _mix_kernel,
        grid=(n_tiles,),
        in_specs=[
            row(CONV_DIM), row(SSM_DIM), row(SSM_DIM), row(SSM_DIM), row(2 * D_MODEL),
            vec(CONV_DIM), vec(CONV_DIM), vec(SSM_DIM),
            wgt(CONV_DIM, D_MODEL), wgt(SSM_DIM, 2 * D_MODEL), vec(2 * D_MODEL),
        ],
        out_specs=row(D_MODEL),
        out_shape=jax.ShapeDtypeStruct((n_tiles * tm, D_MODEL), BF16),
        compiler_params=_cparams(("arbitrary",)),
        name="mixer_mix",
    )(cv, yf, yb, us, gt, ln_g.reshape(1, -1), ln_b.reshape(1, -1), ssm_d.reshape(1, -1),
      cwo, wglu, bglu.reshape(1, -1))


def _outproj_kernel(x_ref, mix_ref, mod_ref, w_ref, o_ref):
    y = jnp.dot(mix_ref[...], w_ref[...], preferred_element_type=F32)
    o_ref[...] = x_ref[...] + _rows8(y, mod_ref[2], jnp.multiply)


def _outproj(x, mix, modtab, w_out, layer, *, tm, latent_only):
    n_lat_tiles = ROWS_LAT // tm
    n_tiles = n_lat_tiles if latent_only else ROWS // tm
    return pl.pallas_call(
        _outproj_kernel,
        grid=(n_tiles,),
        in_specs=[
            pl.BlockSpec((tm, D_MODEL), lambda i: (i, 0)),
            pl.BlockSpec((tm, D_MODEL), lambda i: (i, 0)),
            pl.BlockSpec((None, 3, SUBLANES, D_MODEL),
                         lambda i: (jnp.where(i < n_lat_tiles, 1, 0), 1, 0, 0)),
            pl.BlockSpec((None, D_MODEL, D_MODEL), lambda i: (layer, 0, 0)),
        ],
        out_specs=pl.BlockSpec((tm, D_MODEL), lambda i: (i, 0)),
        out_shape=jax.ShapeDtypeStruct((ROWS, D_MODEL), F32),
        input_output_aliases={0: 0},
        compiler_params=_cparams(("arbitrary",)),
        name="mixer_outproj",
    )(x, mix, modtab, w_out)


def kernel(x, c, ctx, c_ctx, ada_w, ada_b, norm_ffn1, ffn1_w_up, ffn1_w_down, norm_mix, w_in, b_in, conv_dw, conv_db, conv_ln_g, conv_ln_b, conv_w_out, ssm_a_re, ssm_a_im, ssm_log_dt, ssm_b_re, ssm_b_im, ssm_c_re, ssm_c_im, ssm_d, ssm_w_glu, ssm_b_glu, w_out, norm_ffn2, ffn2_w_up, ffn2_w_down, norm_final):
    xs = _to_rows(x, ctx)

    c8 = jnp.zeros((SUBLANES, D_MODEL), F32).at[:BATCH].set(c).at[BATCH].set(c_ctx)
    mod = _ada(c8, ada_w, ada_b).reshape(DEPTH, SUBLANES, N_MOD, D_MODEL)
    lat_rows = jnp.arange(SUBLANES) % BATCH
    mod_lat = mod[:, lat_rows].transpose(0, 2, 1, 3)
    mod_ctx = jnp.broadcast_to(mod[:, BATCH][:, :, None, :], mod_lat.shape)
    modtab = jnp.stack([mod_ctx, mod_lat], axis=1)

    up_last = _cast_layer_bf16(ffn2_w_up, DEPTH - 1, 256)
    down_last = _cast_layer_bf16(ffn2_w_down, DEPTH - 1, 512)
    win, cwo = w_in.astype(BF16), conv_w_out.astype(BF16)
    wglu, wout = ssm_w_glu.astype(BF16), w_out.astype(BF16)

    for l in range(DEPTH):
        last = l == DEPTH - 1
        mt = modtab[l]
        xs = _ffn(xs, mt, 0, norm_ffn1[l], ffn1_w_up, ffn1_w_down, l, tm=1024, tf=512)
        uc, us, gt = _inproj(xs, mt, norm_mix[l], win, b_in[l], l, tm=1024)

        cv = _conv_rows(uc, conv_dw[l], conv_db[l], rows_out=ROWS_LAT if last else ROWS)
        cv = _conv_vert(uc, conv_dw[l], conv_db[l], cv)
        if not last:
            cv = _conv_ctx(uc, conv_dw[l], conv_db[l], cv)

        b2, c2, lamtab = _ssm_tables(ssm_a_re[l], ssm_a_im[l], ssm_log_dt[l],
                                     ssm_b_re[l], ssm_b_im[l], ssm_c_re[l], ssm_c_im[l])
        yf, yb = _ssm(us, b2, c2, lamtab)

        mix = _mix(cv, yf, yb, us, gt, conv_ln_g[l], conv_ln_b[l], ssm_d[l], cwo, wglu, ssm_b_glu[l], l,
                   tm=256, latent_only=last)
        xs = _outproj(xs, mix, mt, wout, l, tm=512, latent_only=last)
        if last:
            xs = _ffn_final(xs, mt, 2, norm_ffn2[l], up_last, down_last, 0, norm_final, tm=512, tf=512)
        else:
            xs = _ffn(xs, mt, 2, norm_ffn2[l], ffn2_w_up, ffn2_w_down, l, tm=1024, tf=512)

    return xs
```

```python
import functools

import jax
import jax.numpy as jnp
from jax import lax
from jax.experimental import pallas as pl
from jax.experimental.pallas import tpu as pltpu

F32 = jnp.float32
BF16 = jnp.bfloat16

D_MODEL = 2048
BATCH = 4
SEQ = 2048
CTX_LEN = 256
DEPTH = 2
GRID_W = 64
GRID_ROWS = SEQ // GRID_W
D_FF = 5632
CONV_DIM = 1024
CONV_K = 31
CONV_PAD = CONV_K // 2
SSM_DIM = 1024
SSM_GROUP = 16
SSM_GROUPS = 64
SSM_STATE = 64
N_MOD = 9
N_IN = 2 * CONV_DIM + SSM_DIM + 2 * D_MODEL
FFN_RES = 0.5
EPS = 1e-6

LANES = 128
SUBLANES = 8
ROWS_CTX = CTX_LEN * BATCH
ROWS_LAT = SEQ * BATCH
ROWS = ROWS_CTX + ROWS_LAT
ROWS_PER_GRID_ROW = GRID_W * BATCH

VMEM_LIMIT = 60 * 1024 * 1024

GROUPS_PER_SLAB = LANES // SSM_GROUP
N_GP = SSM_GROUPS // 2
GP_PER_SLAB = GROUPS_PER_SLAB // 2
N_SLAB = SSM_DIM // LANES
SSM_CHUNK_T = 64
SSM_CHUNK_ROWS = SSM_CHUNK_T * BATCH
SSM_STEP_ROWS = 2 * SSM_CHUNK_ROWS
N_CHUNK_CTX = CTX_LEN // SSM_CHUNK_T
N_CHUNK = (CTX_LEN + SEQ) // SSM_CHUNK_T
assert 2 * BATCH == SUBLANES


def _cparams(sem):
    return pltpu.CompilerParams(dimension_semantics=sem, vmem_limit_bytes=VMEM_LIMIT)


def _const_spec(shape):
    nd = len(shape)
    return pl.BlockSpec(shape, lambda *_: (0,) * nd, pipeline_mode=pl.Buffered(1))


def _mxu_operand(w_ref):
    w = w_ref[...]
    return w if w.dtype == BF16 else w.astype(BF16)


def _rows8(x, m, op):
    n, d = x.shape
    x3 = x.reshape(n // SUBLANES, SUBLANES, d)
    return op(x3, m[None]).reshape(n, d)


def _interleave_batch(src_ref, o_ref, slab_ref, tq):
    for c in range(D_MODEL // LANES):
        cs = slice(c * LANES, (c + 1) * LANES)
        for b in range(BATCH):
            slab_ref[c, pl.ds(b, tq, stride=BATCH), :] = src_ref[b, :, cs]
        o_ref[:, cs] = slab_ref[c]


def _to_rows_kernel(x_ref, c_ref, o_ref, slab_ref, *, n_lat_tiles, tq):
    i = pl.program_id(0)

    @pl.when(i < n_lat_tiles)
    def _():
        _interleave_batch(x_ref, o_ref, slab_ref, tq)

    @pl.when(i >= n_lat_tiles)
    def _():
        _interleave_batch(c_ref, o_ref, slab_ref, tq)


def _to_rows(x, ctx):
    tq = 64
    n_lat_tiles = SEQ // tq
    n_tiles = (SEQ + CTX_LEN) // tq
    return pl.pallas_call(
        functools.partial(_to_rows_kernel, n_lat_tiles=n_lat_tiles, tq=tq),
        grid=(n_tiles,),
        in_specs=[
            pl.BlockSpec((BATCH, tq, D_MODEL), lambda i: (0, jnp.minimum(i, n_lat_tiles - 1), 0)),
            pl.BlockSpec((BATCH, tq, D_MODEL), lambda i: (0, jnp.maximum(i - n_lat_tiles, 0), 0)),
        ],
        out_specs=pl.BlockSpec((tq * BATCH, D_MODEL), lambda i: (i, 0)),
        out_shape=jax.ShapeDtypeStruct((ROWS, D_MODEL), F32),
        scratch_shapes=[pltpu.VMEM((D_MODEL // LANES, tq * BATCH, LANES), F32)],
        compiler_params=_cparams(("arbitrary",)),
        name="to_rows",
    )(x, ctx)


def _final_norm_kernel(x_ref, gain_ref, o_ref, slab_ref, *, tq):
    x = x_ref[...]
    xn = x * lax.rsqrt(jnp.mean(x * x, axis=-1, keepdims=True) + EPS) * gain_ref[...]
    for c in range(D_MODEL // LANES):
        cs = slice(c * LANES, (c + 1) * LANES)
        slab_ref[c] = xn[:, cs]
        for b in range(BATCH):
            o_ref[b, :, cs] = slab_ref[c, pl.ds(b, tq, stride=BATCH), :]


def _final_norm(xs, gain):
    tq = 64
    return pl.pallas_call(
        functools.partial(_final_norm_kernel, tq=tq),
        grid=(SEQ // tq,),
        in_specs=[pl.BlockSpec((tq * BATCH, D_MODEL), lambda i: (i, 0)),
                  pl.BlockSpec((1, D_MODEL), lambda i: (0, 0))],
        out_specs=pl.BlockSpec((BATCH, tq, D_MODEL), lambda i: (0, i, 0)),
        out_shape=jax.ShapeDtypeStruct((BATCH, SEQ, D_MODEL), F32),
        scratch_shapes=[pltpu.VMEM((D_MODEL // LANES, tq * BATCH, LANES), F32)],
        compiler_params=_cparams(("arbitrary",)),
        name="final_norm",
    )(xs, gain.reshape(1, D_MODEL))


def _ada_kernel(c_ref, w_ref, b_ref, o_ref):
    c = c_ref[...]
    s = jax.nn.silu(c).astype(BF16)
    o_ref[0] = jnp.dot(s, w_ref[0].astype(BF16), preferred_element_type=F32) + b_ref[0]


def _ada(c8, ada_w, ada_b):
    tn = 1024
    n = N_MOD * D_MODEL
    return pl.pallas_call(
        _ada_kernel,
        grid=(DEPTH, n // tn),
        in_specs=[
            pl.BlockSpec((SUBLANES, D_MODEL), lambda l, j: (0, 0)),
            pl.BlockSpec((1, D_MODEL, tn), lambda l, j: (l, 0, j)),
            pl.BlockSpec((1, 1, tn), lambda l, j: (l, 0, j)),
        ],
        out_specs=pl.BlockSpec((1, SUBLANES, tn), lambda l, j: (l, 0, j)),
        out_shape=jax.ShapeDtypeStruct((DEPTH, SUBLANES, n), F32),
        compiler_params=_cparams(("arbitrary", "arbitrary")),
        name="ada_mod",
    )(c8, ada_w, ada_b.reshape(DEPTH, 1, n))


def _norm_mod(x, gain_ref, mod_ref):
    y = x * lax.rsqrt(jnp.mean(x * x, axis=-1, keepdims=True) + EPS) * gain_ref[...]
    h = _rows8(y, 1.0 + mod_ref[1], jnp.multiply)
    return _rows8(h, mod_ref[0], jnp.add).astype(BF16)


def _ffn_chunk(hn_ref, wa_ref, wg_ref, wd_ref):
    hn = hn_ref[...]
    a = jnp.dot(hn, _mxu_operand(wa_ref), preferred_element_type=F32)
    g = jnp.dot(hn, _mxu_operand(wg_ref), preferred_element_type=F32)
    act = (jax.nn.silu(g) * a).astype(BF16)
    return jnp.dot(act, _mxu_operand(wd_ref), preferred_element_type=F32)


def _ffn_kernel(x_hbm, mod_ref, gain_ref, wa_ref, wg_ref, wd_ref, o_ref, hn_ref, xbuf_ref, sem,
                *, tm, n_tiles):
    i = pl.program_id(0)
    j = pl.program_id(1)

    def x_copy(tile):
        rows = pl.ds(pl.multiple_of(tile * tm, tm), tm)
        return pltpu.make_async_copy(x_hbm.at[rows, :], xbuf_ref, sem)

    def gated_chunk():
        return _rows8(_ffn_chunk(hn_ref, wa_ref, wg_ref, wd_ref), mod_ref[2] * FFN_RES, jnp.multiply)

    @pl.when(j == 0)
    def _():
        @pl.when(i == 0)
        def _():
            x_copy(0).start()

        x_copy(i).wait()
        x = xbuf_ref[...]
        hn_ref[...] = _norm_mod(x, gain_ref, mod_ref)
        o_ref[...] = x + gated_chunk()

    @pl.when(j == 1)
    def _():
        @pl.when(i + 1 < n_tiles)
        def _():
            x_copy(i + 1).start()

        o_ref[...] += gated_chunk()

    @pl.when(j > 1)
    def _():
        o_ref[...] += gated_chunk()


def _ffn(x, modtab, sub, gain, w_up, w_down, layer, *, tm, tf, latent_only=False):
    nf = D_FF // tf
    assert nf >= 2
    n_lat_tiles = ROWS_LAT // tm
    n_tiles = n_lat_tiles if latent_only else ROWS // tm
    return pl.pallas_call(
        functools.partial(_ffn_kernel, tm=tm, n_tiles=n_tiles),
        grid=(n_tiles, nf),
        in_specs=[
            pl.BlockSpec(memory_space=pl.ANY),
            pl.BlockSpec((None, 3, SUBLANES, D_MODEL),
                         lambda i, j: (jnp.where(i < n_lat_tiles, 1, 0), sub, 0, 0)),
            pl.BlockSpec((1, D_MODEL), lambda i, j: (0, 0)),
            pl.BlockSpec((None, D_MODEL, tf), lambda i, j: (layer, 0, j)),
            pl.BlockSpec((None, D_MODEL, tf), lambda i, j: (layer, 0, j + nf)),
            pl.BlockSpec((None, tf, D_MODEL), lambda i, j: (layer, j, 0)),
        ],
        out_specs=pl.BlockSpec((tm, D_MODEL), lambda i, j: (i, 0)),
        out_shape=jax.ShapeDtypeStruct((n_tiles * tm, D_MODEL), F32),
        scratch_shapes=[pltpu.VMEM((tm, D_MODEL), BF16), pltpu.VMEM((tm, D_MODEL), F32),
                        pltpu.SemaphoreType.DMA(())],
        compiler_params=_cparams(("arbitrary", "arbitrary")),
        name="ffn",
    )(x, modtab, gain.reshape(1, D_MODEL), w_up, w_up, w_down)


def _inproj_kernel(x_ref, mod_ref, gain_ref, w_ref, b_ref, uc_ref, us_ref, gt_ref, hn_ref):
    j = pl.program_id(1)

    @pl.when(j == 0)
    def _():
        hn_ref[...] = _norm_mod(x_ref[...], gain_ref, mod_ref)

    def proj():
        return jnp.dot(hn_ref[...], _mxu_operand(w_ref), preferred_element_type=F32) + b_ref[0]

    @pl.when(j == 0)
    def _():
        uc_ref[...] = proj()

    @pl.when(j == 1)
    def _():
        uc_ref[...] = uc_ref[...] * jax.nn.sigmoid(proj())

    @pl.when(j == 2)
    def _():
        us_ref[...] = proj()

    @pl.when(j >= 3)
    def _():
        gt_ref[...] = jax.nn.sigmoid(proj()).astype(BF16)


def _inproj(x, modtab, gain, w_in, b_in, layer, *, tm):
    tn = 1024
    n_lat_tiles = ROWS_LAT // tm
    nblk = N_IN // tn
    n_head = nblk - 2 * D_MODEL // tn
    return pl.pallas_call(
        _inproj_kernel,
        grid=(ROWS // tm, nblk),
        in_specs=[
            pl.BlockSpec((tm, D_MODEL), lambda i, j: (i, 0)),
            pl.BlockSpec((None, 3, SUBLANES, D_MODEL),
                         lambda i, j: (jnp.where(i < n_lat_tiles, 1, 0), 1, 0, 0)),
            pl.BlockSpec((1, D_MODEL), lambda i, j: (0, 0)),
            pl.BlockSpec((None, D_MODEL, tn), lambda i, j: (layer, 0, j)),
            pl.BlockSpec((1, 1, tn), lambda i, j: (j, 0, 0)),
        ],
        out_specs=[
            pl.BlockSpec((tm, tn), lambda i, j: (i, 0)),
            pl.BlockSpec((tm, tn), lambda i, j: (i, 0)),
            pl.BlockSpec((tm, tn), lambda i, j: (i, jnp.maximum(j - n_head, 0))),
        ],
        out_shape=[
            jax.ShapeDtypeStruct((ROWS, CONV_DIM), F32),
            jax.ShapeDtypeStruct((ROWS, SSM_DIM), F32),
            jax.ShapeDtypeStruct((ROWS, 2 * D_MODEL), BF16),
        ],
        scratch_shapes=[pltpu.VMEM((tm, D_MODEL), BF16)],
        compiler_params=_cparams(("arbitrary", "arbitrary")),
        name="mixer_inproj",
    )(x, modtab, gain.reshape(1, D_MODEL), w_in, b_in.reshape(nblk, 1, tn))


CONV_ROW_PAD = 64
CONV_SUB_ROWS = 64
CONV_HALF = CONV_DIM // 2


def _conv_seq_taps(u_ref, w_ref, b_ref, o_ref, pad_ref, pad4_ref, *, n, cb):
    zeros = jnp.zeros((CONV_ROW_PAD, cb), F32)
    pad_ref[0:CONV_ROW_PAD, :] = zeros
    pad_ref[CONV_ROW_PAD + n:, :] = zeros
    pad_ref[CONV_ROW_PAD:CONV_ROW_PAD + n, :] = u_ref[...]
    n_pad = n + 2 * CONV_ROW_PAD
    pad4_ref[0:n_pad - SUBLANES, :] = pad_ref[BATCH:n_pad - SUBLANES + BATCH, :]
    base = CONV_ROW_PAD - CONV_PAD * BATCH
    for c in range(cb // LANES):
        cs = slice(c * LANES, (c + 1) * LANES)
        for r in range(n // CONV_SUB_ROWS):
            r0 = r * CONV_SUB_ROWS
            acc = jnp.broadcast_to(b_ref[:, cs], (CONV_SUB_ROWS, LANES))
            for k in range(CONV_K):
                off = base + r0 + k * BATCH
                if off % SUBLANES == 0:
                    tap = pad_ref[off:off + CONV_SUB_ROWS, cs]
                else:
                    tap = pad4_ref[off - BATCH:off - BATCH + CONV_SUB_ROWS, cs]
                acc = acc + w_ref[k:k + 1, cs] * tap
            o_ref[r0:r0 + CONV_SUB_ROWS, cs] = acc


def _conv_rows_kernel(u_ref, w_ref, b_ref, o_ref, pad_ref, pad4_ref):
    i = pl.program_id(0)
    o_ref[:, CONV_HALF:] = jnp.zeros((ROWS_PER_GRID_ROW, CONV_DIM - CONV_HALF), F32)

    @pl.when(i < GRID_ROWS)
    def _():
        _conv_seq_taps(u_ref, w_ref, b_ref, o_ref, pad_ref, pad4_ref, n=ROWS_PER_GRID_ROW, cb=CONV_HALF)

    @pl.when(i >= GRID_ROWS)
    def _():
        o_ref[:, :CONV_HALF] = jnp.zeros((ROWS_PER_GRID_ROW, CONV_HALF), F32)


def _conv_rows(u, w, b, *, rows_out):
    n = ROWS_PER_GRID_ROW
    n_pad = n + 2 * CONV_ROW_PAD
    return pl.pallas_call(
        _conv_rows_kernel,
        grid=(rows_out // n,),
        in_specs=[
            pl.BlockSpec((n, CONV_HALF), lambda i: (jnp.minimum(i, GRID_ROWS - 1), 0)),
            pl.BlockSpec((CONV_K, CONV_HALF), lambda i: (0, 0)),
            pl.BlockSpec((1, CONV_HALF), lambda i: (0, 0)),
        ],
        out_specs=pl.BlockSpec((n, CONV_DIM), lambda i: (i, 0)),
        out_shape=jax.ShapeDtypeStruct((rows_out, CONV_DIM), F32),
        scratch_shapes=[pltpu.VMEM((n_pad, CONV_HALF), F32), pltpu.VMEM((n_pad, CONV_HALF), F32)],
        compiler_params=_cparams(("arbitrary",)),
        name="conv_rows",
    )(u, w, b.reshape(1, CONV_DIM))


def _conv_ctx_kernel(u_ref, w_ref, b_ref, prev_ref, o_ref, pad_ref, pad4_ref):
    del prev_ref
    _conv_seq_taps(u_ref, w_ref, b_ref, o_ref, pad_ref, pad4_ref, n=ROWS_CTX, cb=LANES)


def _conv_ctx(u, w, b, prev):
    blk0 = ROWS_LAT // ROWS_CTX
    n_pad = ROWS_CTX + 2 * CONV_ROW_PAD
    return pl.pallas_call(
        _conv_ctx_kernel,
        grid=(CONV_DIM // LANES,),
        in_specs=[
            pl.BlockSpec((ROWS_CTX, LANES), lambda c: (blk0, c)),
            pl.BlockSpec((CONV_K, LANES), lambda c: (0, c)),
            pl.BlockSpec((1, LANES), lambda c: (0, c)),
            pl.BlockSpec(memory_space=pl.ANY),
        ],
        out_specs=pl.BlockSpec((ROWS_CTX, LANES), lambda c: (blk0, c)),
        out_shape=jax.ShapeDtypeStruct((ROWS, CONV_DIM), F32),
        scratch_shapes=[pltpu.VMEM((n_pad, LANES), F32), pltpu.VMEM((n_pad, LANES), F32)],
        input_output_aliases={3: 0},
        compiler_params=_cparams(("arbitrary",)),
        name="conv_ctx",
    )(u, w, b.reshape(1, CONV_DIM), prev)


def _conv_vert_kernel(u_ref, w_ref, b_ref, prev_ref, o_ref, pad_ref):
    del prev_ref
    rpg = ROWS_PER_GRID_ROW
    zeros = jnp.zeros((CONV_PAD * rpg, LANES), F32)
    pad_ref[0:CONV_PAD * rpg, :] = zeros
    pad_ref[(CONV_PAD + GRID_ROWS) * rpg:, :] = zeros
    pad_ref[CONV_PAD * rpg:(CONV_PAD + GRID_ROWS) * rpg, :] = u_ref[...]
    bias = jnp.broadcast_to(b_ref[...], (CONV_SUB_ROWS, LANES))

    def body(r, carry):
        for s in range(rpg // CONV_SUB_ROWS):
            acc = bias
            for k in range(CONV_K):
                off = pl.multiple_of((r + k) * rpg + s * CONV_SUB_ROWS, CONV_SUB_ROWS)
                acc = acc + w_ref[k:k + 1, :] * pad_ref[pl.ds(off, CONV_SUB_ROWS), :]
            o0 = pl.multiple_of(r * rpg + s * CONV_SUB_ROWS, CONV_SUB_ROWS)
            o_ref[pl.ds(o0, CONV_SUB_ROWS), :] = acc
        return carry

    lax.fori_loop(0, GRID_ROWS, body, 0)


def _conv_vert(u, w, b, prev):
    half_blk = CONV_HALF // LANES
    return pl.pallas_call(
        _conv_vert_kernel,
        grid=(half_blk,),
        in_specs=[
            pl.BlockSpec((ROWS_LAT, LANES), lambda c: (0, c + half_blk)),
            pl.BlockSpec((CONV_K, LANES), lambda c: (0, c + half_blk)),
            pl.BlockSpec((1, LANES), lambda c: (0, c + half_blk)),
            pl.BlockSpec(memory_space=pl.ANY),
        ],
        out_specs=pl.BlockSpec((ROWS_LAT, LANES), lambda c: (0, c + half_blk)),
        out_shape=jax.ShapeDtypeStruct(prev.shape, F32),
        scratch_shapes=[pltpu.VMEM(((GRID_ROWS + 2 * CONV_PAD) * ROWS_PER_GRID_ROW, LANES), F32)],
        input_output_aliases={3: 0},
        compiler_params=_cparams(("arbitrary",)),
        name="conv_vert",
    )(u, w, b.reshape(1, CONV_DIM), prev)


def _ssm_kernel(uf_ref, ub_ref, b2_ref, c2_ref, lam_ref, yf_ref, yb_ref,
                lhs_ref, s_ref, y2_ref, hst_ref):
    k = pl.program_id(0)

    @pl.when(k == 0)
    def _():
        hst_ref[...] = jnp.zeros_like(hst_ref)

    n_pair = SSM_CHUNK_T // 2
    low = lax.broadcasted_iota(jnp.int32, (SUBLANES, SSM_DIM), 0) < BATCH
    zero = jnp.zeros((SUBLANES, SSM_DIM), F32)
    for m in range(n_pair):
        f = uf_ref[m * SUBLANES:(m + 1) * SUBLANES, :]
        mb = n_pair - 1 - m
        g = ub_ref[mb * SUBLANES:(mb + 1) * SUBLANES, :]
        fr = pltpu.roll(f, BATCH, 0)
        gr = pltpu.roll(g, BATCH, 0)
        fwd = jnp.concatenate([jnp.where(low, f, zero), jnp.where(low, fr, zero)], axis=0).astype(BF16)
        bwd = jnp.concatenate([jnp.where(low, zero, g), jnp.where(low, zero, gr)], axis=0).astype(BF16)
        r0 = m * 2 * SUBLANES
        for j in range(N_SLAB):
            cs = slice(j * LANES, (j + 1) * LANES)
            lhs_ref[r0:r0 + 2 * SUBLANES, 2 * j * LANES:(2 * j + 1) * LANES] = fwd[:, cs]
            lhs_ref[r0:r0 + 2 * SUBLANES, (2 * j + 1) * LANES:(2 * j + 2) * LANES] = bwd[:, cs]

    re = slice(0, LANES)
    im = slice(LANES, 2 * LANES)

    def project(j):
        for gp in range(j * GP_PER_SLAB, (j + 1) * GP_PER_SLAB):
            s_ref[gp] = jnp.dot(lhs_ref[:, 2 * j * LANES:(2 * j + 2) * LANES], b2_ref[gp],
                                preferred_element_type=F32)

    def scan(j):
        for gp in range(j * GP_PER_SLAB, (j + 1) * GP_PER_SLAB):
            lr, li = lam_ref[gp, 0], lam_ref[gp, 1]
            hre, him = hst_ref[gp, 0], hst_ref[gp, 1]
            for s in range(SSM_CHUNK_T):
                rows = slice(s * SUBLANES, (s + 1) * SUBLANES)
                nre = lr * hre - li * him + s_ref[gp, rows, re]
                nim = lr * him + li * hre + s_ref[gp, rows, im]
                s_ref[gp, rows, re] = nre
                s_ref[gp, rows, im] = nim
                hre, him = nre, nim
            hst_ref[gp, 0] = hre
            hst_ref[gp, 1] = him

    def read_out(j):
        acc = None
        for gp in range(j * GP_PER_SLAB, (j + 1) * GP_PER_SLAB):
            p = jnp.dot(s_ref[gp].astype(BF16), c2_ref[gp], preferred_element_type=F32)
            acc = p if acc is None else acc + p
        y2_ref[:, 2 * j * LANES:(2 * j + 2) * LANES] = acc

    project(0)
    for j in range(N_SLAB):
        if j + 1 < N_SLAB:
            project(j + 1)
        scan(j)
        if j >= 1:
            read_out(j - 1)
    read_out(N_SLAB - 1)

    low1 = lax.broadcasted_iota(jnp.int32, (SUBLANES, LANES), 0) < BATCH
    for m in range(n_pair):
        mb = n_pair - 1 - m
        r0 = m * 2 * SUBLANES
        for j in range(N_SLAB):
            cf = slice(2 * j * LANES, (2 * j + 1) * LANES)
            cb = slice((2 * j + 1) * LANES, (2 * j + 2) * LANES)
            cs = slice(j * LANES, (j + 1) * LANES)
            ef = y2_ref[r0:r0 + SUBLANES, cf]
            of = pltpu.roll(y2_ref[r0 + SUBLANES:r0 + 2 * SUBLANES, cf], BATCH, 0)
            yf_ref[m * SUBLANES:(m + 1) * SUBLANES, cs] = jnp.where(low1, ef, of)
            eb = y2_ref[r0:r0 + SUBLANES, cb]
            ob = pltpu.roll(y2_ref[r0 + SUBLANES:r0 + 2 * SUBLANES, cb], BATCH, 0)
            yb_ref[mb * SUBLANES:(mb + 1) * SUBLANES, cs] = jnp.where(low1, ob, eb)


def _ssm(us, b2, c2, lamtab):
    n_lat = N_CHUNK - N_CHUNK_CTX

    def fwd_blk(k):
        return jnp.where(k < N_CHUNK_CTX, n_lat + k, k - N_CHUNK_CTX)

    def bwd_blk(k):
        return N_CHUNK - 1 - k

    blk = (SSM_CHUNK_ROWS, SSM_DIM)
    return pl.pallas_call(
        _ssm_kernel,
        grid=(N_CHUNK,),
        in_specs=[
            pl.BlockSpec(blk, lambda k: (fwd_blk(k), 0)),
            pl.BlockSpec(blk, lambda k: (bwd_blk(k), 0)),
            _const_spec(b2.shape),
            _const_spec(c2.shape),
            _const_spec(lamtab.shape),
        ],
        out_specs=[
            pl.BlockSpec(blk, lambda k: (fwd_blk(k), 0)),
            pl.BlockSpec(blk, lambda k: (bwd_blk(k), 0)),
        ],
        out_shape=[jax.ShapeDtypeStruct((ROWS, SSM_DIM), F32)] * 2,
        scratch_shapes=[
            pltpu.VMEM((SSM_STEP_ROWS, 2 * SSM_DIM), BF16),
            pltpu.VMEM((N_GP, SSM_STEP_ROWS, 2 * LANES), F32),
            pltpu.VMEM((SSM_STEP_ROWS, 2 * SSM_DIM), F32),
            pltpu.VMEM((N_GP, 2, SUBLANES, LANES), F32),
        ],
        compiler_params=_cparams(("arbitrary",)),
        name="s5_scan",
    )(us, us, b2, c2, lamtab)


def _ssm_tables(a_re, a_im, log_dt, b_re, b_im, c_re, c_im):
    lam = lax.complex(a_re.astype(F32), a_im.astype(F32))
    dt = jnp.exp(log_dt.astype(F32))[..., None]
    lam_bar = jnp.exp(lam * dt)
    bmat_c = lax.complex(b_re.astype(F32), b_im.astype(F32))
    b_bar = ((lam_bar - 1) / lam)[..., None] * bmat_c

    def pair_rows(v):
        v = v.reshape(2, N_GP, 2, SSM_GROUP, SSM_STATE).transpose(1, 0, 2, 3, 4)
        z = jnp.zeros_like(v[:, :, 0])
        q0 = jnp.concatenate([v[:, :, 0], z], axis=-1)
        q1 = jnp.concatenate([z, v[:, :, 1]], axis=-1)
        return jnp.stack([q0, q1], axis=2)

    slab_pos = (jnp.arange(N_GP) % GP_PER_SLAB)[:, None] == jnp.arange(GP_PER_SLAB)[None, :]
    place = slab_pos.astype(F32)[:, None, :, None, None, None]

    def place_rows(re_v, im_v):
        t = jnp.concatenate([pair_rows(re_v), pair_rows(im_v)], axis=-1)
        t = t[:, :, None] * place
        return t.reshape(N_GP, 2 * LANES, 2 * LANES)

    bt = jnp.swapaxes(b_bar, -1, -2)
    b2 = place_rows(jnp.real(bt), jnp.imag(bt))
    c2 = jnp.swapaxes(place_rows(c_re.astype(F32), -c_im.astype(F32)), 1, 2)
    lam_ri = jnp.stack([jnp.real(lam_bar), jnp.imag(lam_bar)], axis=0)
    lam_ri = lam_ri.reshape(2, 2, N_GP, 1, LANES).transpose(2, 0, 1, 3, 4)
    lamtab = jnp.broadcast_to(lam_ri, (N_GP, 2, 2, BATCH, LANES)).reshape(N_GP, 2, SUBLANES, LANES)
    return b2.astype(BF16), c2.astype(BF16), lamtab


MIX_COLS = 512


def _mix_kernel(cv_ref, yf_ref, yb_ref, us_ref, gt_ref, lng_ref, lnb_ref, sd_ref,
                cwo_ref, wglu_ref, bglu_ref, o_ref):
    cv = cv_ref[...]
    mu = jnp.mean(cv, axis=-1, keepdims=True)
    var = jnp.mean(jnp.square(cv - mu), axis=-1, keepdims=True)
    ln = (cv - mu) * lax.rsqrt(var + EPS) * lng_ref[...] + lnb_ref[...]
    s_conv = jax.nn.silu(ln).astype(BF16)
    y = yf_ref[...] + yb_ref[...] + sd_ref[...] * us_ref[...]
    s_ssm = jax.nn.gelu(y).astype(BF16)
    for c in range(D_MODEL // MIX_COLS):
        lo = c * MIX_COLS
        ca = slice(lo, lo + MIX_COLS)
        cg = slice(D_MODEL + lo, D_MODEL + lo + MIX_COLS)
        y_conv = jnp.dot(s_conv, cwo_ref[:, ca], preferred_element_type=F32)
        za = jnp.dot(s_ssm, wglu_ref[:, ca], preferred_element_type=F32) + bglu_ref[:, ca]
        zg = jnp.dot(s_ssm, wglu_ref[:, cg], preferred_element_type=F32) + bglu_ref[:, cg]
        y_ssm = za * jax.nn.sigmoid(zg)
        mixed = gt_ref[:, ca].astype(F32) * y_conv + gt_ref[:, cg].astype(F32) * y_ssm
        o_ref[:, ca] = mixed.astype(BF16)


def _mix(cv, yf, yb, us, gt, ln_g, ln_b, ssm_d, cwo, wglu, bglu, layer, *, tm, latent_only):
    n_tiles = (ROWS_LAT if latent_only else ROWS) // tm
    row = lambda w: pl.BlockSpec((tm, w), lambda i: (i, 0))
    vec = lambda w: pl.BlockSpec((1, w), lambda i: (0, 0), pipeline_mode=pl.Buffered(1))
    wgt = lambda r, c: pl.BlockSpec((None, r, c), lambda i: (layer, 0, 0), pipeline_mode=pl.Buffered(1))
    return pl.pallas_call(
        _mix_kernel,
        grid=(n_tiles,),
        in_specs=[
            row(CONV_DIM), row(SSM_DIM), row(SSM_DIM), row(SSM_DIM), row(2 * D_MODEL),
            vec(CONV_DIM), vec(CONV_DIM), vec(SSM_DIM),
            wgt(CONV_DIM, D_MODEL), wgt(SSM_DIM, 2 * D_MODEL), vec(2 * D_MODEL),
        ],
        out_specs=row(D_MODEL),
        out_shape=jax.ShapeDtypeStruct((n_tiles * tm, D_MODEL), BF16),
        compiler_params=_cparams(("arbitrary",)),
        name="mixer_mix",
    )(cv, yf, yb, us, gt, ln_g.reshape(1, -1), ln_b.reshape(1, -1), ssm_d.reshape(1, -1),
      cwo, wglu, bglu.reshape(1, -1))


def _outproj_kernel(x_ref, mix_ref, mod_ref, w_ref, o_ref):
    y = jnp.dot(mix_ref[...], w_ref[...], preferred_element_type=F32)
    o_ref[...] = x_ref[...] + _rows8(y, mod_ref[2], jnp.multiply)


def _outproj(x, mix, modtab, w_out, layer, *, tm, latent_only):
    n_lat_tiles = ROWS_LAT // tm
    n_tiles = n_lat_tiles if latent_only else ROWS // tm
    return pl.pallas_call(
        _outproj_kernel,
        grid=(n_tiles,),
        in_specs=[
            pl.BlockSpec((tm, D_MODEL), lambda i: (i, 0)),
            pl.BlockSpec((tm, D_MODEL), lambda i: (i, 0)),
            pl.BlockSpec((None, 3, SUBLANES, D_MODEL),
                         lambda i: (jnp.where(i < n_lat_tiles, 1, 0), 1, 0, 0)),
            pl.BlockSpec((None, D_MODEL, D_MODEL), lambda i: (layer, 0, 0)),
        ],
        out_specs=pl.BlockSpec((tm, D_MODEL), lambda i: (i, 0)),
        out_shape=jax.ShapeDtypeStruct((ROWS, D_MODEL), F32),
        input_output_aliases={0: 0},
        compiler_params=_cparams(("arbitrary",)),
        name="mixer_outproj",
    )(x, mix, modtab, w_out)


def kernel(x, c, ctx, c_ctx, ada_w, ada_b, norm_ffn1, ffn1_w_up, ffn1_w_down, norm_mix, w_in, b_in, conv_dw, conv_db, conv_ln_g, conv_ln_b, conv_w_out, ssm_a_re, ssm_a_im, ssm_log_dt, ssm_b_re, ssm_b_im, ssm_c_re, ssm_c_im, ssm_d, ssm_w_glu, ssm_b_glu, w_out, norm_ffn2, ffn2_w_up, ffn2_w_down, norm_final):
    xs = _to_rows(x, ctx)

    c8 = jnp.zeros((SUBLANES, D_MODEL), F32).at[:BATCH].set(c).at[BATCH].set(c_ctx)
    mod = _ada(c8, ada_w, ada_b).reshape(DEPTH, SUBLANES, N_MOD, D_MODEL)
    lat_rows = jnp.arange(SUBLANES) % BATCH
    mod_lat = mod[:, lat_rows].transpose(0, 2, 1, 3)
    mod_ctx = jnp.broadcast_to(mod[:, BATCH][:, :, None, :], mod_lat.shape)
    modtab = jnp.stack([mod_ctx, mod_lat], axis=1)

    win, cwo = w_in.astype(BF16), conv_w_out.astype(BF16)
    wglu, wout = ssm_w_glu.astype(BF16), w_out.astype(BF16)

    for l in range(DEPTH):
        last = l == DEPTH - 1
        mt = modtab[l]
        xs = _ffn(xs, mt, 0, norm_ffn1[l], ffn1_w_up, ffn1_w_down, l, tm=1024, tf=512)
        uc, us, gt = _inproj(xs, mt, norm_mix[l], win, b_in[l], l, tm=1024)

        cv = _conv_rows(uc, conv_dw[l], conv_db[l], rows_out=ROWS_LAT if last else ROWS)
        cv = _conv_vert(uc, conv_dw[l], conv_db[l], cv)
        if not last:
            cv = _conv_ctx(uc, conv_dw[l], conv_db[l], cv)

        b2, c2, lamtab = _ssm_tables(ssm_a_re[l], ssm_a_im[l], ssm_log_dt[l],
                                     ssm_b_re[l], ssm_b_im[l], ssm_c_re[l], ssm_c_im[l])
        yf, yb = _ssm(us, b2, c2, lamtab)

        mix = _mix(cv, yf, yb, us, gt, conv_ln_g[l], conv_ln_b[l], ssm_d[l], cwo, wglu, ssm_b_glu[l], l,
                   tm=512, latent_only=last)
        xs = _outproj(xs, mix, mt, wout, l, tm=512, latent_only=last)
        xs = _ffn(xs, mt, 2, norm_ffn2[l], ffn2_w_up, ffn2_w_down, l, tm=1024, tf=512, latent_only=last)

    return _final_norm(xs, norm_final)
```

```python
import functools

import jax
import jax.numpy as jnp
from jax import lax
from jax.experimental import pallas as pl
from jax.experimental.pallas import tpu as pltpu

F32 = jnp.float32
BF16 = jnp.bfloat16

D_MODEL = 2048
BATCH = 4
SEQ = 2048
CTX_LEN = 256
DEPTH = 2
GRID_W = 64
GRID_ROWS = SEQ // GRID_W
D_FF = 5632
CONV_DIM = 1024
CONV_K = 31
CONV_PAD = CONV_K // 2
SSM_DIM = 1024
SSM_GROUP = 16
SSM_GROUPS = 64
SSM_STATE = 64
N_MOD = 9
N_IN = 2 * CONV_DIM + SSM_DIM + 2 * D_MODEL
FFN_RES = 0.5
EPS = 1e-6

LANES = 128
SUBLANES = 8
ROWS_CTX = CTX_LEN * BATCH
ROWS_LAT = SEQ * BATCH
ROWS = ROWS_CTX + ROWS_LAT
ROWS_PER_GRID_ROW = GRID_W * BATCH

VMEM_LIMIT = 60 * 1024 * 1024

GROUPS_PER_SLAB = LANES // SSM_GROUP
N_GP = SSM_GROUPS // 2
GP_PER_SLAB = GROUPS_PER_SLAB // 2
N_SLAB = SSM_DIM // LANES
SSM_CHUNK_T = 64
SSM_CHUNK_ROWS = SSM_CHUNK_T * BATCH
SSM_STEP_ROWS = 2 * SSM_CHUNK_ROWS
N_CHUNK_CTX = CTX_LEN // SSM_CHUNK_T
N_CHUNK = (CTX_LEN + SEQ) // SSM_CHUNK_T
assert 2 * BATCH == SUBLANES


def _cparams(sem):
    return pltpu.CompilerParams(dimension_semantics=sem, vmem_limit_bytes=VMEM_LIMIT)


def _const_spec(shape):
    nd = len(shape)
    return pl.BlockSpec(shape, lambda *_: (0,) * nd, pipeline_mode=pl.Buffered(1))


def _mxu_operand(w_ref):
    w = w_ref[...]
    return w if w.dtype == BF16 else w.astype(BF16)


def _rows8(x, m, op):
    n, d = x.shape
    x3 = x.reshape(n // SUBLANES, SUBLANES, d)
    return op(x3, m[None]).reshape(n, d)


def _row_tile_copy(x_hbm, xbuf_ref, sem, tile, tm):
    rows = pl.ds(pl.multiple_of(tile * tm, tm), tm)
    return pltpu.make_async_copy(x_hbm.at[rows, :], xbuf_ref, sem)


def _interleave_batch(src_ref, o_ref, slab_ref, tq):
    for c in range(D_MODEL // LANES):
        cs = slice(c * LANES, (c + 1) * LANES)
        for b in range(BATCH):
            slab_ref[c, pl.ds(b, tq, stride=BATCH), :] = src_ref[b, :, cs]
        o_ref[:, cs] = slab_ref[c]


def _to_rows_kernel(x_ref, c_ref, o_ref, slab_ref, *, n_lat_tiles, tq):
    i = pl.program_id(0)

    @pl.when(i < n_lat_tiles)
    def _():
        _interleave_batch(x_ref, o_ref, slab_ref, tq)

    @pl.when(i >= n_lat_tiles)
    def _():
        _interleave_batch(c_ref, o_ref, slab_ref, tq)


def _to_rows(x, ctx):
    tq = 128
    n_lat_tiles = SEQ // tq
    n_tiles = (SEQ + CTX_LEN) // tq
    return pl.pallas_call(
        functools.partial(_to_rows_kernel, n_lat_tiles=n_lat_tiles, tq=tq),
        grid=(n_tiles,),
        in_specs=[
            pl.BlockSpec((BATCH, tq, D_MODEL), lambda i: (0, jnp.minimum(i, n_lat_tiles - 1), 0)),
            pl.BlockSpec((BATCH, tq, D_MODEL), lambda i: (0, jnp.maximum(i - n_lat_tiles, 0), 0)),
        ],
        out_specs=pl.BlockSpec((tq * BATCH, D_MODEL), lambda i: (i, 0)),
        out_shape=jax.ShapeDtypeStruct((ROWS, D_MODEL), F32),
        scratch_shapes=[pltpu.VMEM((D_MODEL // LANES, tq * BATCH, LANES), F32)],
        compiler_params=_cparams(("arbitrary",)),
        name="to_rows",
    )(x, ctx)


def _final_norm_kernel(x_ref, gain_ref, o_ref, slab_ref, *, tq):
    x = x_ref[...]
    xn = x * lax.rsqrt(jnp.mean(x * x, axis=-1, keepdims=True) + EPS) * gain_ref[...]
    for c in range(D_MODEL // LANES):
        cs = slice(c * LANES, (c + 1) * LANES)
        slab_ref[c] = xn[:, cs]
        for b in range(BATCH):
            o_ref[b, :, cs] = slab_ref[c, pl.ds(b, tq, stride=BATCH), :]


def _final_norm(xs, gain):
    tq = 128
    return pl.pallas_call(
        functools.partial(_final_norm_kernel, tq=tq),
        grid=(SEQ // tq,),
        in_specs=[pl.BlockSpec((tq * BATCH, D_MODEL), lambda i: (i, 0)),
                  pl.BlockSpec((1, D_MODEL), lambda i: (0, 0))],
        out_specs=pl.BlockSpec((BATCH, tq, D_MODEL), lambda i: (0, i, 0)),
        out_shape=jax.ShapeDtypeStruct((BATCH, SEQ, D_MODEL), F32),
        scratch_shapes=[pltpu.VMEM((D_MODEL // LANES, tq * BATCH, LANES), F32)],
        compiler_params=_cparams(("arbitrary",)),
        name="final_norm",
    )(xs, gain.reshape(1, D_MODEL))


def _ada_kernel(c_ref, w_ref, b_ref, o_ref):
    c = c_ref[...]
    s = jax.nn.silu(c).astype(BF16)
    o_ref[0] = jnp.dot(s, w_ref[0].astype(BF16), preferred_element_type=F32) + b_ref[0]


def _ada(c8, ada_w, ada_b):
    tn = 1024
    n = N_MOD * D_MODEL
    return pl.pallas_call(
        _ada_kernel,
        grid=(DEPTH, n // tn),
        in_specs=[
            pl.BlockSpec((SUBLANES, D_MODEL), lambda l, j: (0, 0)),
            pl.BlockSpec((1, D_MODEL, tn), lambda l, j: (l, 0, j)),
            pl.BlockSpec((1, 1, tn), lambda l, j: (l, 0, j)),
        ],
        out_specs=pl.BlockSpec((1, SUBLANES, tn), lambda l, j: (l, 0, j)),
        out_shape=jax.ShapeDtypeStruct((DEPTH, SUBLANES, n), F32),
        compiler_params=_cparams(("arbitrary", "arbitrary")),
        name="ada_mod",
    )(c8, ada_w, ada_b.reshape(DEPTH, 1, n))


def _norm_mod(x, gain_ref, mod_ref):
    y = x * lax.rsqrt(jnp.mean(x * x, axis=-1, keepdims=True) + EPS) * gain_ref[...]
    h = _rows8(y, 1.0 + mod_ref[1], jnp.multiply)
    return _rows8(h, mod_ref[0], jnp.add).astype(BF16)


def _ffn_chunk(hn_ref, wa_ref, wg_ref, wd_ref):
    hn = hn_ref[...]
    a = jnp.dot(hn, _mxu_operand(wa_ref), preferred_element_type=F32)
    g = jnp.dot(hn, _mxu_operand(wg_ref), preferred_element_type=F32)
    act = (jax.nn.silu(g) * a).astype(BF16)
    return jnp.dot(act, _mxu_operand(wd_ref), preferred_element_type=F32)


def _ffn_kernel(x_hbm, mod_ref, gain_ref, wa_ref, wg_ref, wd_ref, o_ref, hn_ref, xbuf_ref, sem,
                *, tm, n_tiles):
    i = pl.program_id(0)
    j = pl.program_id(1)

    x_copy = functools.partial(_row_tile_copy, x_hbm, xbuf_ref, sem, tm=tm)

    def gated_chunk():
        return _rows8(_ffn_chunk(hn_ref, wa_ref, wg_ref, wd_ref), mod_ref[2] * FFN_RES, jnp.multiply)

    @pl.when(j == 0)
    def _():
        @pl.when(i == 0)
        def _():
            x_copy(0).start()

        x_copy(i).wait()
        x = xbuf_ref[...]
        hn_ref[...] = _norm_mod(x, gain_ref, mod_ref)
        o_ref[...] = x + gated_chunk()

    @pl.when(j == 1)
    def _():
        @pl.when(i + 1 < n_tiles)
        def _():
            x_copy(i + 1).start()

        o_ref[...] += gated_chunk()

    @pl.when(j > 1)
    def _():
        o_ref[...] += gated_chunk()


def _ffn(x, modtab, sub, gain, w_up, w_down, layer, *, tm, tf, latent_only=False):
    nf = D_FF // tf
    assert nf >= 2
    n_lat_tiles = ROWS_LAT // tm
    n_tiles = n_lat_tiles if latent_only else ROWS // tm
    return pl.pallas_call(
        functools.partial(_ffn_kernel, tm=tm, n_tiles=n_tiles),
        grid=(n_tiles, nf),
        in_specs=[
            pl.BlockSpec(memory_space=pl.ANY),
            pl.BlockSpec((None, 3, SUBLANES, D_MODEL),
                         lambda i, j: (jnp.where(i < n_lat_tiles, 1, 0), sub, 0, 0)),
            pl.BlockSpec((1, D_MODEL), lambda i, j: (0, 0)),
            pl.BlockSpec((None, D_MODEL, tf), lambda i, j: (layer, 0, j)),
            pl.BlockSpec((None, D_MODEL, tf), lambda i, j: (layer, 0, j + nf)),
            pl.BlockSpec((None, tf, D_MODEL), lambda i, j: (layer, j, 0)),
        ],
        out_specs=pl.BlockSpec((tm, D_MODEL), lambda i, j: (i, 0)),
        out_shape=jax.ShapeDtypeStruct((n_tiles * tm, D_MODEL), F32),
        scratch_shapes=[pltpu.VMEM((tm, D_MODEL), BF16), pltpu.VMEM((tm, D_MODEL), F32),
                        pltpu.SemaphoreType.DMA(())],
        compiler_params=_cparams(("arbitrary", "arbitrary")),
        name="ffn",
    )(x, modtab, gain.reshape(1, D_MODEL), w_up, w_up, w_down)


def _inproj_kernel(x_hbm, mod_ref, gain_ref, w_ref, b_ref, uc_ref, us_ref, gt_ref, hn_ref, xbuf_ref, sem,
                   *, tm, n_tiles):
    i = pl.program_id(0)
    j = pl.program_id(1)
    x_copy = functools.partial(_row_tile_copy, x_hbm, xbuf_ref, sem, tm=tm)

    def proj():
        return jnp.dot(hn_ref[...], _mxu_operand(w_ref), preferred_element_type=F32) + b_ref[0]

    @pl.when(j == 0)
    def _():
        @pl.when(i == 0)
        def _():
            x_copy(0).start()

        x_copy(i).wait()
        hn_ref[...] = _norm_mod(xbuf_ref[...], gain_ref, mod_ref)
        uc_ref[...] = proj()

    @pl.when(j == 1)
    def _():
        @pl.when(i + 1 < n_tiles)
        def _():
            x_copy(i + 1).start()

        uc_ref[...] = uc_ref[...] * jax.nn.sigmoid(proj())

    @pl.when(j == 2)
    def _():
        us_ref[...] = proj()

    @pl.when(j >= 3)
    def _():
        gt_ref[...] = jax.nn.sigmoid(proj()).astype(BF16)


def _inproj(x, modtab, gain, w_in, b_in, layer, *, tm):
    tn = 1024
    n_lat_tiles = ROWS_LAT // tm
    nblk = N_IN // tn
    n_head = nblk - 2 * D_MODEL // tn
    n_tiles = ROWS // tm
    return pl.pallas_call(
        functools.partial(_inproj_kernel, tm=tm, n_tiles=n_tiles),
        grid=(n_tiles, nblk),
        in_specs=[
            pl.BlockSpec(memory_space=pl.ANY),
            pl.BlockSpec((None, 3, SUBLANES, D_MODEL),
                         lambda i, j: (jnp.where(i < n_lat_tiles, 1, 0), 1, 0, 0)),
            pl.BlockSpec((1, D_MODEL), lambda i, j: (0, 0)),
            pl.BlockSpec((None, D_MODEL, tn), lambda i, j: (layer, 0, j)),
            pl.BlockSpec((1, 1, tn), lambda i, j: (j, 0, 0)),
        ],
        out_specs=[
            pl.BlockSpec((tm, tn), lambda i, j: (i, 0)),
            pl.BlockSpec((tm, tn), lambda i, j: (i, 0)),
            pl.BlockSpec((tm, tn), lambda i, j: (i, jnp.maximum(j - n_head, 0))),
        ],
        out_shape=[
            jax.ShapeDtypeStruct((ROWS, CONV_DIM), F32),
            jax.ShapeDtypeStruct((ROWS, SSM_DIM), F32),
            jax.ShapeDtypeStruct((ROWS, 2 * D_MODEL), BF16),
        ],
        scratch_shapes=[pltpu.VMEM((tm, D_MODEL), BF16), pltpu.VMEM((tm, D_MODEL), F32),
                        pltpu.SemaphoreType.DMA(())],
        compiler_params=_cparams(("arbitrary", "arbitrary")),
        name="mixer_inproj",
    )(x, modtab, gain.reshape(1, D_MODEL), w_in, b_in.reshape(nblk, 1, tn))


CONV_ROW_PAD = 64
CONV_SUB_ROWS = 64
CONV_HALF = CONV_DIM // 2


def _conv_seq_taps(u_ref, w_ref, b_ref, o_ref, pad_ref, pad4_ref, *, n, cb):
    zeros = jnp.zeros((CONV_ROW_PAD, cb), F32)
    pad_ref[0:CONV_ROW_PAD, :] = zeros
    pad_ref[CONV_ROW_PAD + n:, :] = zeros
    pad_ref[CONV_ROW_PAD:CONV_ROW_PAD + n, :] = u_ref[...]
    n_pad = n + 2 * CONV_ROW_PAD
    pad4_ref[0:n_pad - SUBLANES, :] = pad_ref[BATCH:n_pad - SUBLANES + BATCH, :]
    base = CONV_ROW_PAD - CONV_PAD * BATCH
    for c in range(cb // LANES):
        cs = slice(c * LANES, (c + 1) * LANES)
        for r in range(n // CONV_SUB_ROWS):
            r0 = r * CONV_SUB_ROWS
            acc = jnp.broadcast_to(b_ref[:, cs], (CONV_SUB_ROWS, LANES))
            for k in range(CONV_K):
                off = base + r0 + k * BATCH
                if off % SUBLANES == 0:
                    tap = pad_ref[off:off + CONV_SUB_ROWS, cs]
                else:
                    tap = pad4_ref[off - BATCH:off - BATCH + CONV_SUB_ROWS, cs]
                acc = acc + w_ref[k:k + 1, cs] * tap
            o_ref[r0:r0 + CONV_SUB_ROWS, cs] = acc


def _conv_rows_kernel(u_ref, w_ref, b_ref, o_ref, pad_ref, pad4_ref):
    i = pl.program_id(0)
    o_ref[:, CONV_HALF:] = jnp.zeros((ROWS_PER_GRID_ROW, CONV_DIM - CONV_HALF), F32)

    @pl.when(i < GRID_ROWS)
    def _():
        _conv_seq_taps(u_ref, w_ref, b_ref, o_ref, pad_ref, pad4_ref, n=ROWS_PER_GRID_ROW, cb=CONV_HALF)

    @pl.when(i >= GRID_ROWS)
    def _():
        o_ref[:, :CONV_HALF] = jnp.zeros((ROWS_PER_GRID_ROW, CONV_HALF), F32)


def _conv_rows(u, w, b, *, rows_out):
    n = ROWS_PER_GRID_ROW
    n_pad = n + 2 * CONV_ROW_PAD
    return pl.pallas_call(
        _conv_rows_kernel,
        grid=(rows_out // n,),
        in_specs=[
            pl.BlockSpec((n, CONV_HALF), lambda i: (jnp.minimum(i, GRID_ROWS - 1), 0)),
            pl.BlockSpec((CONV_K, CONV_HALF), lambda i: (0, 0)),
            pl.BlockSpec((1, CONV_HALF), lambda i: (0, 0)),
        ],
        out_specs=pl.BlockSpec((n, CONV_DIM), lambda i: (i, 0)),
        out_shape=jax.ShapeDtypeStruct((rows_out, CONV_DIM), F32),
        scratch_shapes=[pltpu.VMEM((n_pad, CONV_HALF), F32), pltpu.VMEM((n_pad, CONV_HALF), F32)],
        compiler_params=_cparams(("arbitrary",)),
        name="conv_rows",
    )(u, w, b.reshape(1, CONV_DIM))


def _conv_ctx_kernel(u_ref, w_ref, b_ref, prev_ref, o_ref, pad_ref, pad4_ref):
    del prev_ref
    _conv_seq_taps(u_ref, w_ref, b_ref, o_ref, pad_ref, pad4_ref, n=ROWS_CTX, cb=LANES)


def _conv_ctx(u, w, b, prev):
    blk0 = ROWS_LAT // ROWS_CTX
    n_pad = ROWS_CTX + 2 * CONV_ROW_PAD
    return pl.pallas_call(
        _conv_ctx_kernel,
        grid=(CONV_DIM // LANES,),
        in_specs=[
            pl.BlockSpec((ROWS_CTX, LANES), lambda c: (blk0, c)),
            pl.BlockSpec((CONV_K, LANES), lambda c: (0, c)),
            pl.BlockSpec((1, LANES), lambda c: (0, c)),
            pl.BlockSpec(memory_space=pl.ANY),
        ],
        out_specs=pl.BlockSpec((ROWS_CTX, LANES), lambda c: (blk0, c)),
        out_shape=jax.ShapeDtypeStruct((ROWS, CONV_DIM), F32),
        scratch_shapes=[pltpu.VMEM((n_pad, LANES), F32), pltpu.VMEM((n_pad, LANES), F32)],
        input_output_aliases={3: 0},
        compiler_params=_cparams(("arbitrary",)),
        name="conv_ctx",
    )(u, w, b.reshape(1, CONV_DIM), prev)


def _conv_vert_kernel(u_ref, w_ref, b_ref, prev_ref, o_ref, pad_ref):
    del prev_ref
    rpg = ROWS_PER_GRID_ROW
    zeros = jnp.zeros((CONV_PAD * rpg, LANES), F32)
    pad_ref[0:CONV_PAD * rpg, :] = zeros
    pad_ref[(CONV_PAD + GRID_ROWS) * rpg:, :] = zeros
    pad_ref[CONV_PAD * rpg:(CONV_PAD + GRID_ROWS) * rpg, :] = u_ref[...]
    bias = jnp.broadcast_to(b_ref[...], (CONV_SUB_ROWS, LANES))

    def body(r, carry):
        for s in range(rpg // CONV_SUB_ROWS):
            acc = bias
            for k in range(CONV_K):
                off = pl.multiple_of((r + k) * rpg + s * CONV_SUB_ROWS, CONV_SUB_ROWS)
                acc = acc + w_ref[k:k + 1, :] * pad_ref[pl.ds(off, CONV_SUB_ROWS), :]
            o0 = pl.multiple_of(r * rpg + s * CONV_SUB_ROWS, CONV_SUB_ROWS)
            o_ref[pl.ds(o0, CONV_SUB_ROWS), :] = acc
        return carry

    lax.fori_loop(0, GRID_ROWS, body, 0)


def _conv_vert(u, w, b, prev):
    half_blk = CONV_HALF // LANES
    return pl.pallas_call(
        _conv_vert_kernel,
        grid=(half_blk,),
        in_specs=[
            pl.BlockSpec((ROWS_LAT, LANES), lambda c: (0, c + half_blk)),
            pl.BlockSpec((CONV_K, LANES), lambda c: (0, c + half_blk)),
            pl.BlockSpec((1, LANES), lambda c: (0, c + half_blk)),
            pl.BlockSpec(memory_space=pl.ANY),
        ],
        out_specs=pl.BlockSpec((ROWS_LAT, LANES), lambda c: (0, c + half_blk)),
        out_shape=jax.ShapeDtypeStruct(prev.shape, F32),
        scratch_shapes=[pltpu.VMEM(((GRID_ROWS + 2 * CONV_PAD) * ROWS_PER_GRID_ROW, LANES), F32)],
        input_output_aliases={3: 0},
        compiler_params=_cparams(("arbitrary",)),
        name="conv_vert",
    )(u, w, b.reshape(1, CONV_DIM), prev)


def _ssm_kernel(uf_ref, ub_ref, b2_ref, c2_ref, lam_ref, yf_ref, yb_ref,
                lhs_ref, s_ref, y2_ref, hst_ref):
    k = pl.program_id(0)

    @pl.when(k == 0)
    def _():
        hst_ref[...] = jnp.zeros_like(hst_ref)

    n_pair = SSM_CHUNK_T // 2
    low = lax.broadcasted_iota(jnp.int32, (SUBLANES, SSM_DIM), 0) < BATCH
    zero = jnp.zeros((SUBLANES, SSM_DIM), F32)
    for m in range(n_pair):
        f = uf_ref[m * SUBLANES:(m + 1) * SUBLANES, :]
        mb = n_pair - 1 - m
        g = ub_ref[mb * SUBLANES:(mb + 1) * SUBLANES, :]
        fr = pltpu.roll(f, BATCH, 0)
        gr = pltpu.roll(g, BATCH, 0)
        fwd = jnp.concatenate([jnp.where(low, f, zero), jnp.where(low, fr, zero)], axis=0).astype(BF16)
        bwd = jnp.concatenate([jnp.where(low, zero, g), jnp.where(low, zero, gr)], axis=0).astype(BF16)
        r0 = m * 2 * SUBLANES
        for j in range(N_SLAB):
            cs = slice(j * LANES, (j + 1) * LANES)
            lhs_ref[r0:r0 + 2 * SUBLANES, 2 * j * LANES:(2 * j + 1) * LANES] = fwd[:, cs]
            lhs_ref[r0:r0 + 2 * SUBLANES, (2 * j + 1) * LANES:(2 * j + 2) * LANES] = bwd[:, cs]

    re = slice(0, LANES)
    im = slice(LANES, 2 * LANES)

    def project(j):
        for gp in range(j * GP_PER_SLAB, (j + 1) * GP_PER_SLAB):
            s_ref[gp] = jnp.dot(lhs_ref[:, 2 * j * LANES:(2 * j + 2) * LANES], b2_ref[gp],
                                preferred_element_type=F32)

    def scan(j):
        for gp in range(j * GP_PER_SLAB, (j + 1) * GP_PER_SLAB):
            lr, li = lam_ref[gp, 0], lam_ref[gp, 1]
            hre, him = hst_ref[gp, 0], hst_ref[gp, 1]
            for s in range(SSM_CHUNK_T):
                rows = slice(s * SUBLANES, (s + 1) * SUBLANES)
                nre = lr * hre - li * him + s_ref[gp, rows, re]
                nim = lr * him + li * hre + s_ref[gp, rows, im]
                s_ref[gp, rows, re] = nre
                s_ref[gp, rows, im] = nim
                hre, him = nre, nim
            hst_ref[gp, 0] = hre
            hst_ref[gp, 1] = him

    def read_out(j):
        acc = None
        for gp in range(j * GP_PER_SLAB, (j + 1) * GP_PER_SLAB):
            p = jnp.dot(s_ref[gp].astype(BF16), c2_ref[gp], preferred_element_type=F32)
            acc = p if acc is None else acc + p
        y2_ref[:, 2 * j * LANES:(2 * j + 2) * LANES] = acc

    project(0)
    for j in range(N_SLAB):
        if j + 1 < N_SLAB:
            project(j + 1)
        scan(j)
        if j >= 1:
            read_out(j - 1)
    read_out(N_SLAB - 1)

    low1 = lax.broadcasted_iota(jnp.int32, (SUBLANES, LANES), 0) < BATCH
    for m in range(n_pair):
        mb = n_pair - 1 - m
        r0 = m * 2 * SUBLANES
        for j in range(N_SLAB):
            cf = slice(2 * j * LANES, (2 * j + 1) * LANES)
            cb = slice((2 * j + 1) * LANES, (2 * j + 2) * LANES)
            cs = slice(j * LANES, (j + 1) * LANES)
            ef = y2_ref[r0:r0 + SUBLANES, cf]
            of = pltpu.roll(y2_ref[r0 + SUBLANES:r0 + 2 * SUBLANES, cf], BATCH, 0)
            yf_ref[m * SUBLANES:(m + 1) * SUBLANES, cs] = jnp.where(low1, ef, of)
            eb = y2_ref[r0:r0 + SUBLANES, cb]
            ob = pltpu.roll(y2_ref[r0 + SUBLANES:r0 + 2 * SUBLANES, cb], BATCH, 0)
            yb_ref[mb * SUBLANES:(mb + 1) * SUBLANES, cs] = jnp.where(low1, ob, eb)


def _ssm(us, b2, c2, lamtab):
    n_lat = N_CHUNK - N_CHUNK_CTX

    def fwd_blk(k):
        return jnp.where(k < N_CHUNK_CTX, n_lat + k, k - N_CHUNK_CTX)

    def bwd_blk(k):
        return N_CHUNK - 1 - k

    blk = (SSM_CHUNK_ROWS, SSM_DIM)
    return pl.pallas_call(
        _ssm_kernel,
        grid=(N_CHUNK,),
        in_specs=[
            pl.BlockSpec(blk, lambda k: (fwd_blk(k), 0)),
            pl.BlockSpec(blk, lambda k: (bwd_blk(k), 0)),
            _const_spec(b2.shape),
            _const_spec(c2.shape),
            _const_spec(lamtab.shape),
        ],
        out_specs=[
            pl.BlockSpec(blk, lambda k: (fwd_blk(k), 0)),
            pl.BlockSpec(blk, lambda k: (bwd_blk(k), 0)),
        ],
        out_shape=[jax.ShapeDtypeStruct((ROWS, SSM_DIM), F32)] * 2,
        scratch_shapes=[
            pltpu.VMEM((SSM_STEP_ROWS, 2 * SSM_DIM), BF16),
            pltpu.VMEM((N_GP, SSM_STEP_ROWS, 2 * LANES), F32),
            pltpu.VMEM((SSM_STEP_ROWS, 2 * SSM_DIM), F32),
            pltpu.VMEM((N_GP, 2, SUBLANES, LANES), F32),
        ],
        compiler_params=_cparams(("arbitrary",)),
        name="s5_scan",
    )(us, us, b2, c2, lamtab)


def _ssm_tables(a_re, a_im, log_dt, b_re, b_im, c_re, c_im):
    lam = lax.complex(a_re.astype(F32), a_im.astype(F32))
    dt = jnp.exp(log_dt.astype(F32))[..., None]
    lam_bar = jnp.exp(lam * dt)
    bmat_c = lax.complex(b_re.astype(F32), b_im.astype(F32))
    b_bar = ((lam_bar - 1) / lam)[..., None] * bmat_c

    def pair_rows(v):
        v = v.reshape(2, N_GP, 2, SSM_GROUP, SSM_STATE).transpose(1, 0, 2, 3, 4)
        z = jnp.zeros_like(v[:, :, 0])
        q0 = jnp.concatenate([v[:, :, 0], z], axis=-1)
        q1 = jnp.concatenate([z, v[:, :, 1]], axis=-1)
        return jnp.stack([q0, q1], axis=2)

    slab_pos = (jnp.arange(N_GP) % GP_PER_SLAB)[:, None] == jnp.arange(GP_PER_SLAB)[None, :]
    place = slab_pos.astype(F32)[:, None, :, None, None, None]

    def place_rows(re_v, im_v):
        t = jnp.concatenate([pair_rows(re_v), pair_rows(im_v)], axis=-1)
        t = t[:, :, None] * place
        return t.reshape(N_GP, 2 * LANES, 2 * LANES)

    bt = jnp.swapaxes(b_bar, -1, -2)
    b2 = place_rows(jnp.real(bt), jnp.imag(bt))
    c2 = jnp.swapaxes(place_rows(c_re.astype(F32), -c_im.astype(F32)), 1, 2)
    lam_ri = jnp.stack([jnp.real(lam_bar), jnp.imag(lam_bar)], axis=0)
    lam_ri = lam_ri.reshape(2, 2, N_GP, 1, LANES).transpose(2, 0, 1, 3, 4)
    lamtab = jnp.broadcast_to(lam_ri, (N_GP, 2, 2, BATCH, LANES)).reshape(N_GP, 2, SUBLANES, LANES)
    return b2.astype(BF16), c2.astype(BF16), lamtab


MIX_COLS = 512


def _mix_kernel(cv_ref, yf_ref, yb_ref, us_ref, gt_ref, lng_ref, lnb_ref, sd_ref,
                cwo_ref, wglu_ref, bglu_ref, o_ref):
    cv = cv_ref[...]
    mu = jnp.mean(cv, axis=-1, keepdims=True)
    var = jnp.mean(jnp.square(cv - mu), axis=-1, keepdims=True)
    ln = (cv - mu) * lax.rsqrt(var + EPS) * lng_ref[...] + lnb_ref[...]
    s_conv = jax.nn.silu(ln).astype(BF16)
    y = yf_ref[...] + yb_ref[...] + sd_ref[...] * us_ref[...]
    s_ssm = jax.nn.gelu(y).astype(BF16)
    for c in range(D_MODEL // MIX_COLS):
        lo = c * MIX_COLS
        ca = slice(lo, lo + MIX_COLS)
        cg = slice(D_MODEL + lo, D_MODEL + lo + MIX_COLS)
        y_conv = jnp.dot(s_conv, cwo_ref[:, ca], preferred_element_type=F32)
        za = jnp.dot(s_ssm, wglu_ref[:, ca], preferred_element_type=F32) + bglu_ref[:, ca]
        zg = jnp.dot(s_ssm, wglu_ref[:, cg], preferred_element_type=F32) + bglu_ref[:, cg]
        y_ssm = za * jax.nn.sigmoid(zg)
        mixed = gt_ref[:, ca].astype(F32) * y_conv + gt_ref[:, cg].astype(F32) * y_ssm
        o_ref[:, ca] = mixed.astype(BF16)


def _mix(cv, yf, yb, us, gt, ln_g, ln_b, ssm_d, cwo, wglu, bglu, layer, *, tm, latent_only):
    n_tiles = (ROWS_LAT if latent_only else ROWS) // tm
    row = lambda w: pl.BlockSpec((tm, w), lambda i: (i, 0))
    vec = lambda w: pl.BlockSpec((1, w), lambda i: (0, 0), pipeline_mode=pl.Buffered(1))
    wgt = lambda r, c: pl.BlockSpec((None, r, c), lambda i: (layer, 0, 0), pipeline_mode=pl.Buffered(1))
    return pl.pallas_call(
        _mix_kernel,
        grid=(n_tiles,),
        in_specs=[
            row(CONV_DIM), row(SSM_DIM), row(SSM_DIM), row(SSM_DIM), row(2 * D_MODEL),
            vec(CONV_DIM), vec(CONV_DIM), vec(SSM_DIM),
            wgt(CONV_DIM, D_MODEL), wgt(SSM_DIM, 2 * D_MODEL), vec(2 * D_MODEL),
        ],
        out_specs=row(D_MODEL),
        out_shape=jax.ShapeDtypeStruct((n_tiles * tm, D_MODEL), BF16),
        compiler_params=_cparams(("arbitrary",)),
        name="mixer_mix",
    )(cv, yf, yb, us, gt, ln_g.reshape(1, -1), ln_b.reshape(1, -1), ssm_d.reshape(1, -1),
      cwo, wglu, bglu.reshape(1, -1))


def _outproj_kernel(x_ref, mix_ref, mod_ref, w_ref, o_ref):
    y = jnp.dot(mix_ref[...], w_ref[...], preferred_element_type=F32)
    o_ref[...] = x_ref[...] + _rows8(y, mod_ref[2], jnp.multiply)


def _outproj(x, mix, modtab, w_out, layer, *, tm, latent_only):
    n_lat_tiles = ROWS_LAT // tm
    n_tiles = n_lat_tiles if latent_only else ROWS // tm
    return pl.pallas_call(
        _outproj_kernel,
        grid=(n_tiles,),
        in_specs=[
            pl.BlockSpec((tm, D_MODEL), lambda i: (i, 0)),
            pl.BlockSpec((tm, D_MODEL), lambda i: (i, 0)),
            pl.BlockSpec((None, 3, SUBLANES, D_MODEL),
                         lambda i: (jnp.where(i < n_lat_tiles, 1, 0), 1, 0, 0)),
            pl.BlockSpec((None, D_MODEL, D_MODEL), lambda i: (layer, 0, 0)),
        ],
        out_specs=pl.BlockSpec((tm, D_MODEL), lambda i: (i, 0)),
        out_shape=jax.ShapeDtypeStruct((ROWS, D_MODEL), F32),
        input_output_aliases={0: 0},
        compiler_params=_cparams(("arbitrary",)),
        name="mixer_outproj",
    )(x, mix, modtab, w_out)


def kernel(x, c, ctx, c_ctx, ada_w, ada_b, norm_ffn1, ffn1_w_up, ffn1_w_down, norm_mix, w_in, b_in, conv_dw, conv_db, conv_ln_g, conv_ln_b, conv_w_out, ssm_a_re, ssm_a_im, ssm_log_dt, ssm_b_re, ssm_b_im, ssm_c_re, ssm_c_im, ssm_d, ssm_w_glu, ssm_b_glu, w_out, norm_ffn2, ffn2_w_up, ffn2_w_down, norm_final):
    xs = _to_rows(x, ctx)

    c8 = jnp.zeros((SUBLANES, D_MODEL), F32).at[:BATCH].set(c).at[BATCH].set(c_ctx)
    mod = _ada(c8, ada_w, ada_b).reshape(DEPTH, SUBLANES, N_MOD, D_MODEL)
    lat_rows = jnp.arange(SUBLANES) % BATCH
    mod_lat = mod[:, lat_rows].transpose(0, 2, 1, 3)
    mod_ctx = jnp.broadcast_to(mod[:, BATCH][:, :, None, :], mod_lat.shape)
    modtab = jnp.stack([mod_ctx, mod_lat], axis=1)

    cwo, wglu, wout = conv_w_out.astype(BF16), ssm_w_glu.astype(BF16), w_out.astype(BF16)

    for l in range(DEPTH):
        last = l == DEPTH - 1
        mt = modtab[l]
        xs = _ffn(xs, mt, 0, norm_ffn1[l], ffn1_w_up, ffn1_w_down, l, tm=1024, tf=512)
        uc, us, gt = _inproj(xs, mt, norm_mix[l], w_in, b_in[l], l, tm=1024)

        cv = _conv_rows(uc, conv_dw[l], conv_db[l], rows_out=ROWS_LAT if last else ROWS)
        cv = _conv_vert(uc, conv_dw[l], conv_db[l], cv)
        if not last:
            cv = _conv_ctx(uc, conv_dw[l], conv_db[l], cv)

        b2, c2, lamtab = _ssm_tables(ssm_a_re[l], ssm_a_im[l], ssm_log_dt[l],
                                     ssm_b_re[l], ssm_b_im[l], ssm_c_re[l], ssm_c_im[l])
        yf, yb = _ssm(us, b2, c2, lamtab)

        mix = _mix(cv, yf, yb, us, gt, conv_ln_g[l], conv_ln_b[l], ssm_d[l], cwo, wglu, ssm_b_glu[l], l,
                   tm=512, latent_only=last)
        xs = _outproj(xs, mix, mt, wout, l, tm=512, latent_only=last)
        xs = _ffn(xs, mt, 2, norm_ffn2[l], ffn2_w_up, ffn2_w_down, l, tm=1024, tf=512, latent_only=last)

    return _final_norm(xs, norm_final)
```

```python
import functools

import jax
import jax.numpy as jnp
from jax import lax
from jax.experimental import pallas as pl
from jax.experimental.pallas import tpu as pltpu

F32 = jnp.float32
BF16 = jnp.bfloat16

D_MODEL = 2048
BATCH = 4
SEQ = 2048
CTX_LEN = 256
DEPTH = 2
GRID_W = 64
GRID_ROWS = SEQ // GRID_W
D_FF = 5632
CONV_DIM = 1024
CONV_K = 31
CONV_PAD = CONV_K // 2
SSM_DIM = 1024
SSM_GROUP = 16
SSM_GROUPS = 64
SSM_STATE = 64
N_MOD = 9
N_IN = 2 * CONV_DIM + SSM_DIM + 2 * D_MODEL
FFN_RES = 0.5
EPS = 1e-6

LANES = 128
SUBLANES = 8
ROWS_CTX = CTX_LEN * BATCH
ROWS_LAT = SEQ * BATCH
ROWS = ROWS_CTX + ROWS_LAT
ROWS_PER_GRID_ROW = GRID_W * BATCH

VMEM_LIMIT = 60 * 1024 * 1024

STREAM_TILE_ROWS = 1024
FFN_CHUNK = 512
PROJ_COLS = 1024
ADA_COLS = N_MOD * D_MODEL // 8
MIX_TILE_ROWS = 512
RELAYOUT_TILE_T = 128

GROUPS_PER_SLAB = LANES // SSM_GROUP
N_GP = SSM_GROUPS // 2
GP_PER_SLAB = GROUPS_PER_SLAB // 2
N_SLAB = SSM_DIM // LANES
SSM_CHUNK_T = 64
SSM_CHUNK_ROWS = SSM_CHUNK_T * BATCH
SSM_STEP_ROWS = 2 * SSM_CHUNK_ROWS
N_CHUNK_CTX = CTX_LEN // SSM_CHUNK_T
N_CHUNK = (CTX_LEN + SEQ) // SSM_CHUNK_T
assert 2 * BATCH == SUBLANES


def _cparams(sem):
    return pltpu.CompilerParams(dimension_semantics=sem, vmem_limit_bytes=VMEM_LIMIT)


def _const_spec(shape):
    nd = len(shape)
    return pl.BlockSpec(shape, lambda *_: (0,) * nd, pipeline_mode=pl.Buffered(1))


def _mxu_operand(w_ref):
    w = w_ref[...]
    return w if w.dtype == BF16 else w.astype(BF16)


def _rows8(x, m, op):
    n, d = x.shape
    x3 = x.reshape(n // SUBLANES, SUBLANES, d)
    return op(x3, m[None]).reshape(n, d)


def _row_tile_copy(x_hbm, xbuf_ref, sem, tile, tm):
    rows = pl.ds(pl.multiple_of(tile * tm, tm), tm)
    return pltpu.make_async_copy(x_hbm.at[rows, :], xbuf_ref, sem)


def _interleave_batch(src_ref, o_ref, slab_ref, tq):
    for c in range(D_MODEL // LANES):
        cs = slice(c * LANES, (c + 1) * LANES)
        for b in range(BATCH):
            slab_ref[c, pl.ds(b, tq, stride=BATCH), :] = src_ref[b, :, cs]
        o_ref[:, cs] = slab_ref[c]


def _to_rows_kernel(x_ref, c_ref, o_ref, slab_ref, *, n_lat_tiles, tq):
    i = pl.program_id(0)

    @pl.when(i < n_lat_tiles)
    def _():
        _interleave_batch(x_ref, o_ref, slab_ref, tq)

    @pl.when(i >= n_lat_tiles)
    def _():
        _interleave_batch(c_ref, o_ref, slab_ref, tq)


def _to_rows(x, ctx):
    tq = RELAYOUT_TILE_T
    n_lat_tiles = SEQ // tq
    n_tiles = (SEQ + CTX_LEN) // tq
    return pl.pallas_call(
        functools.partial(_to_rows_kernel, n_lat_tiles=n_lat_tiles, tq=tq),
        grid=(n_tiles,),
        in_specs=[
            pl.BlockSpec((BATCH, tq, D_MODEL), lambda i: (0, jnp.minimum(i, n_lat_tiles - 1), 0)),
            pl.BlockSpec((BATCH, tq, D_MODEL), lambda i: (0, jnp.maximum(i - n_lat_tiles, 0), 0)),
        ],
        out_specs=pl.BlockSpec((tq * BATCH, D_MODEL), lambda i: (i, 0)),
        out_shape=jax.ShapeDtypeStruct((ROWS, D_MODEL), F32),
        scratch_shapes=[pltpu.VMEM((D_MODEL // LANES, tq * BATCH, LANES), F32)],
        compiler_params=_cparams(("arbitrary",)),
        name="to_rows",
    )(x, ctx)


def _final_norm_kernel(x_ref, gain_ref, o_ref, slab_ref, *, tq):
    x = x_ref[...]
    xn = x * lax.rsqrt(jnp.mean(x * x, axis=-1, keepdims=True) + EPS) * gain_ref[...]
    for c in range(D_MODEL // LANES):
        cs = slice(c * LANES, (c + 1) * LANES)
        slab_ref[c] = xn[:, cs]
        for b in range(BATCH):
            o_ref[b, :, cs] = slab_ref[c, pl.ds(b, tq, stride=BATCH), :]


def _final_norm(xs, gain):
    tq = RELAYOUT_TILE_T
    return pl.pallas_call(
        functools.partial(_final_norm_kernel, tq=tq),
        grid=(SEQ // tq,),
        in_specs=[pl.BlockSpec((tq * BATCH, D_MODEL), lambda i: (i, 0)),
                  pl.BlockSpec((1, D_MODEL), lambda i: (0, 0))],
        out_specs=pl.BlockSpec((BATCH, tq, D_MODEL), lambda i: (0, i, 0)),
        out_shape=jax.ShapeDtypeStruct((BATCH, SEQ, D_MODEL), F32),
        scratch_shapes=[pltpu.VMEM((D_MODEL // LANES, tq * BATCH, LANES), F32)],
        compiler_params=_cparams(("arbitrary",)),
        name="final_norm",
    )(xs, gain.reshape(1, D_MODEL))


def _ada_kernel(c_ref, w_ref, b_ref, o_ref):
    c = c_ref[...]
    s = jax.nn.silu(c).astype(BF16)
    o_ref[0] = jnp.dot(s, w_ref[0].astype(BF16), preferred_element_type=F32) + b_ref[0]


def _ada(c8, ada_w, ada_b):
    tn = ADA_COLS
    n = N_MOD * D_MODEL
    return pl.pallas_call(
        _ada_kernel,
        grid=(DEPTH, n // tn),
        in_specs=[
            pl.BlockSpec((SUBLANES, D_MODEL), lambda l, j: (0, 0)),
            pl.BlockSpec((1, D_MODEL, tn), lambda l, j: (l, 0, j)),
            pl.BlockSpec((1, 1, tn), lambda l, j: (l, 0, j)),
        ],
        out_specs=pl.BlockSpec((1, SUBLANES, tn), lambda l, j: (l, 0, j)),
        out_shape=jax.ShapeDtypeStruct((DEPTH, SUBLANES, n), F32),
        compiler_params=_cparams(("arbitrary", "arbitrary")),
        name="ada_mod",
    )(c8, ada_w, ada_b.reshape(DEPTH, 1, n))


def _norm_mod(x, gain_ref, mod_ref):
    y = x * lax.rsqrt(jnp.mean(x * x, axis=-1, keepdims=True) + EPS) * gain_ref[...]
    h = _rows8(y, 1.0 + mod_ref[1], jnp.multiply)
    return _rows8(h, mod_ref[0], jnp.add).astype(BF16)


def _ffn_chunk(hn_ref, wa_ref, wg_ref, wd_ref):
    hn = hn_ref[...]
    a = jnp.dot(hn, _mxu_operand(wa_ref), preferred_element_type=F32)
    g = jnp.dot(hn, _mxu_operand(wg_ref), preferred_element_type=F32)
    act = (jax.nn.silu(g) * a).astype(BF16)
    return jnp.dot(act, _mxu_operand(wd_ref), preferred_element_type=F32)


def _ffn_kernel(x_hbm, mod_ref, gain_ref, wa_ref, wg_ref, wd_ref, o_ref, hn_ref, xbuf_ref, sem,
                *, tm, n_tiles):
    i = pl.program_id(0)
    j = pl.program_id(1)

    x_copy = functools.partial(_row_tile_copy, x_hbm, xbuf_ref, sem, tm=tm)

    def gated_chunk():
        return _rows8(_ffn_chunk(hn_ref, wa_ref, wg_ref, wd_ref), mod_ref[2] * FFN_RES, jnp.multiply)

    @pl.when(j == 0)
    def _():
        @pl.when(i == 0)
        def _():
            x_copy(0).start()

        x_copy(i).wait()
        x = xbuf_ref[...]
        hn_ref[...] = _norm_mod(x, gain_ref, mod_ref)
        o_ref[...] = x + gated_chunk()

    @pl.when(j == 1)
    def _():
        @pl.when(i + 1 < n_tiles)
        def _():
            x_copy(i + 1).start()

        o_ref[...] += gated_chunk()

    @pl.when(j > 1)
    def _():
        o_ref[...] += gated_chunk()


def _ffn(x, modtab, sub, gain, w_up, w_down, layer, *, latent_only=False):
    tm, tf = STREAM_TILE_ROWS, FFN_CHUNK
    nf = D_FF // tf
    assert nf >= 2
    n_lat_tiles = ROWS_LAT // tm
    n_tiles = n_lat_tiles if latent_only else ROWS // tm
    return pl.pallas_call(
        functools.partial(_ffn_kernel, tm=tm, n_tiles=n_tiles),
        grid=(n_tiles, nf),
        in_specs=[
            pl.BlockSpec(memory_space=pl.ANY),
            pl.BlockSpec((None, 3, SUBLANES, D_MODEL),
                         lambda i, j: (jnp.where(i < n_lat_tiles, 1, 0), sub, 0, 0)),
            pl.BlockSpec((1, D_MODEL), lambda i, j: (0, 0)),
            pl.BlockSpec((None, D_MODEL, tf), lambda i, j: (layer, 0, j)),
            pl.BlockSpec((None, D_MODEL, tf), lambda i, j: (layer, 0, j + nf)),
            pl.BlockSpec((None, tf, D_MODEL), lambda i, j: (layer, j, 0)),
        ],
        out_specs=pl.BlockSpec((tm, D_MODEL), lambda i, j: (i, 0)),
        out_shape=jax.ShapeDtypeStruct((n_tiles * tm, D_MODEL), F32),
        scratch_shapes=[pltpu.VMEM((tm, D_MODEL), BF16), pltpu.VMEM((tm, D_MODEL), F32),
                        pltpu.SemaphoreType.DMA(())],
        compiler_params=_cparams(("arbitrary", "arbitrary")),
        name="ffn",
    )(x, modtab, gain.reshape(1, D_MODEL), w_up, w_up, w_down)


def _inproj_kernel(x_hbm, mod_ref, gain_ref, w_ref, b_ref, uc_ref, us_ref, gt_ref, hn_ref, xbuf_ref, sem,
                   *, tm, n_tiles):
    i = pl.program_id(0)
    j = pl.program_id(1)
    x_copy = functools.partial(_row_tile_copy, x_hbm, xbuf_ref, sem, tm=tm)

    def proj():
        return jnp.dot(hn_ref[...], _mxu_operand(w_ref), preferred_element_type=F32) + b_ref[0]

    @pl.when(j == 0)
    def _():
        @pl.when(i == 0)
        def _():
            x_copy(0).start()

        x_copy(i).wait()
        hn_ref[...] = _norm_mod(xbuf_ref[...], gain_ref, mod_ref)
        uc_ref[...] = proj()

    @pl.when(j == 1)
    def _():
        @pl.when(i + 1 < n_tiles)
        def _():
            x_copy(i + 1).start()

        uc_ref[...] = uc_ref[...] * jax.nn.sigmoid(proj())

    @pl.when(j == 2)
    def _():
        us_ref[...] = proj()

    @pl.when(j >= 3)
    def _():
        gt_ref[...] = jax.nn.sigmoid(proj()).astype(BF16)


def _inproj(x, modtab, gain, w_in, b_in, layer):
    tm = STREAM_TILE_ROWS
    tn = PROJ_COLS
    n_lat_tiles = ROWS_LAT // tm
    nblk = N_IN // tn
    n_head = nblk - 2 * D_MODEL // tn
    n_tiles = ROWS // tm
    return pl.pallas_call(
        functools.partial(_inproj_kernel, tm=tm, n_tiles=n_tiles),
        grid=(n_tiles, nblk),
        in_specs=[
            pl.BlockSpec(memory_space=pl.ANY),
            pl.BlockSpec((None, 3, SUBLANES, D_MODEL),
                         lambda i, j: (jnp.where(i < n_lat_tiles, 1, 0), 1, 0, 0)),
            pl.BlockSpec((1, D_MODEL), lambda i, j: (0, 0)),
            pl.BlockSpec((None, D_MODEL, tn), lambda i, j: (layer, 0, j)),
            pl.BlockSpec((1, 1, tn), lambda i, j: (j, 0, 0)),
        ],
        out_specs=[
            pl.BlockSpec((tm, tn), lambda i, j: (i, 0)),
            pl.BlockSpec((tm, tn), lambda i, j: (i, 0)),
            pl.BlockSpec((tm, tn), lambda i, j: (i, jnp.maximum(j - n_head, 0))),
        ],
        out_shape=[
            jax.ShapeDtypeStruct((ROWS, CONV_DIM), F32),
            jax.ShapeDtypeStruct((ROWS, SSM_DIM), F32),
            jax.ShapeDtypeStruct((ROWS, 2 * D_MODEL), BF16),
        ],
        scratch_shapes=[pltpu.VMEM((tm, D_MODEL), BF16), pltpu.VMEM((tm, D_MODEL), F32),
                        pltpu.SemaphoreType.DMA(())],
        compiler_params=_cparams(("arbitrary", "arbitrary")),
        name="mixer_inproj",
    )(x, modtab, gain.reshape(1, D_MODEL), w_in, b_in.reshape(nblk, 1, tn))


CONV_ROW_PAD = 64
CONV_SUB_ROWS = 64
CONV_HALF = CONV_DIM // 2


def _conv_seq_taps(u_ref, w_ref, b_ref, o_ref, pad_ref, pad4_ref, *, n, cb):
    zeros = jnp.zeros((CONV_ROW_PAD, cb), F32)
    pad_ref[0:CONV_ROW_PAD, :] = zeros
    pad_ref[CONV_ROW_PAD + n:, :] = zeros
    pad_ref[CONV_ROW_PAD:CONV_ROW_PAD + n, :] = u_ref[...]
    n_pad = n + 2 * CONV_ROW_PAD
    pad4_ref[0:n_pad - SUBLANES, :] = pad_ref[BATCH:n_pad - SUBLANES + BATCH, :]
    base = CONV_ROW_PAD - CONV_PAD * BATCH
    for c in range(cb // LANES):
        cs = slice(c * LANES, (c + 1) * LANES)
        for r in range(n // CONV_SUB_ROWS):
            r0 = r * CONV_SUB_ROWS
            acc = jnp.broadcast_to(b_ref[:, cs], (CONV_SUB_ROWS, LANES))
            for k in range(CONV_K):
                off = base + r0 + k * BATCH
                if off % SUBLANES == 0:
                    tap = pad_ref[off:off + CONV_SUB_ROWS, cs]
                else:
                    tap = pad4_ref[off - BATCH:off - BATCH + CONV_SUB_ROWS, cs]
                acc = acc + w_ref[k:k + 1, cs] * tap
            o_ref[r0:r0 + CONV_SUB_ROWS, cs] = acc


def _conv_rows_kernel(u_ref, w_ref, b_ref, o_ref, pad_ref, pad4_ref):
    i = pl.program_id(0)
    o_ref[:, CONV_HALF:] = jnp.zeros((ROWS_PER_GRID_ROW, CONV_DIM - CONV_HALF), F32)

    @pl.when(i < GRID_ROWS)
    def _():
        _conv_seq_taps(u_ref, w_ref, b_ref, o_ref, pad_ref, pad4_ref, n=ROWS_PER_GRID_ROW, cb=CONV_HALF)

    @pl.when(i >= GRID_ROWS)
    def _():
        o_ref[:, :CONV_HALF] = jnp.zeros((ROWS_PER_GRID_ROW, CONV_HALF), F32)


def _conv_rows(u, w, b, *, rows_out):
    n = ROWS_PER_GRID_ROW
    n_pad = n + 2 * CONV_ROW_PAD
    return pl.pallas_call(
        _conv_rows_kernel,
        grid=(rows_out // n,),
        in_specs=[
            pl.BlockSpec((n, CONV_HALF), lambda i: (jnp.minimum(i, GRID_ROWS - 1), 0)),
            pl.BlockSpec((CONV_K, CONV_HALF), lambda i: (0, 0)),
            pl.BlockSpec((1, CONV_HALF), lambda i: (0, 0)),
        ],
        out_specs=pl.BlockSpec((n, CONV_DIM), lambda i: (i, 0)),
        out_shape=jax.ShapeDtypeStruct((rows_out, CONV_DIM), F32),
        scratch_shapes=[pltpu.VMEM((n_pad, CONV_HALF), F32), pltpu.VMEM((n_pad, CONV_HALF), F32)],
        compiler_params=_cparams(("arbitrary",)),
        name="conv_rows",
    )(u, w, b.reshape(1, CONV_DIM))


def _conv_ctx_kernel(u_ref, w_ref, b_ref, prev_ref, o_ref, pad_ref, pad4_ref):
    del prev_ref
    _conv_seq_taps(u_ref, w_ref, b_ref, o_ref, pad_ref, pad4_ref, n=ROWS_CTX, cb=LANES)


def _conv_ctx(u, w, b, prev):
    blk0 = ROWS_LAT // ROWS_CTX
    n_pad = ROWS_CTX + 2 * CONV_ROW_PAD
    return pl.pallas_call(
        _conv_ctx_kernel,
        grid=(CONV_DIM // LANES,),
        in_specs=[
            pl.BlockSpec((ROWS_CTX, LANES), lambda c: (blk0, c)),
            pl.BlockSpec((CONV_K, LANES), lambda c: (0, c)),
            pl.BlockSpec((1, LANES), lambda c: (0, c)),
            pl.BlockSpec(memory_space=pl.ANY),
        ],
        out_specs=pl.BlockSpec((ROWS_CTX, LANES), lambda c: (blk0, c)),
        out_shape=jax.ShapeDtypeStruct((ROWS, CONV_DIM), F32),
        scratch_shapes=[pltpu.VMEM((n_pad, LANES), F32), pltpu.VMEM((n_pad, LANES), F32)],
        input_output_aliases={3: 0},
        compiler_params=_cparams(("arbitrary",)),
        name="conv_ctx",
    )(u, w, b.reshape(1, CONV_DIM), prev)


def _conv_vert_kernel(u_ref, w_ref, b_ref, prev_ref, o_ref, pad_ref):
    del prev_ref
    rpg = ROWS_PER_GRID_ROW
    zeros = jnp.zeros((CONV_PAD * rpg, LANES), F32)
    pad_ref[0:CONV_PAD * rpg, :] = zeros
    pad_ref[(CONV_PAD + GRID_ROWS) * rpg:, :] = zeros
    pad_ref[CONV_PAD * rpg:(CONV_PAD + GRID_ROWS) * rpg, :] = u_ref[...]
    bias = jnp.broadcast_to(b_ref[...], (CONV_SUB_ROWS, LANES))

    def body(r, carry):
        for s in range(rpg // CONV_SUB_ROWS):
            acc = bias
            for k in range(CONV_K):
                off = pl.multiple_of((r + k) * rpg + s * CONV_SUB_ROWS, CONV_SUB_ROWS)
                acc = acc + w_ref[k:k + 1, :] * pad_ref[pl.ds(off, CONV_SUB_ROWS), :]
            o0 = pl.multiple_of(r * rpg + s * CONV_SUB_ROWS, CONV_SUB_ROWS)
            o_ref[pl.ds(o0, CONV_SUB_ROWS), :] = acc
        return carry

    lax.fori_loop(0, GRID_ROWS, body, 0)


def _conv_vert(u, w, b, prev):
    half_blk = CONV_HALF // LANES
    return pl.pallas_call(
        _conv_vert_kernel,
        grid=(half_blk,),
        in_specs=[
            pl.BlockSpec((ROWS_LAT, LANES), lambda c: (0, c + half_blk)),
            pl.BlockSpec((CONV_K, LANES), lambda c: (0, c + half_blk)),
            pl.BlockSpec((1, LANES), lambda c: (0, c + half_blk)),
            pl.BlockSpec(memory_space=pl.ANY),
        ],
        out_specs=pl.BlockSpec((ROWS_LAT, LANES), lambda c: (0, c + half_blk)),
        out_shape=jax.ShapeDtypeStruct(prev.shape, F32),
        scratch_shapes=[pltpu.VMEM(((GRID_ROWS + 2 * CONV_PAD) * ROWS_PER_GRID_ROW, LANES), F32)],
        input_output_aliases={3: 0},
        compiler_params=_cparams(("arbitrary",)),
        name="conv_vert",
    )(u, w, b.reshape(1, CONV_DIM), prev)


def _ssm_kernel(uf_ref, ub_ref, b2_ref, c2_ref, lam_ref, yf_ref, yb_ref,
                lhs_ref, s_ref, y2_ref, hst_ref):
    k = pl.program_id(0)

    @pl.when(k == 0)
    def _():
        hst_ref[...] = jnp.zeros_like(hst_ref)

    n_pair = SSM_CHUNK_T // 2
    low = lax.broadcasted_iota(jnp.int32, (SUBLANES, SSM_DIM), 0) < BATCH
    zero = jnp.zeros((SUBLANES, SSM_DIM), F32)
    for m in range(n_pair):
        f = uf_ref[m * SUBLANES:(m + 1) * SUBLANES, :]
        mb = n_pair - 1 - m
        g = ub_ref[mb * SUBLANES:(mb + 1) * SUBLANES, :]
        fr = pltpu.roll(f, BATCH, 0)
        gr = pltpu.roll(g, BATCH, 0)
        fwd = jnp.concatenate([jnp.where(low, f, zero), jnp.where(low, fr, zero)], axis=0).astype(BF16)
        bwd = jnp.concatenate([jnp.where(low, zero, g), jnp.where(low, zero, gr)], axis=0).astype(BF16)
        r0 = m * 2 * SUBLANES
        for j in range(N_SLAB):
            cs = slice(j * LANES, (j + 1) * LANES)
            lhs_ref[r0:r0 + 2 * SUBLANES, 2 * j * LANES:(2 * j + 1) * LANES] = fwd[:, cs]
            lhs_ref[r0:r0 + 2 * SUBLANES, (2 * j + 1) * LANES:(2 * j + 2) * LANES] = bwd[:, cs]

    re = slice(0, LANES)
    im = slice(LANES, 2 * LANES)

    def project(j):
        for gp in range(j * GP_PER_SLAB, (j + 1) * GP_PER_SLAB):
            s_ref[gp] = jnp.dot(lhs_ref[:, 2 * j * LANES:(2 * j + 2) * LANES], b2_ref[gp],
                                preferred_element_type=F32)

    def scan(j):
        for gp in range(j * GP_PER_SLAB, (j + 1) * GP_PER_SLAB):
            lr, li = lam_ref[gp, 0], lam_ref[gp, 1]
            hre, him = hst_ref[gp, 0], hst_ref[gp, 1]
            for s in range(SSM_CHUNK_T):
                rows = slice(s * SUBLANES, (s + 1) * SUBLANES)
                nre = lr * hre - li * him + s_ref[gp, rows, re]
                nim = lr * him + li * hre + s_ref[gp, rows, im]
                s_ref[gp, rows, re] = nre
                s_ref[gp, rows, im] = nim
                hre, him = nre, nim
            hst_ref[gp, 0] = hre
            hst_ref[gp, 1] = him

    def read_out(j):
        acc = None
        for gp in range(j * GP_PER_SLAB, (j + 1) * GP_PER_SLAB):
            p = jnp.dot(s_ref[gp].astype(BF16), c2_ref[gp], preferred_element_type=F32)
            acc = p if acc is None else acc + p
        y2_ref[:, 2 * j * LANES:(2 * j + 2) * LANES] = acc

    project(0)
    for j in range(N_SLAB):
        if j + 1 < N_SLAB:
            project(j + 1)
        scan(j)
        if j >= 1:
            read_out(j - 1)
    read_out(N_SLAB - 1)

    low1 = lax.broadcasted_iota(jnp.int32, (SUBLANES, LANES), 0) < BATCH
    for m in range(n_pair):
        mb = n_pair - 1 - m
        r0 = m * 2 * SUBLANES
        for j in range(N_SLAB):
            cf = slice(2 * j * LANES, (2 * j + 1) * LANES)
            cb = slice((2 * j + 1) * LANES, (2 * j + 2) * LANES)
            cs = slice(j * LANES, (j + 1) * LANES)
            ef = y2_ref[r0:r0 + SUBLANES, cf]
            of = pltpu.roll(y2_ref[r0 + SUBLANES:r0 + 2 * SUBLANES, cf], BATCH, 0)
            yf_ref[m * SUBLANES:(m + 1) * SUBLANES, cs] = jnp.where(low1, ef, of)
            eb = y2_ref[r0:r0 + SUBLANES, cb]
            ob = pltpu.roll(y2_ref[r0 + SUBLANES:r0 + 2 * SUBLANES, cb], BATCH, 0)
            yb_ref[mb * SUBLANES:(mb + 1) * SUBLANES, cs] = jnp.where(low1, ob, eb)


def _ssm(us, b2, c2, lamtab):
    n_lat = N_CHUNK - N_CHUNK_CTX

    def fwd_blk(k):
        return jnp.where(k < N_CHUNK_CTX, n_lat + k, k - N_CHUNK_CTX)

    def bwd_blk(k):
        return N_CHUNK - 1 - k

    blk = (SSM_CHUNK_ROWS, SSM_DIM)
    return pl.pallas_call(
        _ssm_kernel,
        grid=(N_CHUNK,),
        in_specs=[
            pl.BlockSpec(blk, lambda k: (fwd_blk(k), 0)),
            pl.BlockSpec(blk, lambda k: (bwd_blk(k), 0)),
            _const_spec(b2.shape),
            _const_spec(c2.shape),
            _const_spec(lamtab.shape),
        ],
        out_specs=[
            pl.BlockSpec(blk, lambda k: (fwd_blk(k), 0)),
            pl.BlockSpec(blk, lambda k: (bwd_blk(k), 0)),
        ],
        out_shape=[jax.ShapeDtypeStruct((ROWS, SSM_DIM), F32)] * 2,
        scratch_shapes=[
            pltpu.VMEM((SSM_STEP_ROWS, 2 * SSM_DIM), BF16),
            pltpu.VMEM((N_GP, SSM_STEP_ROWS, 2 * LANES), F32),
            pltpu.VMEM((SSM_STEP_ROWS, 2 * SSM_DIM), F32),
            pltpu.VMEM((N_GP, 2, SUBLANES, LANES), F32),
        ],
        compiler_params=_cparams(("arbitrary",)),
        name="s5_scan",
    )(us, us, b2, c2, lamtab)


def _ssm_tables(a_re, a_im, log_dt, b_re, b_im, c_re, c_im):
    lam = lax.complex(a_re.astype(F32), a_im.astype(F32))
    dt = jnp.exp(log_dt.astype(F32))[..., None]
    lam_bar = jnp.exp(lam * dt)
    bmat_c = lax.complex(b_re.astype(F32), b_im.astype(F32))
    b_bar = ((lam_bar - 1) / lam)[..., None] * bmat_c

    def pair_rows(v):
        v = v.reshape(2, N_GP, 2, SSM_GROUP, SSM_STATE).transpose(1, 0, 2, 3, 4)
        z = jnp.zeros_like(v[:, :, 0])
        q0 = jnp.concatenate([v[:, :, 0], z], axis=-1)
        q1 = jnp.concatenate([z, v[:, :, 1]], axis=-1)
        return jnp.stack([q0, q1], axis=2)

    slab_pos = (jnp.arange(N_GP) % GP_PER_SLAB)[:, None] == jnp.arange(GP_PER_SLAB)[None, :]
    place = slab_pos.astype(F32)[:, None, :, None, None, None]

    def place_rows(re_v, im_v):
        t = jnp.concatenate([pair_rows(re_v), pair_rows(im_v)], axis=-1)
        t = t[:, :, None] * place
        return t.reshape(N_GP, 2 * LANES, 2 * LANES)

    bt = jnp.swapaxes(b_bar, -1, -2)
    b2 = place_rows(jnp.real(bt), jnp.imag(bt))
    c2 = jnp.swapaxes(place_rows(c_re.astype(F32), -c_im.astype(F32)), 1, 2)
    lam_ri = jnp.stack([jnp.real(lam_bar), jnp.imag(lam_bar)], axis=0)
    lam_ri = lam_ri.reshape(2, 2, N_GP, 1, LANES).transpose(2, 0, 1, 3, 4)
    lamtab = jnp.broadcast_to(lam_ri, (N_GP, 2, 2, BATCH, LANES)).reshape(N_GP, 2, SUBLANES, LANES)
    return b2.astype(BF16), c2.astype(BF16), lamtab


MIX_COLS = 512


def _mix_kernel(cv_ref, yf_ref, yb_ref, us_ref, gt_ref, lng_ref, lnb_ref, sd_ref,
                cwo_ref, wglu_ref, bglu_ref, o_ref):
    cv = cv_ref[...]
    mu = jnp.mean(cv, axis=-1, keepdims=True)
    var = jnp.mean(jnp.square(cv - mu), axis=-1, keepdims=True)
    ln = (cv - mu) * lax.rsqrt(var + EPS) * lng_ref[...] + lnb_ref[...]
    s_conv = jax.nn.silu(ln).astype(BF16)
    y = yf_ref[...] + yb_ref[...] + sd_ref[...] * us_ref[...]
    s_ssm = jax.nn.gelu(y).astype(BF16)
    for c in range(D_MODEL // MIX_COLS):
        lo = c * MIX_COLS
        ca = slice(lo, lo + MIX_COLS)
        cg = slice(D_MODEL + lo, D_MODEL + lo + MIX_COLS)
        y_conv = jnp.dot(s_conv, cwo_ref[:, ca], preferred_element_type=F32)
        za = jnp.dot(s_ssm, wglu_ref[:, ca], preferred_element_type=F32) + bglu_ref[:, ca]
        zg = jnp.dot(s_ssm, wglu_ref[:, cg], preferred_element_type=F32) + bglu_ref[:, cg]
        y_ssm = za * jax.nn.sigmoid(zg)
        mixed = gt_ref[:, ca].astype(F32) * y_conv + gt_ref[:, cg].astype(F32) * y_ssm
        o_ref[:, ca] = mixed.astype(BF16)


def _mix(cv, yf, yb, us, gt, ln_g, ln_b, ssm_d, cwo, wglu, bglu, layer, *, latent_only):
    tm = MIX_TILE_ROWS
    n_tiles = (ROWS_LAT if latent_only else ROWS) // tm
    row = lambda w: pl.BlockSpec((tm, w), lambda i: (i, 0))
    vec = lambda w: pl.BlockSpec((1, w), lambda i: (0, 0), pipeline_mode=pl.Buffered(1))
    wgt = lambda r, c: pl.BlockSpec((None, r, c), lambda i: (layer, 0, 0), pipeline_mode=pl.Buffered(1))
    return pl.pallas_call(
        _mix_kernel,
        grid=(n_tiles,),
        in_specs=[
            row(CONV_DIM), row(SSM_DIM), row(SSM_DIM), row(SSM_DIM), row(2 * D_MODEL),
            vec(CONV_DIM), vec(CONV_DIM), vec(SSM_DIM),
            wgt(CONV_DIM, D_MODEL), wgt(SSM_DIM, 2 * D_MODEL), vec(2 * D_MODEL),
        ],
        out_specs=row(D_MODEL),
        out_shape=jax.ShapeDtypeStruct((n_tiles * tm, D_MODEL), BF16),
        compiler_params=_cparams(("arbitrary",)),
        name="mixer_mix",
    )(cv, yf, yb, us, gt, ln_g.reshape(1, -1), ln_b.reshape(1, -1), ssm_d.reshape(1, -1),
      cwo, wglu, bglu.reshape(1, -1))


def _outproj_kernel(x_ref, mix_ref, mod_ref, w_ref, o_ref):
    y = jnp.dot(mix_ref[...], w_ref[...], preferred_element_type=F32)
    o_ref[...] = x_ref[...] + _rows8(y, mod_ref[2], jnp.multiply)


def _outproj(x, mix, modtab, w_out, layer, *, latent_only):
    tm = MIX_TILE_ROWS
    n_lat_tiles = ROWS_LAT // tm
    n_tiles = n_lat_tiles if latent_only else ROWS // tm
    return pl.pallas_call(
        _outproj_kernel,
        grid=(n_tiles,),
        in_specs=[
            pl.BlockSpec((tm, D_MODEL), lambda i: (i, 0)),
            pl.BlockSpec((tm, D_MODEL), lambda i: (i, 0)),
            pl.BlockSpec((None, 3, SUBLANES, D_MODEL),
                         lambda i: (jnp.where(i < n_lat_tiles, 1, 0), 1, 0, 0)),
            pl.BlockSpec((None, D_MODEL, D_MODEL), lambda i: (layer, 0, 0)),
        ],
        out_specs=pl.BlockSpec((tm, D_MODEL), lambda i: (i, 0)),
        out_shape=jax.ShapeDtypeStruct((ROWS, D_MODEL), F32),
        input_output_aliases={0: 0},
        compiler_params=_cparams(("arbitrary",)),
        name="mixer_outproj",
    )(x, mix, modtab, w_out)


def kernel(x, c, ctx, c_ctx, ada_w, ada_b, norm_ffn1, ffn1_w_up, ffn1_w_down, norm_mix, w_in, b_in, conv_dw, conv_db, conv_ln_g, conv_ln_b, conv_w_out, ssm_a_re, ssm_a_im, ssm_log_dt, ssm_b_re, ssm_b_im, ssm_c_re, ssm_c_im, ssm_d, ssm_w_glu, ssm_b_glu, w_out, norm_ffn2, ffn2_w_up, ffn2_w_down, norm_final):
    xs = _to_rows(x, ctx)

    c8 = jnp.zeros((SUBLANES, D_MODEL), F32).at[:BATCH].set(c).at[BATCH].set(c_ctx)
    mod = _ada(c8, ada_w, ada_b).reshape(DEPTH, SUBLANES, N_MOD, D_MODEL)
    lat_rows = jnp.arange(SUBLANES) % BATCH
    mod_lat = mod[:, lat_rows].transpose(0, 2, 1, 3)
    mod_ctx = jnp.broadcast_to(mod[:, BATCH][:, :, None, :], mod_lat.shape)
    modtab = jnp.stack([mod_ctx, mod_lat], axis=1)

    cwo, wglu, wout = conv_w_out.astype(BF16), ssm_w_glu.astype(BF16), w_out.astype(BF16)

    for l in range(DEPTH):
        last = l == DEPTH - 1
        mt = modtab[l]
        xs = _ffn(xs, mt, 0, norm_ffn1[l], ffn1_w_up, ffn1_w_down, l)
        uc, us, gt = _inproj(xs, mt, norm_mix[l], w_in, b_in[l], l)

        cv = _conv_rows(uc, conv_dw[l], conv_db[l], rows_out=ROWS_LAT if last else ROWS)
        cv = _conv_vert(uc, conv_dw[l], conv_db[l], cv)
        if not last:
            cv = _conv_ctx(uc, conv_dw[l], conv_db[l], cv)

        b2, c2, lamtab = _ssm_tables(ssm_a_re[l], ssm_a_im[l], ssm_log_dt[l],
                                     ssm_b_re[l], ssm_b_im[l], ssm_c_re[l], ssm_c_im[l])
        yf, yb = _ssm(us, b2, c2, lamtab)

        mix = _mix(cv, yf, yb, us, gt, conv_ln_g[l], conv_ln_b[l], ssm_d[l], cwo, wglu, ssm_b_glu[l], l,
                   latent_only=last)
        xs = _outproj(xs, mix, mt, wout, l, latent_only=last)
        xs = _ffn(xs, mt, 2, norm_ffn2[l], ffn2_w_up, ffn2_w_down, l, latent_only=last)

    return _final_norm(xs, norm_final)
```

```python
import functools

import jax
import jax.numpy as jnp
from jax import lax
from jax.experimental import pallas as pl
from jax.experimental.pallas import tpu as pltpu

F32 = jnp.float32
BF16 = jnp.bfloat16

D_MODEL = 2048
BATCH = 4
SEQ = 2048
CTX_LEN = 256
DEPTH = 2
GRID_W = 64
GRID_ROWS = SEQ // GRID_W
D_FF = 5632
CONV_DIM = 1024
CONV_K = 31
CONV_PAD = CONV_K // 2
SSM_DIM = 1024
SSM_GROUP = 16
SSM_GROUPS = 64
SSM_STATE = 64
N_MOD = 9
N_IN = 2 * CONV_DIM + SSM_DIM + 2 * D_MODEL
FFN_RES = 0.5
EPS = 1e-6

LANES = 128
SUBLANES = 8
ROWS_CTX = CTX_LEN * BATCH
ROWS_LAT = SEQ * BATCH
ROWS = ROWS_CTX + ROWS_LAT
ROWS_PER_GRID_ROW = GRID_W * BATCH

VMEM_LIMIT = 60 * 1024 * 1024

STREAM_TILE_ROWS = 1024
FFN_CHUNK = 512
PROJ_COLS = 1024
MIX_TILE_ROWS = 512
RELAYOUT_TILE_T = 128

GROUPS_PER_SLAB = LANES // SSM_GROUP
N_GP = SSM_GROUPS // 2
GP_PER_SLAB = GROUPS_PER_SLAB // 2
N_SLAB = SSM_DIM // LANES
SSM_CHUNK_T = 64
SSM_CHUNK_ROWS = SSM_CHUNK_T * BATCH
SSM_STEP_ROWS = 2 * SSM_CHUNK_ROWS
N_CHUNK_CTX = CTX_LEN // SSM_CHUNK_T
N_CHUNK = (CTX_LEN + SEQ) // SSM_CHUNK_T
assert 2 * BATCH == SUBLANES


def _cparams(sem):
    return pltpu.CompilerParams(dimension_semantics=sem, vmem_limit_bytes=VMEM_LIMIT)


def _layer_spec(shape, layer):
    nd = len(shape) - 1
    return pl.BlockSpec((None,) + tuple(shape[1:]), lambda *_: (layer,) + (0,) * nd,
                        pipeline_mode=pl.Buffered(1))


def _mxu_operand(w_ref):
    w = w_ref[...]
    return w if w.dtype == BF16 else w.astype(BF16)


def _rows8(x, m, op):
    n, d = x.shape
    x3 = x.reshape(n // SUBLANES, SUBLANES, d)
    return op(x3, m[None]).reshape(n, d)


def _row_tile_copy(x_hbm, xbuf_ref, sem, tile, tm):
    rows = pl.ds(pl.multiple_of(tile * tm, tm), tm)
    return pltpu.make_async_copy(x_hbm.at[rows, :], xbuf_ref, sem)


def _interleave_batch(src_ref, o_ref, slab_ref, tq):
    for c in range(D_MODEL // LANES):
        cs = slice(c * LANES, (c + 1) * LANES)
        for b in range(BATCH):
            slab_ref[c, pl.ds(b, tq, stride=BATCH), :] = src_ref[b, :, cs]
        o_ref[:, cs] = slab_ref[c]


def _to_rows_kernel(x_ref, c_ref, o_ref, slab_ref, *, n_lat_tiles, tq):
    i = pl.program_id(0)

    @pl.when(i < n_lat_tiles)
    def _():
        _interleave_batch(x_ref, o_ref, slab_ref, tq)

    @pl.when(i >= n_lat_tiles)
    def _():
        _interleave_batch(c_ref, o_ref, slab_ref, tq)


def _to_rows(x, ctx):
    tq = RELAYOUT_TILE_T
    n_lat_tiles = SEQ // tq
    n_tiles = (SEQ + CTX_LEN) // tq
    return pl.pallas_call(
        functools.partial(_to_rows_kernel, n_lat_tiles=n_lat_tiles, tq=tq),
        grid=(n_tiles,),
        in_specs=[
            pl.BlockSpec((BATCH, tq, D_MODEL), lambda i: (0, jnp.minimum(i, n_lat_tiles - 1), 0)),
            pl.BlockSpec((BATCH, tq, D_MODEL), lambda i: (0, jnp.maximum(i - n_lat_tiles, 0), 0)),
        ],
        out_specs=pl.BlockSpec((tq * BATCH, D_MODEL), lambda i: (i, 0)),
        out_shape=jax.ShapeDtypeStruct((ROWS, D_MODEL), F32),
        scratch_shapes=[pltpu.VMEM((D_MODEL // LANES, tq * BATCH, LANES), F32)],
        compiler_params=_cparams(("arbitrary",)),
        name="to_rows",
    )(x, ctx)


def _final_norm_kernel(x_ref, gain_ref, o_ref, slab_ref, *, tq):
    x = x_ref[...]
    xn = x * lax.rsqrt(jnp.mean(x * x, axis=-1, keepdims=True) + EPS) * gain_ref[...]
    for c in range(D_MODEL // LANES):
        cs = slice(c * LANES, (c + 1) * LANES)
        slab_ref[c] = xn[:, cs]
        for b in range(BATCH):
            o_ref[b, :, cs] = slab_ref[c, pl.ds(b, tq, stride=BATCH), :]


def _final_norm(xs, gain):
    tq = RELAYOUT_TILE_T
    return pl.pallas_call(
        functools.partial(_final_norm_kernel, tq=tq),
        grid=(SEQ // tq,),
        in_specs=[pl.BlockSpec((tq * BATCH, D_MODEL), lambda i: (i, 0)),
                  pl.BlockSpec((1, D_MODEL), lambda i: (0, 0))],
        out_specs=pl.BlockSpec((BATCH, tq, D_MODEL), lambda i: (0, i, 0)),
        out_shape=jax.ShapeDtypeStruct((BATCH, SEQ, D_MODEL), F32),
        scratch_shapes=[pltpu.VMEM((D_MODEL // LANES, tq * BATCH, LANES), F32)],
        compiler_params=_cparams(("arbitrary",)),
        name="final_norm",
    )(xs, gain.reshape(1, D_MODEL))


def _ada_kernel(c_ref, w_ref, b_ref, o_ref):
    s = jax.nn.silu(c_ref[...]).astype(BF16)
    o_ref[...] = jnp.dot(s, w_ref[...].astype(BF16), preferred_element_type=F32) + b_ref[...]


def _ada(c_rows, ada_w, ada_b):
    n_rows = c_rows.shape[0]
    return pl.pallas_call(
        _ada_kernel,
        grid=(DEPTH, N_MOD),
        in_specs=[
            pl.BlockSpec((n_rows, D_MODEL), lambda l, m: (0, 0)),
            pl.BlockSpec((None, D_MODEL, D_MODEL), lambda l, m: (l, 0, m)),
            pl.BlockSpec((None, None, 1, D_MODEL), lambda l, m: (l, m, 0, 0)),
        ],
        out_specs=pl.BlockSpec((None, None, n_rows, D_MODEL), lambda l, m: (l, m, 0, 0)),
        out_shape=jax.ShapeDtypeStruct((DEPTH, N_MOD, n_rows, D_MODEL), F32),
        compiler_params=_cparams(("arbitrary", "arbitrary")),
        name="ada_mod",
    )(c_rows, ada_w, ada_b.reshape(DEPTH, N_MOD, 1, D_MODEL))


def _norm_mod(x, gain_ref, mod_ref):
    y = x * lax.rsqrt(jnp.mean(x * x, axis=-1, keepdims=True) + EPS) * gain_ref[...]
    h = _rows8(y, 1.0 + mod_ref[1], jnp.multiply)
    return _rows8(h, mod_ref[0], jnp.add).astype(BF16)


def _ffn_chunk(hn_ref, wa_ref, wg_ref, wd_ref):
    hn = hn_ref[...]
    a = jnp.dot(hn, _mxu_operand(wa_ref), preferred_element_type=F32)
    g = jnp.dot(hn, _mxu_operand(wg_ref), preferred_element_type=F32)
    act = (jax.nn.silu(g) * a).astype(BF16)
    return jnp.dot(act, _mxu_operand(wd_ref), preferred_element_type=F32)


def _ffn_kernel(x_hbm, mod_ref, gain_ref, wa_ref, wg_ref, wd_ref, o_ref, hn_ref, xbuf_ref, sem,
                *, tm, n_tiles):
    i = pl.program_id(0)
    j = pl.program_id(1)

    x_copy = functools.partial(_row_tile_copy, x_hbm, xbuf_ref, sem, tm=tm)

    def gated_chunk():
        return _rows8(_ffn_chunk(hn_ref, wa_ref, wg_ref, wd_ref), mod_ref[2] * FFN_RES, jnp.multiply)

    @pl.when(j == 0)
    def _():
        @pl.when(i == 0)
        def _():
            x_copy(0).start()

        x_copy(i).wait()
        x = xbuf_ref[...]
        hn_ref[...] = _norm_mod(x, gain_ref, mod_ref)
        o_ref[...] = x + gated_chunk()

    @pl.when(j == 1)
    def _():
        @pl.when(i + 1 < n_tiles)
        def _():
            x_copy(i + 1).start()

        o_ref[...] += gated_chunk()

    @pl.when(j > 1)
    def _():
        o_ref[...] += gated_chunk()


def _ffn(x, modtab, sub, gain, w_up, w_down, layer, *, latent_only=False):
    tm, tf = STREAM_TILE_ROWS, FFN_CHUNK
    nf = D_FF // tf
    assert nf >= 2
    n_lat_tiles = ROWS_LAT // tm
    n_tiles = n_lat_tiles if latent_only else ROWS // tm
    return pl.pallas_call(
        functools.partial(_ffn_kernel, tm=tm, n_tiles=n_tiles),
        grid=(n_tiles, nf),
        in_specs=[
            pl.BlockSpec(memory_space=pl.ANY),
            pl.BlockSpec((3, SUBLANES, D_MODEL),
                         lambda i, j: (sub, jnp.where(i < n_lat_tiles, 0, 1), 0)),
            pl.BlockSpec((1, D_MODEL), lambda i, j: (0, 0)),
            pl.BlockSpec((None, D_MODEL, tf), lambda i, j: (layer, 0, j)),
            pl.BlockSpec((None, D_MODEL, tf), lambda i, j: (layer, 0, j + nf)),
            pl.BlockSpec((None, tf, D_MODEL), lambda i, j: (layer, j, 0)),
        ],
        out_specs=pl.BlockSpec((tm, D_MODEL), lambda i, j: (i, 0)),
        out_shape=jax.ShapeDtypeStruct((n_tiles * tm, D_MODEL), F32),
        scratch_shapes=[pltpu.VMEM((tm, D_MODEL), BF16), pltpu.VMEM((tm, D_MODEL), F32),
                        pltpu.SemaphoreType.DMA(())],
        compiler_params=_cparams(("arbitrary", "arbitrary")),
        name="ffn",
    )(x, modtab, gain.reshape(1, D_MODEL), w_up, w_up, w_down)


def _inproj_kernel(x_hbm, mod_ref, gain_ref, w_ref, b_ref, uc_ref, us_ref, gt_ref, hn_ref, xbuf_ref, sem,
                   *, tm, n_tiles):
    i = pl.program_id(0)
    j = pl.program_id(1)
    x_copy = functools.partial(_row_tile_copy, x_hbm, xbuf_ref, sem, tm=tm)

    def proj():
        return jnp.dot(hn_ref[...], _mxu_operand(w_ref), preferred_element_type=F32) + b_ref[0]

    @pl.when(j == 0)
    def _():
        @pl.when(i == 0)
        def _():
            x_copy(0).start()

        x_copy(i).wait()
        hn_ref[...] = _norm_mod(xbuf_ref[...], gain_ref, mod_ref)
        uc_ref[...] = proj()

    @pl.when(j == 1)
    def _():
        @pl.when(i + 1 < n_tiles)
        def _():
            x_copy(i + 1).start()

        uc_ref[...] = uc_ref[...] * jax.nn.sigmoid(proj())

    @pl.when(j == 2)
    def _():
        us_ref[...] = proj()

    @pl.when(j >= 3)
    def _():
        gt_ref[...] = jax.nn.sigmoid(proj()).astype(BF16)


def _inproj(x, modtab, gain, w_in, b_in, layer):
    tm = STREAM_TILE_ROWS
    tn = PROJ_COLS
    n_lat_tiles = ROWS_LAT // tm
    nblk = N_IN // tn
    n_head = nblk - 2 * D_MODEL // tn
    n_tiles = ROWS // tm
    return pl.pallas_call(
        functools.partial(_inproj_kernel, tm=tm, n_tiles=n_tiles),
        grid=(n_tiles, nblk),
        in_specs=[
            pl.BlockSpec(memory_space=pl.ANY),
            pl.BlockSpec((3, SUBLANES, D_MODEL),
                         lambda i, j: (1, jnp.where(i < n_lat_tiles, 0, 1), 0)),
            pl.BlockSpec((1, D_MODEL), lambda i, j: (0, 0)),
            pl.BlockSpec((None, D_MODEL, tn), lambda i, j: (layer, 0, j)),
            pl.BlockSpec((1, 1, tn), lambda i, j: (j, 0, 0)),
        ],
        out_specs=[
            pl.BlockSpec((tm, tn), lambda i, j: (i, 0)),
            pl.BlockSpec((tm, tn), lambda i, j: (i, 0)),
            pl.BlockSpec((tm, tn), lambda i, j: (i, jnp.maximum(j - n_head, 0))),
        ],
        out_shape=[
            jax.ShapeDtypeStruct((ROWS, CONV_DIM), F32),
            jax.ShapeDtypeStruct((ROWS, SSM_DIM), F32),
            jax.ShapeDtypeStruct((ROWS, 2 * D_MODEL), BF16),
        ],
        scratch_shapes=[pltpu.VMEM((tm, D_MODEL), BF16), pltpu.VMEM((tm, D_MODEL), F32),
                        pltpu.SemaphoreType.DMA(())],
        compiler_params=_cparams(("arbitrary", "arbitrary")),
        name="mixer_inproj",
    )(x, modtab, gain.reshape(1, D_MODEL), w_in, b_in.reshape(nblk, 1, tn))


CONV_ROW_PAD = 64
CONV_SUB_ROWS = 64
CONV_HALF = CONV_DIM // 2


def _conv_seq_taps(u_ref, w_ref, b_ref, o_ref, pad_ref, pad4_ref, *, n, cb):
    zeros = jnp.zeros((CONV_ROW_PAD, cb), F32)
    pad_ref[0:CONV_ROW_PAD, :] = zeros
    pad_ref[CONV_ROW_PAD + n:, :] = zeros
    pad_ref[CONV_ROW_PAD:CONV_ROW_PAD + n, :] = u_ref[...]
    n_pad = n + 2 * CONV_ROW_PAD
    pad4_ref[0:n_pad - SUBLANES, :] = pad_ref[BATCH:n_pad - SUBLANES + BATCH, :]
    base = CONV_ROW_PAD - CONV_PAD * BATCH
    for c in range(cb // LANES):
        cs = slice(c * LANES, (c + 1) * LANES)
        for r in range(n // CONV_SUB_ROWS):
            r0 = r * CONV_SUB_ROWS
            acc = jnp.broadcast_to(b_ref[:, cs], (CONV_SUB_ROWS, LANES))
            for k in range(CONV_K):
                off = base + r0 + k * BATCH
                if off % SUBLANES == 0:
                    tap = pad_ref[off:off + CONV_SUB_ROWS, cs]
                else:
                    tap = pad4_ref[off - BATCH:off - BATCH + CONV_SUB_ROWS, cs]
                acc = acc + w_ref[k:k + 1, cs] * tap
            o_ref[r0:r0 + CONV_SUB_ROWS, cs] = acc


def _conv_rows_kernel(u_ref, w_ref, b_ref, o_ref, pad_ref, pad4_ref):
    i = pl.program_id(0)
    o_ref[:, CONV_HALF:] = jnp.zeros((ROWS_PER_GRID_ROW, CONV_DIM - CONV_HALF), F32)

    @pl.when(i < GRID_ROWS)
    def _():
        _conv_seq_taps(u_ref, w_ref, b_ref, o_ref, pad_ref, pad4_ref, n=ROWS_PER_GRID_ROW, cb=CONV_HALF)

    @pl.when(i >= GRID_ROWS)
    def _():
        o_ref[:, :CONV_HALF] = jnp.zeros((ROWS_PER_GRID_ROW, CONV_HALF), F32)


def _conv_rows(u, w, b, *, rows_out):
    n = ROWS_PER_GRID_ROW
    n_pad = n + 2 * CONV_ROW_PAD
    return pl.pallas_call(
        _conv_rows_kernel,
        grid=(rows_out // n,),
        in_specs=[
            pl.BlockSpec((n, CONV_HALF), lambda i: (jnp.minimum(i, GRID_ROWS - 1), 0)),
            pl.BlockSpec((CONV_K, CONV_HALF), lambda i: (0, 0)),
            pl.BlockSpec((1, CONV_HALF), lambda i: (0, 0)),
        ],
        out_specs=pl.BlockSpec((n, CONV_DIM), lambda i: (i, 0)),
        out_shape=jax.ShapeDtypeStruct((rows_out, CONV_DIM), F32),
        scratch_shapes=[pltpu.VMEM((n_pad, CONV_HALF), F32), pltpu.VMEM((n_pad, CONV_HALF), F32)],
        compiler_params=_cparams(("arbitrary",)),
        name="conv_rows",
    )(u, w, b.reshape(1, CONV_DIM))


def _conv_ctx_kernel(u_ref, w_ref, b_ref, prev_ref, o_ref, pad_ref, pad4_ref):
    del prev_ref
    _conv_seq_taps(u_ref, w_ref, b_ref, o_ref, pad_ref, pad4_ref, n=ROWS_CTX, cb=LANES)


def _conv_ctx(u, w, b, prev):
    blk0 = ROWS_LAT // ROWS_CTX
    n_pad = ROWS_CTX + 2 * CONV_ROW_PAD
    return pl.pallas_call(
        _conv_ctx_kernel,
        grid=(CONV_DIM // LANES,),
        in_specs=[
            pl.BlockSpec((ROWS_CTX, LANES), lambda c: (blk0, c)),
            pl.BlockSpec((CONV_K, LANES), lambda c: (0, c)),
            pl.BlockSpec((1, LANES), lambda c: (0, c)),
            pl.BlockSpec(memory_space=pl.ANY),
        ],
        out_specs=pl.BlockSpec((ROWS_CTX, LANES), lambda c: (blk0, c)),
        out_shape=jax.ShapeDtypeStruct((ROWS, CONV_DIM), F32),
        scratch_shapes=[pltpu.VMEM((n_pad, LANES), F32), pltpu.VMEM((n_pad, LANES), F32)],
        input_output_aliases={3: 0},
        compiler_params=_cparams(("arbitrary",)),
        name="conv_ctx",
    )(u, w, b.reshape(1, CONV_DIM), prev)


def _conv_vert_kernel(u_ref, w_ref, b_ref, prev_ref, o_ref, pad_ref):
    del prev_ref
    rpg = ROWS_PER_GRID_ROW
    zeros = jnp.zeros((CONV_PAD * rpg, LANES), F32)
    pad_ref[0:CONV_PAD * rpg, :] = zeros
    pad_ref[(CONV_PAD + GRID_ROWS) * rpg:, :] = zeros
    pad_ref[CONV_PAD * rpg:(CONV_PAD + GRID_ROWS) * rpg, :] = u_ref[...]
    bias = jnp.broadcast_to(b_ref[...], (CONV_SUB_ROWS, LANES))

    def body(r, carry):
        for s in range(rpg // CONV_SUB_ROWS):
            acc = bias
            for k in range(CONV_K):
                off = pl.multiple_of((r + k) * rpg + s * CONV_SUB_ROWS, CONV_SUB_ROWS)
                acc = acc + w_ref[k:k + 1, :] * pad_ref[pl.ds(off, CONV_SUB_ROWS), :]
            o0 = pl.multiple_of(r * rpg + s * CONV_SUB_ROWS, CONV_SUB_ROWS)
            o_ref[pl.ds(o0, CONV_SUB_ROWS), :] = acc
        return carry

    lax.fori_loop(0, GRID_ROWS, body, 0)


def _conv_vert(u, w, b, prev):
    half_blk = CONV_HALF // LANES
    return pl.pallas_call(
        _conv_vert_kernel,
        grid=(half_blk,),
        in_specs=[
            pl.BlockSpec((ROWS_LAT, LANES), lambda c: (0, c + half_blk)),
            pl.BlockSpec((CONV_K, LANES), lambda c: (0, c + half_blk)),
            pl.BlockSpec((1, LANES), lambda c: (0, c + half_blk)),
            pl.BlockSpec(memory_space=pl.ANY),
        ],
        out_specs=pl.BlockSpec((ROWS_LAT, LANES), lambda c: (0, c + half_blk)),
        out_shape=jax.ShapeDtypeStruct(prev.shape, F32),
        scratch_shapes=[pltpu.VMEM(((GRID_ROWS + 2 * CONV_PAD) * ROWS_PER_GRID_ROW, LANES), F32)],
        input_output_aliases={3: 0},
        compiler_params=_cparams(("arbitrary",)),
        name="conv_vert",
    )(u, w, b.reshape(1, CONV_DIM), prev)


def _ssm_kernel(uf_ref, ub_ref, b2_ref, c2_ref, lam_ref, yf_ref, yb_ref,
                lhs_ref, s_ref, y2_ref, hst_ref):
    k = pl.program_id(0)

    @pl.when(k == 0)
    def _():
        hst_ref[...] = jnp.zeros_like(hst_ref)

    n_pair = SSM_CHUNK_T // 2
    low = lax.broadcasted_iota(jnp.int32, (SUBLANES, SSM_DIM), 0) < BATCH
    zero = jnp.zeros((SUBLANES, SSM_DIM), F32)
    for m in range(n_pair):
        f = uf_ref[m * SUBLANES:(m + 1) * SUBLANES, :]
        mb = n_pair - 1 - m
        g = ub_ref[mb * SUBLANES:(mb + 1) * SUBLANES, :]
        fr = pltpu.roll(f, BATCH, 0)
        gr = pltpu.roll(g, BATCH, 0)
        fwd = jnp.concatenate([jnp.where(low, f, zero), jnp.where(low, fr, zero)], axis=0).astype(BF16)
        bwd = jnp.concatenate([jnp.where(low, zero, g), jnp.where(low, zero, gr)], axis=0).astype(BF16)
        r0 = m * 2 * SUBLANES
        for j in range(N_SLAB):
            cs = slice(j * LANES, (j + 1) * LANES)
            lhs_ref[r0:r0 + 2 * SUBLANES, 2 * j * LANES:(2 * j + 1) * LANES] = fwd[:, cs]
            lhs_ref[r0:r0 + 2 * SUBLANES, (2 * j + 1) * LANES:(2 * j + 2) * LANES] = bwd[:, cs]

    re = slice(0, LANES)
    im = slice(LANES, 2 * LANES)

    def project(j):
        for gp in range(j * GP_PER_SLAB, (j + 1) * GP_PER_SLAB):
            s_ref[gp] = jnp.dot(lhs_ref[:, 2 * j * LANES:(2 * j + 2) * LANES], b2_ref[gp],
                                preferred_element_type=F32)

    def scan(j):
        for gp in range(j * GP_PER_SLAB, (j + 1) * GP_PER_SLAB):
            lr, li = lam_ref[gp, 0], lam_ref[gp, 1]
            hre, him = hst_ref[gp, 0], hst_ref[gp, 1]
            for s in range(SSM_CHUNK_T):
                rows = slice(s * SUBLANES, (s + 1) * SUBLANES)
                nre = lr * hre - li * him + s_ref[gp, rows, re]
                nim = lr * him + li * hre + s_ref[gp, rows, im]
                s_ref[gp, rows, re] = nre
                s_ref[gp, rows, im] = nim
                hre, him = nre, nim
            hst_ref[gp, 0] = hre
            hst_ref[gp, 1] = him

    def read_out(j):
        acc = None
        for gp in range(j * GP_PER_SLAB, (j + 1) * GP_PER_SLAB):
            p = jnp.dot(s_ref[gp].astype(BF16), c2_ref[gp], preferred_element_type=F32)
            acc = p if acc is None else acc + p
        y2_ref[:, 2 * j * LANES:(2 * j + 2) * LANES] = acc

    project(0)
    for j in range(N_SLAB):
        if j + 1 < N_SLAB:
            project(j + 1)
        scan(j)
        if j >= 1:
            read_out(j - 1)
    read_out(N_SLAB - 1)

    low1 = lax.broadcasted_iota(jnp.int32, (SUBLANES, LANES), 0) < BATCH
    for m in range(n_pair):
        mb = n_pair - 1 - m
        r0 = m * 2 * SUBLANES
        for j in range(N_SLAB):
            cf = slice(2 * j * LANES, (2 * j + 1) * LANES)
            cb = slice((2 * j + 1) * LANES, (2 * j + 2) * LANES)
            cs = slice(j * LANES, (j + 1) * LANES)
            ef = y2_ref[r0:r0 + SUBLANES, cf]
            of = pltpu.roll(y2_ref[r0 + SUBLANES:r0 + 2 * SUBLANES, cf], BATCH, 0)
            yf_ref[m * SUBLANES:(m + 1) * SUBLANES, cs] = jnp.where(low1, ef, of)
            eb = y2_ref[r0:r0 + SUBLANES, cb]
            ob = pltpu.roll(y2_ref[r0 + SUBLANES:r0 + 2 * SUBLANES, cb], BATCH, 0)
            yb_ref[mb * SUBLANES:(mb + 1) * SUBLANES, cs] = jnp.where(low1, ob, eb)


def _ssm(us, b2, c2, lamtab, layer):
    n_lat = N_CHUNK - N_CHUNK_CTX

    def fwd_blk(k):
        return jnp.where(k < N_CHUNK_CTX, n_lat + k, k - N_CHUNK_CTX)

    def bwd_blk(k):
        return N_CHUNK - 1 - k

    blk = (SSM_CHUNK_ROWS, SSM_DIM)
    return pl.pallas_call(
        _ssm_kernel,
        grid=(N_CHUNK,),
        in_specs=[
            pl.BlockSpec(blk, lambda k: (fwd_blk(k), 0)),
            pl.BlockSpec(blk, lambda k: (bwd_blk(k), 0)),
            _layer_spec(b2.shape, layer),
            _layer_spec(c2.shape, layer),
            _layer_spec(lamtab.shape, layer),
        ],
        out_specs=[
            pl.BlockSpec(blk, lambda k: (fwd_blk(k), 0)),
            pl.BlockSpec(blk, lambda k: (bwd_blk(k), 0)),
        ],
        out_shape=[jax.ShapeDtypeStruct((ROWS, SSM_DIM), F32)] * 2,
        scratch_shapes=[
            pltpu.VMEM((SSM_STEP_ROWS, 2 * SSM_DIM), BF16),
            pltpu.VMEM((N_GP, SSM_STEP_ROWS, 2 * LANES), F32),
            pltpu.VMEM((SSM_STEP_ROWS, 2 * SSM_DIM), F32),
            pltpu.VMEM((N_GP, 2, SUBLANES, LANES), F32),
        ],
        compiler_params=_cparams(("arbitrary",)),
        name="s5_scan",
    )(us, us, b2, c2, lamtab)


def _ssm_tables(a_re, a_im, log_dt, b_re, b_im, c_re, c_im):
    lam = lax.complex(a_re.astype(F32), a_im.astype(F32))
    dt = jnp.exp(log_dt.astype(F32))[..., None]
    lam_bar = jnp.exp(lam * dt)
    bmat_c = lax.complex(b_re.astype(F32), b_im.astype(F32))
    b_bar = ((lam_bar - 1) / lam)[..., None] * bmat_c

    def pair_rows(v):
        v = v.reshape(2, N_GP, 2, SSM_GROUP, SSM_STATE).transpose(1, 0, 2, 3, 4)
        z = jnp.zeros_like(v[:, :, 0])
        q0 = jnp.concatenate([v[:, :, 0], z], axis=-1)
        q1 = jnp.concatenate([z, v[:, :, 1]], axis=-1)
        return jnp.stack([q0, q1], axis=2)

    slab_pos = (jnp.arange(N_GP) % GP_PER_SLAB)[:, None] == jnp.arange(GP_PER_SLAB)[None, :]
    place = slab_pos.astype(F32)[:, None, :, None, None, None]

    def place_rows(re_v, im_v):
        t = jnp.concatenate([pair_rows(re_v), pair_rows(im_v)], axis=-1)
        t = t[:, :, None] * place
        return t.reshape(N_GP, 2 * LANES, 2 * LANES)

    bt = jnp.swapaxes(b_bar, -1, -2)
    b2 = place_rows(jnp.real(bt), jnp.imag(bt))
    c2 = jnp.swapaxes(place_rows(c_re.astype(F32), -c_im.astype(F32)), 1, 2)
    lam_ri = jnp.stack([jnp.real(lam_bar), jnp.imag(lam_bar)], axis=0)
    lam_ri = lam_ri.reshape(2, 2, N_GP, 1, LANES).transpose(2, 0, 1, 3, 4)
    lamtab = jnp.broadcast_to(lam_ri, (N_GP, 2, 2, BATCH, LANES)).reshape(N_GP, 2, SUBLANES, LANES)
    return b2.astype(BF16), c2.astype(BF16), lamtab


MIX_COLS = 512


def _mix_kernel(cv_ref, yf_ref, yb_ref, us_ref, gt_ref, lng_ref, lnb_ref, sd_ref,
                cwo_ref, wglu_ref, bglu_ref, o_ref):
    cv = cv_ref[...]
    mu = jnp.mean(cv, axis=-1, keepdims=True)
    var = jnp.mean(jnp.square(cv - mu), axis=-1, keepdims=True)
    ln = (cv - mu) * lax.rsqrt(var + EPS) * lng_ref[...] + lnb_ref[...]
    s_conv = jax.nn.silu(ln).astype(BF16)
    y = yf_ref[...] + yb_ref[...] + sd_ref[...] * us_ref[...]
    s_ssm = jax.nn.gelu(y).astype(BF16)
    for c in range(D_MODEL // MIX_COLS):
        lo = c * MIX_COLS
        ca = slice(lo, lo + MIX_COLS)
        cg = slice(D_MODEL + lo, D_MODEL + lo + MIX_COLS)
        y_conv = jnp.dot(s_conv, cwo_ref[:, ca], preferred_element_type=F32)
        za = jnp.dot(s_ssm, wglu_ref[:, ca], preferred_element_type=F32) + bglu_ref[:, ca]
        zg = jnp.dot(s_ssm, wglu_ref[:, cg], preferred_element_type=F32) + bglu_ref[:, cg]
        y_ssm = za * jax.nn.sigmoid(zg)
        mixed = gt_ref[:, ca].astype(F32) * y_conv + gt_ref[:, cg].astype(F32) * y_ssm
        o_ref[:, ca] = mixed.astype(BF16)


def _mix(cv, yf, yb, us, gt, ln_g, ln_b, ssm_d, cwo, wglu, bglu, layer, *, latent_only):
    tm = MIX_TILE_ROWS
    n_tiles = (ROWS_LAT if latent_only else ROWS) // tm
    row = lambda w: pl.BlockSpec((tm, w), lambda i: (i, 0))
    vec = lambda w: pl.BlockSpec((1, w), lambda i: (0, 0), pipeline_mode=pl.Buffered(1))
    wgt = lambda r, c: pl.BlockSpec((None, r, c), lambda i: (layer, 0, 0), pipeline_mode=pl.Buffered(1))
    return pl.pallas_call(
        _mix_kernel,
        grid=(n_tiles,),
        in_specs=[
            row(CONV_DIM), row(SSM_DIM), row(SSM_DIM), row(SSM_DIM), row(2 * D_MODEL),
            vec(CONV_DIM), vec(CONV_DIM), vec(SSM_DIM),
            wgt(CONV_DIM, D_MODEL), wgt(SSM_DIM, 2 * D_MODEL), vec(2 * D_MODEL),
        ],
        out_specs=row(D_MODEL),
        out_shape=jax.ShapeDtypeStruct((n_tiles * tm, D_MODEL), BF16),
        compiler_params=_cparams(("arbitrary",)),
        name="mixer_mix",
    )(cv, yf, yb, us, gt, ln_g.reshape(1, -1), ln_b.reshape(1, -1), ssm_d.reshape(1, -1),
      cwo, wglu, bglu.reshape(1, -1))


def _outproj_kernel(x_ref, mix_ref, mod_ref, w_ref, o_ref):
    y = jnp.dot(mix_ref[...], w_ref[...], preferred_element_type=F32)
    o_ref[...] = x_ref[...] + _rows8(y, mod_ref[2], jnp.multiply)


def _outproj(x, mix, modtab, w_out, layer, *, latent_only):
    tm = MIX_TILE_ROWS
    n_lat_tiles = ROWS_LAT // tm
    n_tiles = n_lat_tiles if latent_only else ROWS // tm
    return pl.pallas_call(
        _outproj_kernel,
        grid=(n_tiles,),
        in_specs=[
            pl.BlockSpec((tm, D_MODEL), lambda i: (i, 0)),
            pl.BlockSpec((tm, D_MODEL), lambda i: (i, 0)),
            pl.BlockSpec((3, SUBLANES, D_MODEL),
                         lambda i: (1, jnp.where(i < n_lat_tiles, 0, 1), 0)),
            pl.BlockSpec((None, D_MODEL, D_MODEL), lambda i: (layer, 0, 0)),
        ],
        out_specs=pl.BlockSpec((tm, D_MODEL), lambda i: (i, 0)),
        out_shape=jax.ShapeDtypeStruct((ROWS, D_MODEL), F32),
        input_output_aliases={0: 0},
        compiler_params=_cparams(("arbitrary",)),
        name="mixer_outproj",
    )(x, mix, modtab, w_out)


def kernel(x, c, ctx, c_ctx, ada_w, ada_b, norm_ffn1, ffn1_w_up, ffn1_w_down, norm_mix, w_in, b_in, conv_dw, conv_db, conv_ln_g, conv_ln_b, conv_w_out, ssm_a_re, ssm_a_im, ssm_log_dt, ssm_b_re, ssm_b_im, ssm_c_re, ssm_c_im, ssm_d, ssm_w_glu, ssm_b_glu, w_out, norm_ffn2, ffn2_w_up, ffn2_w_down, norm_final):
    xs = _to_rows(x, ctx)

    lat_rows = jnp.arange(SUBLANES) % BATCH
    c_rows = jnp.concatenate([c[lat_rows], jnp.broadcast_to(c_ctx, (SUBLANES, D_MODEL))], axis=0)
    modtab = _ada(c_rows, ada_w, ada_b)
    b2, c2, lamtab = jax.vmap(_ssm_tables)(ssm_a_re, ssm_a_im, ssm_log_dt,
                                            ssm_b_re, ssm_b_im, ssm_c_re, ssm_c_im)

    cwo, wglu, wout = conv_w_out.astype(BF16), ssm_w_glu.astype(BF16), w_out.astype(BF16)

    for l in range(DEPTH):
        last = l == DEPTH - 1
        mt = modtab[l]
        xs = _ffn(xs, mt, 0, norm_ffn1[l], ffn1_w_up, ffn1_w_down, l)
        uc, us, gt = _inproj(xs, mt, norm_mix[l], w_in, b_in[l], l)

        cv = _conv_rows(uc, conv_dw[l], conv_db[l], rows_out=ROWS_LAT if last else ROWS)
        cv = _conv_vert(uc, conv_dw[l], conv_db[l], cv)
        if not last:
            cv = _conv_ctx(uc, conv_dw[l], conv_db[l], cv)

        yf, yb = _ssm(us, b2, c2, lamtab, l)

        mix = _mix(cv, yf, yb, us, gt, conv_ln_g[l], conv_ln_b[l], ssm_d[l], cwo, wglu, ssm_b_glu[l], l,
                   latent_only=last)
        xs = _outproj(xs, mix, mt, wout, l, latent_only=last)
        xs = _ffn(xs, mt, 2, norm_ffn2[l], ffn2_w_up, ffn2_w_down, l, latent_only=last)

    return _final_norm(xs, norm_final)
```

```python
import functools

import jax
import jax.numpy as jnp
from jax import lax
from jax.experimental import pallas as pl
from jax.experimental.pallas import tpu as pltpu

F32 = jnp.float32
BF16 = jnp.bfloat16

D_MODEL = 2048
BATCH = 4
SEQ = 2048
CTX_LEN = 256
DEPTH = 2
GRID_W = 64
GRID_ROWS = SEQ // GRID_W
D_FF = 5632
CONV_DIM = 1024
CONV_K = 31
CONV_PAD = CONV_K // 2
SSM_DIM = 1024
SSM_GROUP = 16
SSM_GROUPS = 64
SSM_STATE = 64
N_MOD = 9
N_IN = 2 * CONV_DIM + SSM_DIM + 2 * D_MODEL
FFN_RES = 0.5
EPS = 1e-6

LANES = 128
SUBLANES = 8
ROWS_CTX = CTX_LEN * BATCH
ROWS_LAT = SEQ * BATCH
ROWS = ROWS_CTX + ROWS_LAT
ROWS_PER_GRID_ROW = GRID_W * BATCH

VMEM_LIMIT = 60 * 1024 * 1024

STREAM_TILE_ROWS = 1024
FFN_CHUNK = 512
PROJ_COLS = 1024
MIX_TILE_ROWS = 512
RELAYOUT_TILE_T = 128

GROUPS_PER_SLAB = LANES // SSM_GROUP
N_GP = SSM_GROUPS // 2
GP_PER_SLAB = GROUPS_PER_SLAB // 2
N_SLAB = SSM_DIM // LANES
SSM_CHUNK_T = 64
SSM_CHUNK_ROWS = SSM_CHUNK_T * BATCH
SSM_STEP_ROWS = 2 * SSM_CHUNK_ROWS
N_CHUNK_CTX = CTX_LEN // SSM_CHUNK_T
N_CHUNK = (CTX_LEN + SEQ) // SSM_CHUNK_T
assert 2 * BATCH == SUBLANES


def _cparams(sem):
    return pltpu.CompilerParams(dimension_semantics=sem, vmem_limit_bytes=VMEM_LIMIT)


def _layer_spec(shape, layer):
    nd = len(shape) - 1
    return pl.BlockSpec((None,) + tuple(shape[1:]), lambda *_: (layer,) + (0,) * nd,
                        pipeline_mode=pl.Buffered(1))


def _mxu_operand(w_ref, rows=slice(None), cols=slice(None)):
    w = w_ref[rows, cols]
    return w if w.dtype == BF16 else w.astype(BF16)


def _rows8(x, m, op):
    n, d = x.shape
    x3 = x.reshape(n // SUBLANES, SUBLANES, d)
    return op(x3, m[None]).reshape(n, d)


def _row_tile_copy(x_hbm, xbuf_ref, sem, tile, tm):
    rows = pl.ds(pl.multiple_of(tile * tm, tm), tm)
    return pltpu.make_async_copy(x_hbm.at[rows, :], xbuf_ref, sem)


def _interleave_batch(src_ref, o_ref, slab_ref, tq):
    for c in range(D_MODEL // LANES):
        cs = slice(c * LANES, (c + 1) * LANES)
        for b in range(BATCH):
            slab_ref[c, pl.ds(b, tq, stride=BATCH), :] = src_ref[b, :, cs]
        o_ref[:, cs] = slab_ref[c]


def _to_rows_kernel(x_ref, c_ref, o_ref, slab_ref, *, n_lat_tiles, tq):
    i = pl.program_id(0)

    @pl.when(i < n_lat_tiles)
    def _():
        _interleave_batch(x_ref, o_ref, slab_ref, tq)

    @pl.when(i >= n_lat_tiles)
    def _():
        _interleave_batch(c_ref, o_ref, slab_ref, tq)


def _to_rows(x, ctx):
    tq = RELAYOUT_TILE_T
    n_lat_tiles = SEQ // tq
    n_tiles = (SEQ + CTX_LEN) // tq
    return pl.pallas_call(
        functools.partial(_to_rows_kernel, n_lat_tiles=n_lat_tiles, tq=tq),
        grid=(n_tiles,),
        in_specs=[
            pl.BlockSpec((BATCH, tq, D_MODEL), lambda i: (0, jnp.minimum(i, n_lat_tiles - 1), 0)),
            pl.BlockSpec((BATCH, tq, D_MODEL), lambda i: (0, jnp.maximum(i - n_lat_tiles, 0), 0)),
        ],
        out_specs=pl.BlockSpec((tq * BATCH, D_MODEL), lambda i: (i, 0)),
        out_shape=jax.ShapeDtypeStruct((ROWS, D_MODEL), F32),
        scratch_shapes=[pltpu.VMEM((D_MODEL // LANES, tq * BATCH, LANES), F32)],
        compiler_params=_cparams(("arbitrary",)),
        name="to_rows",
    )(x, ctx)


def _final_norm_kernel(x_ref, gain_ref, o_ref, slab_ref, *, tq):
    x = x_ref[...]
    xn = x * lax.rsqrt(jnp.mean(x * x, axis=-1, keepdims=True) + EPS) * gain_ref[...]
    for c in range(D_MODEL // LANES):
        cs = slice(c * LANES, (c + 1) * LANES)
        slab_ref[c] = xn[:, cs]
        for b in range(BATCH):
            o_ref[b, :, cs] = slab_ref[c, pl.ds(b, tq, stride=BATCH), :]


def _final_norm(xs, gain):
    tq = RELAYOUT_TILE_T
    return pl.pallas_call(
        functools.partial(_final_norm_kernel, tq=tq),
        grid=(SEQ // tq,),
        in_specs=[pl.BlockSpec((tq * BATCH, D_MODEL), lambda i: (i, 0)),
                  pl.BlockSpec((1, D_MODEL), lambda i: (0, 0))],
        out_specs=pl.BlockSpec((BATCH, tq, D_MODEL), lambda i: (0, i, 0)),
        out_shape=jax.ShapeDtypeStruct((BATCH, SEQ, D_MODEL), F32),
        scratch_shapes=[pltpu.VMEM((D_MODEL // LANES, tq * BATCH, LANES), F32)],
        compiler_params=_cparams(("arbitrary",)),
        name="final_norm",
    )(xs, gain.reshape(1, D_MODEL))


def _ada_kernel(c_ref, w_ref, b_ref, o_ref):
    s = jax.nn.silu(c_ref[...]).astype(BF16)
    o_ref[...] = jnp.dot(s, w_ref[...].astype(BF16), preferred_element_type=F32) + b_ref[...]


def _ada(c_rows, ada_w, ada_b):
    n_rows = c_rows.shape[0]
    return pl.pallas_call(
        _ada_kernel,
        grid=(DEPTH, N_MOD),
        in_specs=[
            pl.BlockSpec((n_rows, D_MODEL), lambda l, m: (0, 0)),
            pl.BlockSpec((None, D_MODEL, D_MODEL), lambda l, m: (l, 0, m)),
            pl.BlockSpec((None, None, 1, D_MODEL), lambda l, m: (l, m, 0, 0)),
        ],
        out_specs=pl.BlockSpec((None, None, n_rows, D_MODEL), lambda l, m: (l, m, 0, 0)),
        out_shape=jax.ShapeDtypeStruct((DEPTH, N_MOD, n_rows, D_MODEL), F32),
        compiler_params=_cparams(("arbitrary", "arbitrary")),
        name="ada_mod",
    )(c_rows, ada_w, ada_b.reshape(DEPTH, N_MOD, 1, D_MODEL))


def _norm_mod(x, gain_ref, mod_ref):
    y = x * lax.rsqrt(jnp.mean(x * x, axis=-1, keepdims=True) + EPS) * gain_ref[...]
    h = _rows8(y, 1.0 + mod_ref[1], jnp.multiply)
    return _rows8(h, mod_ref[0], jnp.add).astype(BF16)


def _ffn_chunk_halves(hn_ref, wa_ref, wg_ref, wd_ref):
    hn = hn_ref[...]
    half = wa_ref.shape[1] // 2
    halves = (slice(0, half), slice(half, 2 * half))
    acts = []
    for cols in halves:
        a = jnp.dot(hn, _mxu_operand(wa_ref, cols=cols), preferred_element_type=F32)
        g = jnp.dot(hn, _mxu_operand(wg_ref, cols=cols), preferred_element_type=F32)
        acts.append((jax.nn.silu(g) * a).astype(BF16))
    for act, rows in zip(acts, halves):
        yield jnp.dot(act, _mxu_operand(wd_ref, rows=rows), preferred_element_type=F32)


def _ffn_kernel(x_hbm, mod_ref, gain_ref, wa_ref, wg_ref, wd_ref, o_ref, hn_ref, xbuf_ref, sem,
                *, tm, n_tiles):
    i = pl.program_id(0)
    j = pl.program_id(1)

    x_copy = functools.partial(_row_tile_copy, x_hbm, xbuf_ref, sem, tm=tm)

    def accumulate_chunk(base=None):
        for part in _ffn_chunk_halves(hn_ref, wa_ref, wg_ref, wd_ref):
            gated = _rows8(part, mod_ref[2] * FFN_RES, jnp.multiply)
            o_ref[...] = (o_ref[...] if base is None else base) + gated
            base = None

    @pl.when(j == 0)
    def _():
        @pl.when(i == 0)
        def _():
            x_copy(0).start()

        x_copy(i).wait()
        x = xbuf_ref[...]
        hn_ref[...] = _norm_mod(x, gain_ref, mod_ref)
        accumulate_chunk(base=x)

    @pl.when(j == 1)
    def _():
        @pl.when(i + 1 < n_tiles)
        def _():
            x_copy(i + 1).start()

        accumulate_chunk()

    @pl.when(j > 1)
    def _():
        accumulate_chunk()


def _ffn(x, modtab, sub, gain, w_up, w_down, layer, *, latent_only=False):
    tm, tf = STREAM_TILE_ROWS, FFN_CHUNK
    nf = D_FF // tf
    assert nf >= 2
    n_lat_tiles = ROWS_LAT // tm
    n_tiles = n_lat_tiles if latent_only else ROWS // tm
    return pl.pallas_call(
        functools.partial(_ffn_kernel, tm=tm, n_tiles=n_tiles),
        grid=(n_tiles, nf),
        in_specs=[
            pl.BlockSpec(memory_space=pl.ANY),
            pl.BlockSpec((3, SUBLANES, D_MODEL),
                         lambda i, j: (sub, jnp.where(i < n_lat_tiles, 0, 1), 0)),
            pl.BlockSpec((1, D_MODEL), lambda i, j: (0, 0)),
            pl.BlockSpec((None, D_MODEL, tf), lambda i, j: (layer, 0, j)),
            pl.BlockSpec((None, D_MODEL, tf), lambda i, j: (layer, 0, j + nf)),
            pl.BlockSpec((None, tf, D_MODEL), lambda i, j: (layer, j, 0)),
        ],
        out_specs=pl.BlockSpec((tm, D_MODEL), lambda i, j: (i, 0)),
        out_shape=jax.ShapeDtypeStruct((n_tiles * tm, D_MODEL), F32),
        scratch_shapes=[pltpu.VMEM((tm, D_MODEL), BF16), pltpu.VMEM((tm, D_MODEL), F32),
                        pltpu.SemaphoreType.DMA(())],
        compiler_params=_cparams(("arbitrary", "arbitrary")),
        name="ffn",
    )(x, modtab, gain.reshape(1, D_MODEL), w_up, w_up, w_down)


def _inproj_kernel(x_hbm, mod_ref, gain_ref, w_ref, b_ref, uc_ref, us_ref, gt_ref, hn_ref, xbuf_ref, sem,
                   *, tm, n_tiles):
    i = pl.program_id(0)
    j = pl.program_id(1)
    x_copy = functools.partial(_row_tile_copy, x_hbm, xbuf_ref, sem, tm=tm)

    def proj():
        return jnp.dot(hn_ref[...], _mxu_operand(w_ref), preferred_element_type=F32) + b_ref[0]

    @pl.when(j == 0)
    def _():
        @pl.when(i == 0)
        def _():
            x_copy(0).start()

        x_copy(i).wait()
        hn_ref[...] = _norm_mod(xbuf_ref[...], gain_ref, mod_ref)
        uc_ref[...] = proj()

    @pl.when(j == 1)
    def _():
        @pl.when(i + 1 < n_tiles)
        def _():
            x_copy(i + 1).start()

        uc_ref[...] = uc_ref[...] * jax.nn.sigmoid(proj())

    @pl.when(j == 2)
    def _():
        us_ref[...] = proj()

    @pl.when(j >= 3)
    def _():
        gt_ref[...] = jax.nn.sigmoid(proj()).astype(BF16)


def _inproj(x, modtab, gain, w_in, b_in, layer):
    tm = STREAM_TILE_ROWS
    tn = PROJ_COLS
    n_lat_tiles = ROWS_LAT // tm
    nblk = N_IN // tn
    n_head = nblk - 2 * D_MODEL // tn
    n_tiles = ROWS // tm
    return pl.pallas_call(
        functools.partial(_inproj_kernel, tm=tm, n_tiles=n_tiles),
        grid=(n_tiles, nblk),
        in_specs=[
            pl.BlockSpec(memory_space=pl.ANY),
            pl.BlockSpec((3, SUBLANES, D_MODEL),
                         lambda i, j: (1, jnp.where(i < n_lat_tiles, 0, 1), 0)),
            pl.BlockSpec((1, D_MODEL), lambda i, j: (0, 0)),
            pl.BlockSpec((None, D_MODEL, tn), lambda i, j: (layer, 0, j)),
            pl.BlockSpec((1, 1, tn), lambda i, j: (j, 0, 0)),
        ],
        out_specs=[
            pl.BlockSpec((tm, tn), lambda i, j: (i, 0)),
            pl.BlockSpec((tm, tn), lambda i, j: (i, 0)),
            pl.BlockSpec((tm, tn), lambda i, j: (i, jnp.maximum(j - n_head, 0))),
        ],
        out_shape=[
            jax.ShapeDtypeStruct((ROWS, CONV_DIM), F32),
            jax.ShapeDtypeStruct((ROWS, SSM_DIM), F32),
            jax.ShapeDtypeStruct((ROWS, 2 * D_MODEL), BF16),
        ],
        scratch_shapes=[pltpu.VMEM((tm, D_MODEL), BF16), pltpu.VMEM((tm, D_MODEL), F32),
                        pltpu.SemaphoreType.DMA(())],
        compiler_params=_cparams(("arbitrary", "arbitrary")),
        name="mixer_inproj",
    )(x, modtab, gain.reshape(1, D_MODEL), w_in, b_in.reshape(nblk, 1, tn))


CONV_ROW_PAD = 64
CONV_SUB_ROWS = 64
CONV_HALF = CONV_DIM // 2


def _conv_seq_taps(u_ref, w_ref, b_ref, o_ref, pad_ref, pad4_ref, *, n, cb):
    zeros = jnp.zeros((CONV_ROW_PAD, cb), F32)
    pad_ref[0:CONV_ROW_PAD, :] = zeros
    pad_ref[CONV_ROW_PAD + n:, :] = zeros
    pad_ref[CONV_ROW_PAD:CONV_ROW_PAD + n, :] = u_ref[...]
    n_pad = n + 2 * CONV_ROW_PAD
    pad4_ref[0:n_pad - SUBLANES, :] = pad_ref[BATCH:n_pad - SUBLANES + BATCH, :]
    base = CONV_ROW_PAD - CONV_PAD * BATCH
    for c in range(cb // LANES):
        cs = slice(c * LANES, (c + 1) * LANES)
        for r in range(n // CONV_SUB_ROWS):
            r0 = r * CONV_SUB_ROWS
            acc = jnp.broadcast_to(b_ref[:, cs], (CONV_SUB_ROWS, LANES))
            for k in range(CONV_K):
                off = base + r0 + k * BATCH
                if off % SUBLANES == 0:
                    tap = pad_ref[off:off + CONV_SUB_ROWS, cs]
                else:
                    tap = pad4_ref[off - BATCH:off - BATCH + CONV_SUB_ROWS, cs]
                acc = acc + w_ref[k:k + 1, cs] * tap
            o_ref[r0:r0 + CONV_SUB_ROWS, cs] = acc


def _conv_rows_kernel(u_ref, w_ref, b_ref, o_ref, pad_ref, pad4_ref):
    i = pl.program_id(0)
    o_ref[:, CONV_HALF:] = jnp.zeros((ROWS_PER_GRID_ROW, CONV_DIM - CONV_HALF), F32)

    @pl.when(i < GRID_ROWS)
    def _():
        _conv_seq_taps(u_ref, w_ref, b_ref, o_ref, pad_ref, pad4_ref, n=ROWS_PER_GRID_ROW, cb=CONV_HALF)

    @pl.when(i >= GRID_ROWS)
    def _():
        o_ref[:, :CONV_HALF] = jnp.zeros((ROWS_PER_GRID_ROW, CONV_HALF), F32)


def _conv_rows(u, w, b, *, rows_out):
    n = ROWS_PER_GRID_ROW
    n_pad = n + 2 * CONV_ROW_PAD
    return pl.pallas_call(
        _conv_rows_kernel,
        grid=(rows_out // n,),
        in_specs=[
            pl.BlockSpec((n, CONV_HALF), lambda i: (jnp.minimum(i, GRID_ROWS - 1), 0)),
            pl.BlockSpec((CONV_K, CONV_HALF), lambda i: (0, 0)),
            pl.BlockSpec((1, CONV_HALF), lambda i: (0, 0)),
        ],
        out_specs=pl.BlockSpec((n, CONV_DIM), lambda i: (i, 0)),
        out_shape=jax.ShapeDtypeStruct((rows_out, CONV_DIM), F32),
        scratch_shapes=[pltpu.VMEM((n_pad, CONV_HALF), F32), pltpu.VMEM((n_pad, CONV_HALF), F32)],
        compiler_params=_cparams(("arbitrary",)),
        name="conv_rows",
    )(u, w, b.reshape(1, CONV_DIM))


def _conv_ctx_kernel(u_ref, w_ref, b_ref, prev_ref, o_ref, pad_ref, pad4_ref):
    del prev_ref
    _conv_seq_taps(u_ref, w_ref, b_ref, o_ref, pad_ref, pad4_ref, n=ROWS_CTX, cb=LANES)


def _conv_ctx(u, w, b, prev):
    blk0 = ROWS_LAT // ROWS_CTX
    n_pad = ROWS_CTX + 2 * CONV_ROW_PAD
    return pl.pallas_call(
        _conv_ctx_kernel,
        grid=(CONV_DIM // LANES,),
        in_specs=[
            pl.BlockSpec((ROWS_CTX, LANES), lambda c: (blk0, c)),
            pl.BlockSpec((CONV_K, LANES), lambda c: (0, c)),
            pl.BlockSpec((1, LANES), lambda c: (0, c)),
            pl.BlockSpec(memory_space=pl.ANY),
        ],
        out_specs=pl.BlockSpec((ROWS_CTX, LANES), lambda c: (blk0, c)),
        out_shape=jax.ShapeDtypeStruct((ROWS, CONV_DIM), F32),
        scratch_shapes=[pltpu.VMEM((n_pad, LANES), F32), pltpu.VMEM((n_pad, LANES), F32)],
        input_output_aliases={3: 0},
        compiler_params=_cparams(("arbitrary",)),
        name="conv_ctx",
    )(u, w, b.reshape(1, CONV_DIM), prev)


def _conv_vert_kernel(u_ref, w_ref, b_ref, prev_ref, o_ref, pad_ref):
    del prev_ref
    rpg = ROWS_PER_GRID_ROW
    zeros = jnp.zeros((CONV_PAD * rpg, LANES), F32)
    pad_ref[0:CONV_PAD * rpg, :] = zeros
    pad_ref[(CONV_PAD + GRID_ROWS) * rpg:, :] = zeros
    pad_ref[CONV_PAD * rpg:(CONV_PAD + GRID_ROWS) * rpg, :] = u_ref[...]
    bias = jnp.broadcast_to(b_ref[...], (CONV_SUB_ROWS, LANES))

    def body(r, carry):
        for s in range(rpg // CONV_SUB_ROWS):
            acc = bias
            for k in range(CONV_K):
                off = pl.multiple_of((r + k) * rpg + s * CONV_SUB_ROWS, CONV_SUB_ROWS)
                acc = acc + w_ref[k:k + 1, :] * pad_ref[pl.ds(off, CONV_SUB_ROWS), :]
            o0 = pl.multiple_of(r * rpg + s * CONV_SUB_ROWS, CONV_SUB_ROWS)
            o_ref[pl.ds(o0, CONV_SUB_ROWS), :] = acc
        return carry

    lax.fori_loop(0, GRID_ROWS, body, 0)


def _conv_vert(u, w, b, prev):
    half_blk = CONV_HALF // LANES
    return pl.pallas_call(
        _conv_vert_kernel,
        grid=(half_blk,),
        in_specs=[
            pl.BlockSpec((ROWS_LAT, LANES), lambda c: (0, c + half_blk)),
            pl.BlockSpec((CONV_K, LANES), lambda c: (0, c + half_blk)),
            pl.BlockSpec((1, LANES), lambda c: (0, c + half_blk)),
            pl.BlockSpec(memory_space=pl.ANY),
        ],
        out_specs=pl.BlockSpec((ROWS_LAT, LANES), lambda c: (0, c + half_blk)),
        out_shape=jax.ShapeDtypeStruct(prev.shape, F32),
        scratch_shapes=[pltpu.VMEM(((GRID_ROWS + 2 * CONV_PAD) * ROWS_PER_GRID_ROW, LANES), F32)],
        input_output_aliases={3: 0},
        compiler_params=_cparams(("arbitrary",)),
        name="conv_vert",
    )(u, w, b.reshape(1, CONV_DIM), prev)


def _ssm_kernel(uf_ref, ub_ref, b2_ref, c2_ref, lam_ref, yf_ref, yb_ref,
                lhs_ref, s_ref, y2_ref, hst_ref):
    k = pl.program_id(0)

    @pl.when(k == 0)
    def _():
        hst_ref[...] = jnp.zeros_like(hst_ref)

    n_pair = SSM_CHUNK_T // 2
    low = lax.broadcasted_iota(jnp.int32, (SUBLANES, SSM_DIM), 0) < BATCH
    zero = jnp.zeros((SUBLANES, SSM_DIM), F32)
    for m in range(n_pair):
        f = uf_ref[m * SUBLANES:(m + 1) * SUBLANES, :]
        mb = n_pair - 1 - m
        g = ub_ref[mb * SUBLANES:(mb + 1) * SUBLANES, :]
        fr = pltpu.roll(f, BATCH, 0)
        gr = pltpu.roll(g, BATCH, 0)
        fwd = jnp.concatenate([jnp.where(low, f, zero), jnp.where(low, fr, zero)], axis=0).astype(BF16)
        bwd = jnp.concatenate([jnp.where(low, zero, g), jnp.where(low, zero, gr)], axis=0).astype(BF16)
        r0 = m * 2 * SUBLANES
        for j in range(N_SLAB):
            cs = slice(j * LANES, (j + 1) * LANES)
            lhs_ref[r0:r0 + 2 * SUBLANES, 2 * j * LANES:(2 * j + 1) * LANES] = fwd[:, cs]
            lhs_ref[r0:r0 + 2 * SUBLANES, (2 * j + 1) * LANES:(2 * j + 2) * LANES] = bwd[:, cs]

    re = slice(0, LANES)
    im = slice(LANES, 2 * LANES)

    def project(j):
        for gp in range(j * GP_PER_SLAB, (j + 1) * GP_PER_SLAB):
            s_ref[gp] = jnp.dot(lhs_ref[:, 2 * j * LANES:(2 * j + 2) * LANES], b2_ref[gp],
                                preferred_element_type=F32)

    def scan(j):
        for gp in range(j * GP_PER_SLAB, (j + 1) * GP_PER_SLAB):
            lr, li = lam_ref[gp, 0], lam_ref[gp, 1]
            hre, him = hst_ref[gp, 0], hst_ref[gp, 1]
            for s in range(SSM_CHUNK_T):
                rows = slice(s * SUBLANES, (s + 1) * SUBLANES)
                nre = lr * hre - li * him + s_ref[gp, rows, re]
                nim = lr * him + li * hre + s_ref[gp, rows, im]
                s_ref[gp, rows, re] = nre
                s_ref[gp, rows, im] = nim
                hre, him = nre, nim
            hst_ref[gp, 0] = hre
            hst_ref[gp, 1] = him

    def read_out(j):
        acc = None
        for gp in range(j * GP_PER_SLAB, (j + 1) * GP_PER_SLAB):
            p = jnp.dot(s_ref[gp].astype(BF16), c2_ref[gp], preferred_element_type=F32)
            acc = p if acc is None else acc + p
        y2_ref[:, 2 * j * LANES:(2 * j + 2) * LANES] = acc

    project(0)
    for j in range(N_SLAB):
        if j + 1 < N_SLAB:
            project(j + 1)
        scan(j)
        if j >= 1:
            read_out(j - 1)
    read_out(N_SLAB - 1)

    low1 = lax.broadcasted_iota(jnp.int32, (SUBLANES, LANES), 0) < BATCH
    for m in range(n_pair):
        mb = n_pair - 1 - m
        r0 = m * 2 * SUBLANES
        for j in range(N_SLAB):
            cf = slice(2 * j * LANES, (2 * j + 1) * LANES)
            cb = slice((2 * j + 1) * LANES, (2 * j + 2) * LANES)
            cs = slice(j * LANES, (j + 1) * LANES)
            ef = y2_ref[r0:r0 + SUBLANES, cf]
            of = pltpu.roll(y2_ref[r0 + SUBLANES:r0 + 2 * SUBLANES, cf], BATCH, 0)
            yf_ref[m * SUBLANES:(m + 1) * SUBLANES, cs] = jnp.where(low1, ef, of)
            eb = y2_ref[r0:r0 + SUBLANES, cb]
            ob = pltpu.roll(y2_ref[r0 + SUBLANES:r0 + 2 * SUBLANES, cb], BATCH, 0)
            yb_ref[mb * SUBLANES:(mb + 1) * SUBLANES, cs] = jnp.where(low1, ob, eb)


def _ssm(us, b2, c2, lamtab, layer):
    n_lat = N_CHUNK - N_CHUNK_CTX

    def fwd_blk(k):
        return jnp.where(k < N_CHUNK_CTX, n_lat + k, k - N_CHUNK_CTX)

    def bwd_blk(k):
        return N_CHUNK - 1 - k

    blk = (SSM_CHUNK_ROWS, SSM_DIM)
    return pl.pallas_call(
        _ssm_kernel,
        grid=(N_CHUNK,),
        in_specs=[
            pl.BlockSpec(blk, lambda k: (fwd_blk(k), 0)),
            pl.BlockSpec(blk, lambda k: (bwd_blk(k), 0)),
            _layer_spec(b2.shape, layer),
            _layer_spec(c2.shape, layer),
            _layer_spec(lamtab.shape, layer),
        ],
        out_specs=[
            pl.BlockSpec(blk, lambda k: (fwd_blk(k), 0)),
            pl.BlockSpec(blk, lambda k: (bwd_blk(k), 0)),
        ],
        out_shape=[jax.ShapeDtypeStruct((ROWS, SSM_DIM), F32)] * 2,
        scratch_shapes=[
            pltpu.VMEM((SSM_STEP_ROWS, 2 * SSM_DIM), BF16),
            pltpu.VMEM((N_GP, SSM_STEP_ROWS, 2 * LANES), F32),
            pltpu.VMEM((SSM_STEP_ROWS, 2 * SSM_DIM), F32),
            pltpu.VMEM((N_GP, 2, SUBLANES, LANES), F32),
        ],
        compiler_params=_cparams(("arbitrary",)),
        name="s5_scan",
    )(us, us, b2, c2, lamtab)


def _ssm_tables(a_re, a_im, log_dt, b_re, b_im, c_re, c_im):
    lam = lax.complex(a_re.astype(F32), a_im.astype(F32))
    dt = jnp.exp(log_dt.astype(F32))[..., None]
    lam_bar = jnp.exp(lam * dt)
    bmat_c = lax.complex(b_re.astype(F32), b_im.astype(F32))
    b_bar = ((lam_bar - 1) / lam)[..., None] * bmat_c

    def pair_rows(v):
        v = v.reshape(2, N_GP, 2, SSM_GROUP, SSM_STATE).transpose(1, 0, 2, 3, 4)
        z = jnp.zeros_like(v[:, :, 0])
        q0 = jnp.concatenate([v[:, :, 0], z], axis=-1)
        q1 = jnp.concatenate([z, v[:, :, 1]], axis=-1)
        return jnp.stack([q0, q1], axis=2)

    slab_pos = (jnp.arange(N_GP) % GP_PER_SLAB)[:, None] == jnp.arange(GP_PER_SLAB)[None, :]
    place = slab_pos.astype(F32)[:, None, :, None, None, None]

    def place_rows(re_v, im_v):
        t = jnp.concatenate([pair_rows(re_v), pair_rows(im_v)], axis=-1)
        t = t[:, :, None] * place
        return t.reshape(N_GP, 2 * LANES, 2 * LANES)

    bt = jnp.swapaxes(b_bar, -1, -2)
    b2 = place_rows(jnp.real(bt), jnp.imag(bt))
    c2 = jnp.swapaxes(place_rows(c_re.astype(F32), -c_im.astype(F32)), 1, 2)
    lam_ri = jnp.stack([jnp.real(lam_bar), jnp.imag(lam_bar)], axis=0)
    lam_ri = lam_ri.reshape(2, 2, N_GP, 1, LANES).transpose(2, 0, 1, 3, 4)
    lamtab = jnp.broadcast_to(lam_ri, (N_GP, 2, 2, BATCH, LANES)).reshape(N_GP, 2, SUBLANES, LANES)
    return b2.astype(BF16), c2.astype(BF16), lamtab


MIX_COLS = 512


def _mix_kernel(cv_ref, yf_ref, yb_ref, us_ref, gt_ref, lng_ref, lnb_ref, sd_ref,
                cwo_ref, wglu_ref, bglu_ref, o_ref):
    cv = cv_ref[...]
    mu = jnp.mean(cv, axis=-1, keepdims=True)
    var = jnp.mean(jnp.square(cv - mu), axis=-1, keepdims=True)
    ln = (cv - mu) * lax.rsqrt(var + EPS) * lng_ref[...] + lnb_ref[...]
    s_conv = jax.nn.silu(ln).astype(BF16)
    y = yf_ref[...] + yb_ref[...] + sd_ref[...] * us_ref[...]
    s_ssm = jax.nn.gelu(y).astype(BF16)
    for c in range(D_MODEL // MIX_COLS):
        lo = c * MIX_COLS
        ca = slice(lo, lo + MIX_COLS)
        cg = slice(D_MODEL + lo, D_MODEL + lo + MIX_COLS)
        y_conv = jnp.dot(s_conv, cwo_ref[:, ca], preferred_element_type=F32)
        za = jnp.dot(s_ssm, wglu_ref[:, ca], preferred_element_type=F32) + bglu_ref[:, ca]
        zg = jnp.dot(s_ssm, wglu_ref[:, cg], preferred_element_type=F32) + bglu_ref[:, cg]
        y_ssm = za * jax.nn.sigmoid(zg)
        mixed = gt_ref[:, ca].astype(F32) * y_conv + gt_ref[:, cg].astype(F32) * y_ssm
        o_ref[:, ca] = mixed.astype(BF16)


def _mix(cv, yf, yb, us, gt, ln_g, ln_b, ssm_d, cwo, wglu, bglu, layer, *, latent_only):
    tm = MIX_TILE_ROWS
    n_tiles = (ROWS_LAT if latent_only else ROWS) // tm
    row = lambda w: pl.BlockSpec((tm, w), lambda i: (i, 0))
    vec = lambda w: pl.BlockSpec((1, w), lambda i: (0, 0), pipeline_mode=pl.Buffered(1))
    wgt = lambda r, c: pl.BlockSpec((None, r, c), lambda i: (layer, 0, 0), pipeline_mode=pl.Buffered(1))
    return pl.pallas_call(
        _mix_kernel,
        grid=(n_tiles,),
        in_specs=[
            row(CONV_DIM), row(SSM_DIM), row(SSM_DIM), row(SSM_DIM), row(2 * D_MODEL),
            vec(CONV_DIM), vec(CONV_DIM), vec(SSM_DIM),
            wgt(CONV_DIM, D_MODEL), wgt(SSM_DIM, 2 * D_MODEL), vec(2 * D_MODEL),
        ],
        out_specs=row(D_MODEL),
        out_shape=jax.ShapeDtypeStruct((n_tiles * tm, D_MODEL), BF16),
        compiler_params=_cparams(("arbitrary",)),
        name="mixer_mix",
    )(cv, yf, yb, us, gt, ln_g.reshape(1, -1), ln_b.reshape(1, -1), ssm_d.reshape(1, -1),
      cwo, wglu, bglu.reshape(1, -1))


def _outproj_kernel(x_ref, mix_ref, mod_ref, w_ref, o_ref):
    y = jnp.dot(mix_ref[...], w_ref[...], preferred_element_type=F32)
    o_ref[...] = x_ref[...] + _rows8(y, mod_ref[2], jnp.multiply)


def _outproj(x, mix, modtab, w_out, layer, *, latent_only):
    tm = MIX_TILE_ROWS
    n_lat_tiles = ROWS_LAT // tm
    n_tiles = n_lat_tiles if latent_only else ROWS // tm
    return pl.pallas_call(
        _outproj_kernel,
        grid=(n_tiles,),
        in_specs=[
            pl.BlockSpec((tm, D_MODEL), lambda i: (i, 0)),
            pl.BlockSpec((tm, D_MODEL), lambda i: (i, 0)),
            pl.BlockSpec((3, SUBLANES, D_MODEL),
                         lambda i: (1, jnp.where(i < n_lat_tiles, 0, 1), 0)),
            pl.BlockSpec((None, D_MODEL, D_MODEL), lambda i: (layer, 0, 0)),
        ],
        out_specs=pl.BlockSpec((tm, D_MODEL), lambda i: (i, 0)),
        out_shape=jax.ShapeDtypeStruct((ROWS, D_MODEL), F32),
        input_output_aliases={0: 0},
        compiler_params=_cparams(("arbitrary",)),
        name="mixer_outproj",
    )(x, mix, modtab, w_out)


def kernel(x, c, ctx, c_ctx, ada_w, ada_b, norm_ffn1, ffn1_w_up, ffn1_w_down, norm_mix, w_in, b_in, conv_dw, conv_db, conv_ln_g, conv_ln_b, conv_w_out, ssm_a_re, ssm_a_im, ssm_log_dt, ssm_b_re, ssm_b_im, ssm_c_re, ssm_c_im, ssm_d, ssm_w_glu, ssm_b_glu, w_out, norm_ffn2, ffn2_w_up, ffn2_w_down, norm_final):
    xs = _to_rows(x, ctx)

    lat_rows = jnp.arange(SUBLANES) % BATCH
    c_rows = jnp.concatenate([c[lat_rows], jnp.broadcast_to(c_ctx, (SUBLANES, D_MODEL))], axis=0)
    modtab = _ada(c_rows, ada_w, ada_b)
    b2, c2, lamtab = jax.vmap(_ssm_tables)(ssm_a_re, ssm_a_im, ssm_log_dt,
                                            ssm_b_re, ssm_b_im, ssm_c_re, ssm_c_im)

    cwo, wglu, wout = conv_w_out.astype(BF16), ssm_w_glu.astype(BF16), w_out.astype(BF16)

    for l in range(DEPTH):
        last = l == DEPTH - 1
        mt = modtab[l]
        xs = _ffn(xs, mt, 0, norm_ffn1[l], ffn1_w_up, ffn1_w_down, l)
        uc, us, gt = _inproj(xs, mt, norm_mix[l], w_in, b_in[l], l)

        cv = _conv_rows(uc, conv_dw[l], conv_db[l], rows_out=ROWS_LAT if last else ROWS)
        cv = _conv_vert(uc, conv_dw[l], conv_db[l], cv)
        if not last:
            cv = _conv_ctx(uc, conv_dw[l], conv_db[l], cv)

        yf, yb = _ssm(us, b2, c2, lamtab, l)

        mix = _mix(cv, yf, yb, us, gt, conv_ln_g[l], conv_ln_b[l], ssm_d[l], cwo, wglu, ssm_b_glu[l], l,
                   latent_only=last)
        xs = _outproj(xs, mix, mt, wout, l, latent_only=last)
        xs = _ffn(xs, mt, 2, norm_ffn2[l], ffn2_w_up, ffn2_w_down, l, latent_only=last)

    return _final_norm(xs, norm_final)
```

```python
import functools

import jax
import jax.numpy as jnp
from jax import lax
from jax.experimental import pallas as pl
from jax.experimental.pallas import tpu as pltpu

F32 = jnp.float32
BF16 = jnp.bfloat16

D_MODEL = 2048
BATCH = 4
SEQ = 2048
CTX_LEN = 256
DEPTH = 2
GRID_W = 64
GRID_ROWS = SEQ // GRID_W
D_FF = 5632
CONV_DIM = 1024
CONV_K = 31
CONV_PAD = CONV_K // 2
SSM_DIM = 1024
SSM_GROUP = 16
SSM_GROUPS = 64
SSM_STATE = 64
N_MOD = 9
N_IN = 2 * CONV_DIM + SSM_DIM + 2 * D_MODEL
FFN_RES = 0.5
EPS = 1e-6

LANES = 128
SUBLANES = 8
ROWS_CTX = CTX_LEN * BATCH
ROWS_LAT = SEQ * BATCH
ROWS = ROWS_CTX + ROWS_LAT
ROWS_PER_GRID_ROW = GRID_W * BATCH

VMEM_LIMIT = 60 * 1024 * 1024

STREAM_TILE_ROWS = 1024
FFN_CHUNK = 512
PROJ_COLS = 1024
MIXOUT_TILE_ROWS = 256
RELAYOUT_TILE_T = 128

GROUPS_PER_SLAB = LANES // SSM_GROUP
N_GP = SSM_GROUPS // 2
GP_PER_SLAB = GROUPS_PER_SLAB // 2
N_SLAB = SSM_DIM // LANES
SSM_CHUNK_T = 64
SSM_CHUNK_ROWS = SSM_CHUNK_T * BATCH
SSM_STEP_ROWS = 2 * SSM_CHUNK_ROWS
N_CHUNK_CTX = CTX_LEN // SSM_CHUNK_T
N_CHUNK = (CTX_LEN + SEQ) // SSM_CHUNK_T
assert 2 * BATCH == SUBLANES


def _cparams(sem):
    return pltpu.CompilerParams(dimension_semantics=sem, vmem_limit_bytes=VMEM_LIMIT)


def _layer_spec(shape, layer):
    nd = len(shape) - 1
    return pl.BlockSpec((None,) + tuple(shape[1:]), lambda *_: (layer,) + (0,) * nd,
                        pipeline_mode=pl.Buffered(1))


def _mxu_operand(w_ref, rows=slice(None), cols=slice(None)):
    w = w_ref[rows, cols]
    return w if w.dtype == BF16 else w.astype(BF16)


def _rows8(x, m, op):
    n, d = x.shape
    x3 = x.reshape(n // SUBLANES, SUBLANES, d)
    return op(x3, m[None]).reshape(n, d)


def _row_tile_copy(x_hbm, xbuf_ref, sem, tile, tm):
    rows = pl.ds(pl.multiple_of(tile * tm, tm), tm)
    return pltpu.make_async_copy(x_hbm.at[rows, :], xbuf_ref, sem)


def _interleave_batch(src_ref, o_ref, slab_ref, tq):
    for c in range(D_MODEL // LANES):
        cs = slice(c * LANES, (c + 1) * LANES)
        for b in range(BATCH):
            slab_ref[c, pl.ds(b, tq, stride=BATCH), :] = src_ref[b, :, cs]
        o_ref[:, cs] = slab_ref[c]


def _to_rows_kernel(x_ref, c_ref, o_ref, slab_ref, *, n_lat_tiles, tq):
    i = pl.program_id(0)

    @pl.when(i < n_lat_tiles)
    def _():
        _interleave_batch(x_ref, o_ref, slab_ref, tq)

    @pl.when(i >= n_lat_tiles)
    def _():
        _interleave_batch(c_ref, o_ref, slab_ref, tq)


def _to_rows(x, ctx):
    tq = RELAYOUT_TILE_T
    n_lat_tiles = SEQ // tq
    n_tiles = (SEQ + CTX_LEN) // tq
    return pl.pallas_call(
        functools.partial(_to_rows_kernel, n_lat_tiles=n_lat_tiles, tq=tq),
        grid=(n_tiles,),
        in_specs=[
            pl.BlockSpec((BATCH, tq, D_MODEL), lambda i: (0, jnp.minimum(i, n_lat_tiles - 1), 0)),
            pl.BlockSpec((BATCH, tq, D_MODEL), lambda i: (0, jnp.maximum(i - n_lat_tiles, 0), 0)),
        ],
        out_specs=pl.BlockSpec((tq * BATCH, D_MODEL), lambda i: (i, 0)),
        out_shape=jax.ShapeDtypeStruct((ROWS, D_MODEL), F32),
        scratch_shapes=[pltpu.VMEM((D_MODEL // LANES, tq * BATCH, LANES), F32)],
        compiler_params=_cparams(("arbitrary",)),
        name="to_rows",
    )(x, ctx)


def _final_norm_kernel(x_ref, gain_ref, o_ref, slab_ref, *, tq):
    x = x_ref[...]
    xn = x * lax.rsqrt(jnp.mean(x * x, axis=-1, keepdims=True) + EPS) * gain_ref[...]
    for c in range(D_MODEL // LANES):
        cs = slice(c * LANES, (c + 1) * LANES)
        slab_ref[c] = xn[:, cs]
        for b in range(BATCH):
            o_ref[b, :, cs] = slab_ref[c, pl.ds(b, tq, stride=BATCH), :]


def _final_norm(xs, gain):
    tq = RELAYOUT_TILE_T
    return pl.pallas_call(
        functools.partial(_final_norm_kernel, tq=tq),
        grid=(SEQ // tq,),
        in_specs=[pl.BlockSpec((tq * BATCH, D_MODEL), lambda i: (i, 0)),
                  pl.BlockSpec((1, D_MODEL), lambda i: (0, 0))],
        out_specs=pl.BlockSpec((BATCH, tq, D_MODEL), lambda i: (0, i, 0)),
        out_shape=jax.ShapeDtypeStruct((BATCH, SEQ, D_MODEL), F32),
        scratch_shapes=[pltpu.VMEM((D_MODEL // LANES, tq * BATCH, LANES), F32)],
        compiler_params=_cparams(("arbitrary",)),
        name="final_norm",
    )(xs, gain.reshape(1, D_MODEL))


def _ada_kernel(c_ref, w_ref, b_ref, o_ref):
    s = jax.nn.silu(c_ref[...]).astype(BF16)
    o_ref[...] = jnp.dot(s, w_ref[...].astype(BF16), preferred_element_type=F32) + b_ref[...]


def _ada(c_rows, ada_w, ada_b):
    n_rows = c_rows.shape[0]
    return pl.pallas_call(
        _ada_kernel,
        grid=(DEPTH, N_MOD),
        in_specs=[
            pl.BlockSpec((n_rows, D_MODEL), lambda l, m: (0, 0)),
            pl.BlockSpec((None, D_MODEL, D_MODEL), lambda l, m: (l, 0, m)),
            pl.BlockSpec((None, None, 1, D_MODEL), lambda l, m: (l, m, 0, 0)),
        ],
        out_specs=pl.BlockSpec((None, None, n_rows, D_MODEL), lambda l, m: (l, m, 0, 0)),
        out_shape=jax.ShapeDtypeStruct((DEPTH, N_MOD, n_rows, D_MODEL), F32),
        compiler_params=_cparams(("arbitrary", "arbitrary")),
        name="ada_mod",
    )(c_rows, ada_w, ada_b.reshape(DEPTH, N_MOD, 1, D_MODEL))


def _norm_mod(x, gain_ref, mod_ref):
    y = x * lax.rsqrt(jnp.mean(x * x, axis=-1, keepdims=True) + EPS) * gain_ref[...]
    h = _rows8(y, 1.0 + mod_ref[1], jnp.multiply)
    return _rows8(h, mod_ref[0], jnp.add).astype(BF16)


def _ffn_chunk_halves(hn_ref, wa_ref, wg_ref, wd_ref):
    hn = hn_ref[...]
    half = wa_ref.shape[1] // 2
    halves = (slice(0, half), slice(half, 2 * half))
    acts = []
    for cols in halves:
        a = jnp.dot(hn, _mxu_operand(wa_ref, cols=cols), preferred_element_type=F32)
        g = jnp.dot(hn, _mxu_operand(wg_ref, cols=cols), preferred_element_type=F32)
        acts.append((jax.nn.silu(g) * a).astype(BF16))
    for act, rows in zip(acts, halves):
        yield jnp.dot(act, _mxu_operand(wd_ref, rows=rows), preferred_element_type=F32)


def _ffn_kernel(x_hbm, mod_ref, gain_ref, wa_ref, wg_ref, wd_ref, o_ref, hn_ref, xbuf_ref, sem,
                *, tm, n_tiles):
    i = pl.program_id(0)
    j = pl.program_id(1)

    x_copy = functools.partial(_row_tile_copy, x_hbm, xbuf_ref, sem, tm=tm)

    def accumulate_chunk(base=None):
        for part in _ffn_chunk_halves(hn_ref, wa_ref, wg_ref, wd_ref):
            gated = _rows8(part, mod_ref[2] * FFN_RES, jnp.multiply)
            o_ref[...] = (o_ref[...] if base is None else base) + gated
            base = None

    @pl.when(j == 0)
    def _():
        @pl.when(i == 0)
        def _():
            x_copy(0).start()

        x_copy(i).wait()
        x = xbuf_ref[...]
        hn_ref[...] = _norm_mod(x, gain_ref, mod_ref)
        accumulate_chunk(base=x)

    @pl.when(j == 1)
    def _():
        @pl.when(i + 1 < n_tiles)
        def _():
            x_copy(i + 1).start()

        accumulate_chunk()

    @pl.when(j > 1)
    def _():
        accumulate_chunk()


def _ffn(x, modtab, sub, gain, w_up, w_down, layer, *, latent_only=False):
    tm, tf = STREAM_TILE_ROWS, FFN_CHUNK
    nf = D_FF // tf
    assert nf >= 2
    n_lat_tiles = ROWS_LAT // tm
    n_tiles = n_lat_tiles if latent_only else ROWS // tm
    return pl.pallas_call(
        functools.partial(_ffn_kernel, tm=tm, n_tiles=n_tiles),
        grid=(n_tiles, nf),
        in_specs=[
            pl.BlockSpec(memory_space=pl.ANY),
            pl.BlockSpec((3, SUBLANES, D_MODEL),
                         lambda i, j: (sub, jnp.where(i < n_lat_tiles, 0, 1), 0)),
            pl.BlockSpec((1, D_MODEL), lambda i, j: (0, 0)),
            pl.BlockSpec((None, D_MODEL, tf), lambda i, j: (layer, 0, j)),
            pl.BlockSpec((None, D_MODEL, tf), lambda i, j: (layer, 0, j + nf)),
            pl.BlockSpec((None, tf, D_MODEL), lambda i, j: (layer, j, 0)),
        ],
        out_specs=pl.BlockSpec((tm, D_MODEL), lambda i, j: (i, 0)),
        out_shape=jax.ShapeDtypeStruct((n_tiles * tm, D_MODEL), F32),
        scratch_shapes=[pltpu.VMEM((tm, D_MODEL), BF16), pltpu.VMEM((tm, D_MODEL), F32),
                        pltpu.SemaphoreType.DMA(())],
        compiler_params=_cparams(("arbitrary", "arbitrary")),
        name="ffn",
    )(x, modtab, gain.reshape(1, D_MODEL), w_up, w_up, w_down)


def _inproj_kernel(x_hbm, mod_ref, gain_ref, w_ref, b_ref, uc_ref, us_ref, gt_ref, hn_ref, xbuf_ref, sem,
                   *, tm, n_tiles):
    i = pl.program_id(0)
    j = pl.program_id(1)
    x_copy = functools.partial(_row_tile_copy, x_hbm, xbuf_ref, sem, tm=tm)

    def proj():
        return jnp.dot(hn_ref[...], _mxu_operand(w_ref), preferred_element_type=F32) + b_ref[0]

    @pl.when(j == 0)
    def _():
        @pl.when(i == 0)
        def _():
            x_copy(0).start()

        x_copy(i).wait()
        hn_ref[...] = _norm_mod(xbuf_ref[...], gain_ref, mod_ref)
        uc_ref[...] = proj()

    @pl.when(j == 1)
    def _():
        @pl.when(i + 1 < n_tiles)
        def _():
            x_copy(i + 1).start()

        uc_ref[...] = uc_ref[...] * jax.nn.sigmoid(proj())

    @pl.when(j == 2)
    def _():
        us_ref[...] = proj()

    @pl.when(j >= 3)
    def _():
        gt_ref[...] = jax.nn.sigmoid(proj()).astype(BF16)


def _inproj(x, modtab, gain, w_in, b_in, layer):
    tm = STREAM_TILE_ROWS
    tn = PROJ_COLS
    n_lat_tiles = ROWS_LAT // tm
    nblk = N_IN // tn
    n_head = nblk - 2 * D_MODEL // tn
    n_tiles = ROWS // tm
    return pl.pallas_call(
        functools.partial(_inproj_kernel, tm=tm, n_tiles=n_tiles),
        grid=(n_tiles, nblk),
        in_specs=[
            pl.BlockSpec(memory_space=pl.ANY),
            pl.BlockSpec((3, SUBLANES, D_MODEL),
                         lambda i, j: (1, jnp.where(i < n_lat_tiles, 0, 1), 0)),
            pl.BlockSpec((1, D_MODEL), lambda i, j: (0, 0)),
            pl.BlockSpec((None, D_MODEL, tn), lambda i, j: (layer, 0, j)),
            pl.BlockSpec((1, 1, tn), lambda i, j: (j, 0, 0)),
        ],
        out_specs=[
            pl.BlockSpec((tm, tn), lambda i, j: (i, 0)),
            pl.BlockSpec((tm, tn), lambda i, j: (i, 0)),
            pl.BlockSpec((tm, tn), lambda i, j: (i, jnp.maximum(j - n_head, 0))),
        ],
        out_shape=[
            jax.ShapeDtypeStruct((ROWS, CONV_DIM), F32),
            jax.ShapeDtypeStruct((ROWS, SSM_DIM), F32),
            jax.ShapeDtypeStruct((ROWS, 2 * D_MODEL), BF16),
        ],
        scratch_shapes=[pltpu.VMEM((tm, D_MODEL), BF16), pltpu.VMEM((tm, D_MODEL), F32),
                        pltpu.SemaphoreType.DMA(())],
        compiler_params=_cparams(("arbitrary", "arbitrary")),
        name="mixer_inproj",
    )(x, modtab, gain.reshape(1, D_MODEL), w_in, b_in.reshape(nblk, 1, tn))


CONV_ROW_PAD = 64
CONV_SUB_ROWS = 64
CONV_HALF = CONV_DIM // 2


def _conv_seq_taps(u_ref, w_ref, b_ref, o_ref, pad_ref, pad4_ref, *, n, cb):
    zeros = jnp.zeros((CONV_ROW_PAD, cb), F32)
    pad_ref[0:CONV_ROW_PAD, :] = zeros
    pad_ref[CONV_ROW_PAD + n:, :] = zeros
    pad_ref[CONV_ROW_PAD:CONV_ROW_PAD + n, :] = u_ref[...]
    n_pad = n + 2 * CONV_ROW_PAD
    pad4_ref[0:n_pad - SUBLANES, :] = pad_ref[BATCH:n_pad - SUBLANES + BATCH, :]
    base = CONV_ROW_PAD - CONV_PAD * BATCH
    for c in range(cb // LANES):
        cs = slice(c * LANES, (c + 1) * LANES)
        for r in range(n // CONV_SUB_ROWS):
            r0 = r * CONV_SUB_ROWS
            acc = jnp.broadcast_to(b_ref[:, cs], (CONV_SUB_ROWS, LANES))
            for k in range(CONV_K):
                off = base + r0 + k * BATCH
                if off % SUBLANES == 0:
                    tap = pad_ref[off:off + CONV_SUB_ROWS, cs]
                else:
                    tap = pad4_ref[off - BATCH:off - BATCH + CONV_SUB_ROWS, cs]
                acc = acc + w_ref[k:k + 1, cs] * tap
            o_ref[r0:r0 + CONV_SUB_ROWS, cs] = acc


def _conv_rows_kernel(u_ref, w_ref, b_ref, o_ref, pad_ref, pad4_ref):
    i = pl.program_id(0)
    o_ref[:, CONV_HALF:] = jnp.zeros((ROWS_PER_GRID_ROW, CONV_DIM - CONV_HALF), F32)

    @pl.when(i < GRID_ROWS)
    def _():
        _conv_seq_taps(u_ref, w_ref, b_ref, o_ref, pad_ref, pad4_ref, n=ROWS_PER_GRID_ROW, cb=CONV_HALF)

    @pl.when(i >= GRID_ROWS)
    def _():
        o_ref[:, :CONV_HALF] = jnp.zeros((ROWS_PER_GRID_ROW, CONV_HALF), F32)


def _conv_rows(u, w, b, *, rows_out):
    n = ROWS_PER_GRID_ROW
    n_pad = n + 2 * CONV_ROW_PAD
    return pl.pallas_call(
        _conv_rows_kernel,
        grid=(rows_out // n,),
        in_specs=[
            pl.BlockSpec((n, CONV_HALF), lambda i: (jnp.minimum(i, GRID_ROWS - 1), 0)),
            pl.BlockSpec((CONV_K, CONV_HALF), lambda i: (0, 0)),
            pl.BlockSpec((1, CONV_HALF), lambda i: (0, 0)),
        ],
        out_specs=pl.BlockSpec((n, CONV_DIM), lambda i: (i, 0)),
        out_shape=jax.ShapeDtypeStruct((rows_out, CONV_DIM), F32),
        scratch_shapes=[pltpu.VMEM((n_pad, CONV_HALF), F32), pltpu.VMEM((n_pad, CONV_HALF), F32)],
        compiler_params=_cparams(("arbitrary",)),
        name="conv_rows",
    )(u, w, b.reshape(1, CONV_DIM))


def _conv_ctx_kernel(u_ref, w_ref, b_ref, prev_ref, o_ref, pad_ref, pad4_ref):
    del prev_ref
    _conv_seq_taps(u_ref, w_ref, b_ref, o_ref, pad_ref, pad4_ref, n=ROWS_CTX, cb=LANES)


def _conv_ctx(u, w, b, prev):
    blk0 = ROWS_LAT // ROWS_CTX
    n_pad = ROWS_CTX + 2 * CONV_ROW_PAD
    return pl.pallas_call(
        _conv_ctx_kernel,
        grid=(CONV_DIM // LANES,),
        in_specs=[
            pl.BlockSpec((ROWS_CTX, LANES), lambda c: (blk0, c)),
            pl.BlockSpec((CONV_K, LANES), lambda c: (0, c)),
            pl.BlockSpec((1, LANES), lambda c: (0, c)),
            pl.BlockSpec(memory_space=pl.ANY),
        ],
        out_specs=pl.BlockSpec((ROWS_CTX, LANES), lambda c: (blk0, c)),
        out_shape=jax.ShapeDtypeStruct((ROWS, CONV_DIM), F32),
        scratch_shapes=[pltpu.VMEM((n_pad, LANES), F32), pltpu.VMEM((n_pad, LANES), F32)],
        input_output_aliases={3: 0},
        compiler_params=_cparams(("arbitrary",)),
        name="conv_ctx",
    )(u, w, b.reshape(1, CONV_DIM), prev)


def _conv_vert_kernel(u_ref, w_ref, b_ref, prev_ref, o_ref, pad_ref):
    del prev_ref
    rpg = ROWS_PER_GRID_ROW
    zeros = jnp.zeros((CONV_PAD * rpg, LANES), F32)
    pad_ref[0:CONV_PAD * rpg, :] = zeros
    pad_ref[(CONV_PAD + GRID_ROWS) * rpg:, :] = zeros
    pad_ref[CONV_PAD * rpg:(CONV_PAD + GRID_ROWS) * rpg, :] = u_ref[...]
    bias = jnp.broadcast_to(b_ref[...], (CONV_SUB_ROWS, LANES))

    def body(r, carry):
        for s in range(rpg // CONV_SUB_ROWS):
            acc = bias
            for k in range(CONV_K):
                off = pl.multiple_of((r + k) * rpg + s * CONV_SUB_ROWS, CONV_SUB_ROWS)
                acc = acc + w_ref[k:k + 1, :] * pad_ref[pl.ds(off, CONV_SUB_ROWS), :]
            o0 = pl.multiple_of(r * rpg + s * CONV_SUB_ROWS, CONV_SUB_ROWS)
            o_ref[pl.ds(o0, CONV_SUB_ROWS), :] = acc
        return carry

    lax.fori_loop(0, GRID_ROWS, body, 0)


def _conv_vert(u, w, b, prev):
    half_blk = CONV_HALF // LANES
    return pl.pallas_call(
        _conv_vert_kernel,
        grid=(half_blk,),
        in_specs=[
            pl.BlockSpec((ROWS_LAT, LANES), lambda c: (0, c + half_blk)),
            pl.BlockSpec((CONV_K, LANES), lambda c: (0, c + half_blk)),
            pl.BlockSpec((1, LANES), lambda c: (0, c + half_blk)),
            pl.BlockSpec(memory_space=pl.ANY),
        ],
        out_specs=pl.BlockSpec((ROWS_LAT, LANES), lambda c: (0, c + half_blk)),
        out_shape=jax.ShapeDtypeStruct(prev.shape, F32),
        scratch_shapes=[pltpu.VMEM(((GRID_ROWS + 2 * CONV_PAD) * ROWS_PER_GRID_ROW, LANES), F32)],
        input_output_aliases={3: 0},
        compiler_params=_cparams(("arbitrary",)),
        name="conv_vert",
    )(u, w, b.reshape(1, CONV_DIM), prev)


def _ssm_kernel(uf_ref, ub_ref, b2_ref, c2_ref, lam_ref, yf_ref, yb_ref,
                lhs_ref, s_ref, y2_ref, hst_ref):
    k = pl.program_id(0)

    @pl.when(k == 0)
    def _():
        hst_ref[...] = jnp.zeros_like(hst_ref)

    n_pair = SSM_CHUNK_T // 2
    low = lax.broadcasted_iota(jnp.int32, (SUBLANES, SSM_DIM), 0) < BATCH
    zero = jnp.zeros((SUBLANES, SSM_DIM), F32)
    for m in range(n_pair):
        f = uf_ref[m * SUBLANES:(m + 1) * SUBLANES, :]
        mb = n_pair - 1 - m
        g = ub_ref[mb * SUBLANES:(mb + 1) * SUBLANES, :]
        fr = pltpu.roll(f, BATCH, 0)
        gr = pltpu.roll(g, BATCH, 0)
        fwd = jnp.concatenate([jnp.where(low, f, zero), jnp.where(low, fr, zero)], axis=0).astype(BF16)
        bwd = jnp.concatenate([jnp.where(low, zero, g), jnp.where(low, zero, gr)], axis=0).astype(BF16)
        r0 = m * 2 * SUBLANES
        for j in range(N_SLAB):
            cs = slice(j * LANES, (j + 1) * LANES)
            lhs_ref[r0:r0 + 2 * SUBLANES, 2 * j * LANES:(2 * j + 1) * LANES] = fwd[:, cs]
            lhs_ref[r0:r0 + 2 * SUBLANES, (2 * j + 1) * LANES:(2 * j + 2) * LANES] = bwd[:, cs]

    re = slice(0, LANES)
    im = slice(LANES, 2 * LANES)

    def project(j):
        for gp in range(j * GP_PER_SLAB, (j + 1) * GP_PER_SLAB):
            s_ref[gp] = jnp.dot(lhs_ref[:, 2 * j * LANES:(2 * j + 2) * LANES], b2_ref[gp],
                                preferred_element_type=F32)

    def scan(j):
        for gp in range(j * GP_PER_SLAB, (j + 1) * GP_PER_SLAB):
            lr, li = lam_ref[gp, 0], lam_ref[gp, 1]
            hre, him = hst_ref[gp, 0], hst_ref[gp, 1]
            for s in range(SSM_CHUNK_T):
                rows = slice(s * SUBLANES, (s + 1) * SUBLANES)
                nre = lr * hre - li * him + s_ref[gp, rows, re]
                nim = lr * him + li * hre + s_ref[gp, rows, im]
                s_ref[gp, rows, re] = nre
                s_ref[gp, rows, im] = nim
                hre, him = nre, nim
            hst_ref[gp, 0] = hre
            hst_ref[gp, 1] = him

    def read_out(j):
        acc = None
        for gp in range(j * GP_PER_SLAB, (j + 1) * GP_PER_SLAB):
            p = jnp.dot(s_ref[gp].astype(BF16), c2_ref[gp], preferred_element_type=F32)
            acc = p if acc is None else acc + p
        y2_ref[:, 2 * j * LANES:(2 * j + 2) * LANES] = acc

    project(0)
    for j in range(N_SLAB):
        if j + 1 < N_SLAB:
            project(j + 1)
        scan(j)
        if j >= 1:
            read_out(j - 1)
    read_out(N_SLAB - 1)

    low1 = lax.broadcasted_iota(jnp.int32, (SUBLANES, LANES), 0) < BATCH
    for m in range(n_pair):
        mb = n_pair - 1 - m
        r0 = m * 2 * SUBLANES
        for j in range(N_SLAB):
            cf = slice(2 * j * LANES, (2 * j + 1) * LANES)
            cb = slice((2 * j + 1) * LANES, (2 * j + 2) * LANES)
            cs = slice(j * LANES, (j + 1) * LANES)
            ef = y2_ref[r0:r0 + SUBLANES, cf]
            of = pltpu.roll(y2_ref[r0 + SUBLANES:r0 + 2 * SUBLANES, cf], BATCH, 0)
            yf_ref[m * SUBLANES:(m + 1) * SUBLANES, cs] = jnp.where(low1, ef, of)
            eb = y2_ref[r0:r0 + SUBLANES, cb]
            ob = pltpu.roll(y2_ref[r0 + SUBLANES:r0 + 2 * SUBLANES, cb], BATCH, 0)
            yb_ref[mb * SUBLANES:(mb + 1) * SUBLANES, cs] = jnp.where(low1, ob, eb)


def _ssm(us, b2, c2, lamtab, layer):
    n_lat = N_CHUNK - N_CHUNK_CTX

    def fwd_blk(k):
        return jnp.where(k < N_CHUNK_CTX, n_lat + k, k - N_CHUNK_CTX)

    def bwd_blk(k):
        return N_CHUNK - 1 - k

    blk = (SSM_CHUNK_ROWS, SSM_DIM)
    return pl.pallas_call(
        _ssm_kernel,
        grid=(N_CHUNK,),
        in_specs=[
            pl.BlockSpec(blk, lambda k: (fwd_blk(k), 0)),
            pl.BlockSpec(blk, lambda k: (bwd_blk(k), 0)),
            _layer_spec(b2.shape, layer),
            _layer_spec(c2.shape, layer),
            _layer_spec(lamtab.shape, layer),
        ],
        out_specs=[
            pl.BlockSpec(blk, lambda k: (fwd_blk(k), 0)),
            pl.BlockSpec(blk, lambda k: (bwd_blk(k), 0)),
        ],
        out_shape=[jax.ShapeDtypeStruct((ROWS, SSM_DIM), F32)] * 2,
        scratch_shapes=[
            pltpu.VMEM((SSM_STEP_ROWS, 2 * SSM_DIM), BF16),
            pltpu.VMEM((N_GP, SSM_STEP_ROWS, 2 * LANES), F32),
            pltpu.VMEM((SSM_STEP_ROWS, 2 * SSM_DIM), F32),
            pltpu.VMEM((N_GP, 2, SUBLANES, LANES), F32),
        ],
        compiler_params=_cparams(("arbitrary",)),
        name="s5_scan",
    )(us, us, b2, c2, lamtab)


def _ssm_tables(a_re, a_im, log_dt, b_re, b_im, c_re, c_im):
    lam = lax.complex(a_re.astype(F32), a_im.astype(F32))
    dt = jnp.exp(log_dt.astype(F32))[..., None]
    lam_bar = jnp.exp(lam * dt)
    bmat_c = lax.complex(b_re.astype(F32), b_im.astype(F32))
    b_bar = ((lam_bar - 1) / lam)[..., None] * bmat_c

    def pair_rows(v):
        v = v.reshape(2, N_GP, 2, SSM_GROUP, SSM_STATE).transpose(1, 0, 2, 3, 4)
        z = jnp.zeros_like(v[:, :, 0])
        q0 = jnp.concatenate([v[:, :, 0], z], axis=-1)
        q1 = jnp.concatenate([z, v[:, :, 1]], axis=-1)
        return jnp.stack([q0, q1], axis=2)

    slab_pos = (jnp.arange(N_GP) % GP_PER_SLAB)[:, None] == jnp.arange(GP_PER_SLAB)[None, :]
    place = slab_pos.astype(F32)[:, None, :, None, None, None]

    def place_rows(re_v, im_v):
        t = jnp.concatenate([pair_rows(re_v), pair_rows(im_v)], axis=-1)
        t = t[:, :, None] * place
        return t.reshape(N_GP, 2 * LANES, 2 * LANES)

    bt = jnp.swapaxes(b_bar, -1, -2)
    b2 = place_rows(jnp.real(bt), jnp.imag(bt))
    c2 = jnp.swapaxes(place_rows(c_re.astype(F32), -c_im.astype(F32)), 1, 2)
    lam_ri = jnp.stack([jnp.real(lam_bar), jnp.imag(lam_bar)], axis=0)
    lam_ri = lam_ri.reshape(2, 2, N_GP, 1, LANES).transpose(2, 0, 1, 3, 4)
    lamtab = jnp.broadcast_to(lam_ri, (N_GP, 2, 2, BATCH, LANES)).reshape(N_GP, 2, SUBLANES, LANES)
    return b2.astype(BF16), c2.astype(BF16), lamtab


MIX_COLS = 512


def _mixout_kernel(x_ref, cv_ref, yf_ref, yb_ref, us_ref, gt_ref, mod_ref, lng_ref, lnb_ref, sd_ref,
                   cwo_ref, wglu_ref, bglu_ref, wout_ref, o_ref):
    cv = cv_ref[...]
    mu = jnp.mean(cv, axis=-1, keepdims=True)
    var = jnp.mean(jnp.square(cv - mu), axis=-1, keepdims=True)
    ln = (cv - mu) * lax.rsqrt(var + EPS) * lng_ref[...] + lnb_ref[...]
    s_conv = jax.nn.silu(ln).astype(BF16)
    y = yf_ref[...] + yb_ref[...] + sd_ref[...] * us_ref[...]
    s_ssm = jax.nn.gelu(y).astype(BF16)
    acc = None
    for c in range(D_MODEL // MIX_COLS):
        lo = c * MIX_COLS
        ca = slice(lo, lo + MIX_COLS)
        cg = slice(D_MODEL + lo, D_MODEL + lo + MIX_COLS)
        y_conv = jnp.dot(s_conv, cwo_ref[:, ca], preferred_element_type=F32)
        za = jnp.dot(s_ssm, wglu_ref[:, ca], preferred_element_type=F32) + bglu_ref[:, ca]
        zg = jnp.dot(s_ssm, wglu_ref[:, cg], preferred_element_type=F32) + bglu_ref[:, cg]
        y_ssm = za * jax.nn.sigmoid(zg)
        mixed = (gt_ref[:, ca].astype(F32) * y_conv + gt_ref[:, cg].astype(F32) * y_ssm).astype(BF16)
        part = jnp.dot(mixed, wout_ref[ca, :], preferred_element_type=F32)
        acc = part if acc is None else acc + part
    o_ref[...] = x_ref[...] + _rows8(acc, mod_ref[2], jnp.multiply)


def _mixout(x, cv, yf, yb, us, gt, modtab, ln_g, ln_b, ssm_d, cwo, wglu, bglu, wout, layer, *, latent_only):
    tm = MIXOUT_TILE_ROWS
    n_lat_tiles = ROWS_LAT // tm
    n_tiles = n_lat_tiles if latent_only else ROWS // tm
    row = lambda w: pl.BlockSpec((tm, w), lambda i: (i, 0))
    vec = lambda w: pl.BlockSpec((1, w), lambda i: (0, 0), pipeline_mode=pl.Buffered(1))
    wgt = lambda r, c: pl.BlockSpec((None, r, c), lambda i: (layer, 0, 0), pipeline_mode=pl.Buffered(1))
    return pl.pallas_call(
        _mixout_kernel,
        grid=(n_tiles,),
        in_specs=[
            row(D_MODEL), row(CONV_DIM), row(SSM_DIM), row(SSM_DIM), row(SSM_DIM), row(2 * D_MODEL),
            pl.BlockSpec((3, SUBLANES, D_MODEL), lambda i: (1, jnp.where(i < n_lat_tiles, 0, 1), 0)),
            vec(CONV_DIM), vec(CONV_DIM), vec(SSM_DIM),
            wgt(CONV_DIM, D_MODEL), wgt(SSM_DIM, 2 * D_MODEL), vec(2 * D_MODEL), wgt(D_MODEL, D_MODEL),
        ],
        out_specs=row(D_MODEL),
        out_shape=jax.ShapeDtypeStruct((ROWS, D_MODEL), F32),
        input_output_aliases={0: 0},
        compiler_params=_cparams(("arbitrary",)),
        name="mixer_out",
    )(x, cv, yf, yb, us, gt, modtab, ln_g.reshape(1, -1), ln_b.reshape(1, -1), ssm_d.reshape(1, -1),
      cwo, wglu, bglu.reshape(1, -1), wout)


def kernel(x, c, ctx, c_ctx, ada_w, ada_b, norm_ffn1, ffn1_w_up, ffn1_w_down, norm_mix, w_in, b_in, conv_dw, conv_db, conv_ln_g, conv_ln_b, conv_w_out, ssm_a_re, ssm_a_im, ssm_log_dt, ssm_b_re, ssm_b_im, ssm_c_re, ssm_c_im, ssm_d, ssm_w_glu, ssm_b_glu, w_out, norm_ffn2, ffn2_w_up, ffn2_w_down, norm_final):
    xs = _to_rows(x, ctx)

    lat_rows = jnp.arange(SUBLANES) % BATCH
    c_rows = jnp.concatenate([c[lat_rows], jnp.broadcast_to(c_ctx, (SUBLANES, D_MODEL))], axis=0)
    modtab = _ada(c_rows, ada_w, ada_b)
    b2, c2, lamtab = jax.vmap(_ssm_tables)(ssm_a_re, ssm_a_im, ssm_log_dt,
                                            ssm_b_re, ssm_b_im, ssm_c_re, ssm_c_im)

    cwo, wglu, wout = conv_w_out.astype(BF16), ssm_w_glu.astype(BF16), w_out.astype(BF16)

    for l in range(DEPTH):
        last = l == DEPTH - 1
        mt = modtab[l]
        xs = _ffn(xs, mt, 0, norm_ffn1[l], ffn1_w_up, ffn1_w_down, l)
        uc, us, gt = _inproj(xs, mt, norm_mix[l], w_in, b_in[l], l)

        cv = _conv_rows(uc, conv_dw[l], conv_db[l], rows_out=ROWS_LAT if last else ROWS)
        cv = _conv_vert(uc, conv_dw[l], conv_db[l], cv)
        if not last:
            cv = _conv_ctx(uc, conv_dw[l], conv_db[l], cv)

        yf, yb = _ssm(us, b2, c2, lamtab, l)

        xs = _mixout(xs, cv, yf, yb, us, gt, mt, conv_ln_g[l], conv_ln_b[l], ssm_d[l],
                     cwo, wglu, ssm_b_glu[l], wout, l, latent_only=last)
        xs = _ffn(xs, mt, 2, norm_ffn2[l], ffn2_w_up, ffn2_w_down, l, latent_only=last)

    return _final_norm(xs, norm_final)
```

```python
import functools

import jax
import jax.numpy as jnp
from jax import lax
from jax.experimental import pallas as pl
from jax.experimental.pallas import tpu as pltpu

F32 = jnp.float32
BF16 = jnp.bfloat16

D_MODEL = 2048
BATCH = 4
SEQ = 2048
CTX_LEN = 256
DEPTH = 2
GRID_W = 64
GRID_ROWS = SEQ // GRID_W
D_FF = 5632
CONV_DIM = 1024
CONV_K = 31
CONV_PAD = CONV_K // 2
SSM_DIM = 1024
SSM_GROUP = 16
SSM_GROUPS = 64
SSM_STATE = 64
N_MOD = 9
N_IN = 2 * CONV_DIM + SSM_DIM + 2 * D_MODEL
FFN_RES = 0.5
EPS = 1e-6

LANES = 128
SUBLANES = 8
ROWS_CTX = CTX_LEN * BATCH
ROWS_LAT = SEQ * BATCH
ROWS = ROWS_CTX + ROWS_LAT
ROWS_PER_GRID_ROW = GRID_W * BATCH

VMEM_LIMIT = 60 * 1024 * 1024

STREAM_TILE_ROWS = 1024
FFN_CHUNK = 512
PROJ_COLS = 1024
MIX_TILE_ROWS = 512
RELAYOUT_TILE_T = 128

GROUPS_PER_SLAB = LANES // SSM_GROUP
N_GP = SSM_GROUPS // 2
GP_PER_SLAB = GROUPS_PER_SLAB // 2
N_SLAB = SSM_DIM // LANES
SSM_CHUNK_T = 64
SSM_CHUNK_ROWS = SSM_CHUNK_T * BATCH
SSM_STEP_ROWS = 2 * SSM_CHUNK_ROWS
N_CHUNK_CTX = CTX_LEN // SSM_CHUNK_T
N_CHUNK = (CTX_LEN + SEQ) // SSM_CHUNK_T
assert 2 * BATCH == SUBLANES


def _cparams(sem):
    return pltpu.CompilerParams(dimension_semantics=sem, vmem_limit_bytes=VMEM_LIMIT)


def _layer_spec(shape, layer):
    nd = len(shape) - 1
    return pl.BlockSpec((None,) + tuple(shape[1:]), lambda *_: (layer,) + (0,) * nd,
                        pipeline_mode=pl.Buffered(1))


def _mxu_operand(w_ref, rows=slice(None), cols=slice(None)):
    w = w_ref[rows, cols]
    return w if w.dtype == BF16 else w.astype(BF16)


def _rows8(x, m, op):
    n, d = x.shape
    x3 = x.reshape(n // SUBLANES, SUBLANES, d)
    return op(x3, m[None]).reshape(n, d)


def _row_tile_copy(x_hbm, xbuf_ref, sem, tile, tm):
    rows = pl.ds(pl.multiple_of(tile * tm, tm), tm)
    return pltpu.make_async_copy(x_hbm.at[rows, :], xbuf_ref, sem)


def _interleave_batch(src_ref, o_ref, slab_ref, tq):
    for c in range(D_MODEL // LANES):
        cs = slice(c * LANES, (c + 1) * LANES)
        for b in range(BATCH):
            slab_ref[c, pl.ds(b, tq, stride=BATCH), :] = src_ref[b, :, cs]
        o_ref[:, cs] = slab_ref[c]


def _to_rows_kernel(x_ref, c_ref, o_ref, slab_ref, *, n_lat_tiles, tq):
    i = pl.program_id(0)

    @pl.when(i < n_lat_tiles)
    def _():
        _interleave_batch(x_ref, o_ref, slab_ref, tq)

    @pl.when(i >= n_lat_tiles)
    def _():
        _interleave_batch(c_ref, o_ref, slab_ref, tq)


def _to_rows(x, ctx):
    tq = RELAYOUT_TILE_T
    n_lat_tiles = SEQ // tq
    n_tiles = (SEQ + CTX_LEN) // tq
    return pl.pallas_call(
        functools.partial(_to_rows_kernel, n_lat_tiles=n_lat_tiles, tq=tq),
        grid=(n_tiles,),
        in_specs=[
            pl.BlockSpec((BATCH, tq, D_MODEL), lambda i: (0, jnp.minimum(i, n_lat_tiles - 1), 0)),
            pl.BlockSpec((BATCH, tq, D_MODEL), lambda i: (0, jnp.maximum(i - n_lat_tiles, 0), 0)),
        ],
        out_specs=pl.BlockSpec((tq * BATCH, D_MODEL), lambda i: (i, 0)),
        out_shape=jax.ShapeDtypeStruct((ROWS, D_MODEL), F32),
        scratch_shapes=[pltpu.VMEM((D_MODEL // LANES, tq * BATCH, LANES), F32)],
        compiler_params=_cparams(("arbitrary",)),
        name="to_rows",
    )(x, ctx)


def _final_norm_kernel(x_ref, gain_ref, o_ref, slab_ref, *, tq):
    x = x_ref[...]
    xn = x * lax.rsqrt(jnp.mean(x * x, axis=-1, keepdims=True) + EPS) * gain_ref[...]
    for c in range(D_MODEL // LANES):
        cs = slice(c * LANES, (c + 1) * LANES)
        slab_ref[c] = xn[:, cs]
        for b in range(BATCH):
            o_ref[b, :, cs] = slab_ref[c, pl.ds(b, tq, stride=BATCH), :]


def _final_norm(xs, gain):
    tq = RELAYOUT_TILE_T
    return pl.pallas_call(
        functools.partial(_final_norm_kernel, tq=tq),
        grid=(SEQ // tq,),
        in_specs=[pl.BlockSpec((tq * BATCH, D_MODEL), lambda i: (i, 0)),
                  pl.BlockSpec((1, D_MODEL), lambda i: (0, 0))],
        out_specs=pl.BlockSpec((BATCH, tq, D_MODEL), lambda i: (0, i, 0)),
        out_shape=jax.ShapeDtypeStruct((BATCH, SEQ, D_MODEL), F32),
        scratch_shapes=[pltpu.VMEM((D_MODEL // LANES, tq * BATCH, LANES), F32)],
        compiler_params=_cparams(("arbitrary",)),
        name="final_norm",
    )(xs, gain.reshape(1, D_MODEL))


ADA_RING = 3


def _ada_kernel(c_ref, w_hbm, b_ref, o_ref, wbuf_ref, sems):
    step = pl.program_id(0) * N_MOD + pl.program_id(1)
    n_steps = DEPTH * N_MOD

    def w_copy(st, slot):
        cols = pl.ds(pl.multiple_of((st % N_MOD) * D_MODEL, D_MODEL), D_MODEL)
        return pltpu.make_async_copy(w_hbm.at[st // N_MOD, :, cols], wbuf_ref.at[slot], sems.at[slot])

    @pl.when(step == 0)
    def _():
        for k in range(ADA_RING):
            w_copy(k, k).start()

    slot = step % ADA_RING
    w_copy(step, slot).wait()
    s = jax.nn.silu(c_ref[...]).astype(BF16)
    o_ref[...] = jnp.dot(s, wbuf_ref[slot].astype(BF16), preferred_element_type=F32) + b_ref[...]

    @pl.when(step + ADA_RING < n_steps)
    def _():
        w_copy(step + ADA_RING, slot).start()


def _ada(c_rows, ada_w, ada_b):
    n_rows = c_rows.shape[0]
    return pl.pallas_call(
        _ada_kernel,
        grid=(DEPTH, N_MOD),
        in_specs=[
            pl.BlockSpec((n_rows, D_MODEL), lambda l, m: (0, 0)),
            pl.BlockSpec(memory_space=pl.ANY),
            pl.BlockSpec((None, None, 1, D_MODEL), lambda l, m: (l, m, 0, 0)),
        ],
        out_specs=pl.BlockSpec((None, None, n_rows, D_MODEL), lambda l, m: (l, m, 0, 0)),
        out_shape=jax.ShapeDtypeStruct((DEPTH, N_MOD, n_rows, D_MODEL), F32),
        scratch_shapes=[pltpu.VMEM((ADA_RING, D_MODEL, D_MODEL), F32), pltpu.SemaphoreType.DMA((ADA_RING,))],
        compiler_params=_cparams(("arbitrary", "arbitrary")),
        name="ada_mod",
    )(c_rows, ada_w, ada_b.reshape(DEPTH, N_MOD, 1, D_MODEL))


def _norm_mod(x, gain_ref, mod_ref):
    y = x * lax.rsqrt(jnp.mean(x * x, axis=-1, keepdims=True) + EPS) * gain_ref[...]
    h = _rows8(y, 1.0 + mod_ref[1], jnp.multiply)
    return _rows8(h, mod_ref[0], jnp.add).astype(BF16)


def _ffn_chunk_halves(hn_ref, wa_ref, wg_ref, wd_ref):
    hn = hn_ref[...]
    half = wa_ref.shape[1] // 2
    halves = (slice(0, half), slice(half, 2 * half))
    acts = []
    for cols in halves:
        a = jnp.dot(hn, _mxu_operand(wa_ref, cols=cols), preferred_element_type=F32)
        g = jnp.dot(hn, _mxu_operand(wg_ref, cols=cols), preferred_element_type=F32)
        acts.append((jax.nn.silu(g) * a).astype(BF16))
    for act, rows in zip(acts, halves):
        yield jnp.dot(act, _mxu_operand(wd_ref, rows=rows), preferred_element_type=F32)


def _ffn_kernel(x_hbm, mod_ref, gain_ref, wa_ref, wg_ref, wd_ref, o_ref, hn_ref, xbuf_ref, sem,
                *, tm, n_tiles):
    i = pl.program_id(0)
    j = pl.program_id(1)

    x_copy = functools.partial(_row_tile_copy, x_hbm, xbuf_ref, sem, tm=tm)

    def accumulate_chunk(base=None):
        for part in _ffn_chunk_halves(hn_ref, wa_ref, wg_ref, wd_ref):
            gated = _rows8(part, mod_ref[2] * FFN_RES, jnp.multiply)
            o_ref[...] = (o_ref[...] if base is None else base) + gated
            base = None

    @pl.when(j == 0)
    def _():
        @pl.when(i == 0)
        def _():
            x_copy(0).start()

        x_copy(i).wait()
        x = xbuf_ref[...]
        hn_ref[...] = _norm_mod(x, gain_ref, mod_ref)
        accumulate_chunk(base=x)

    @pl.when(j == 1)
    def _():
        @pl.when(i + 1 < n_tiles)
        def _():
            x_copy(i + 1).start()

        accumulate_chunk()

    @pl.when(j > 1)
    def _():
        accumulate_chunk()


def _ffn(x, modtab, sub, gain, w_up, w_down, layer, *, latent_only=False):
    tm, tf = STREAM_TILE_ROWS, FFN_CHUNK
    nf = D_FF // tf
    assert nf >= 2
    n_lat_tiles = ROWS_LAT // tm
    n_tiles = n_lat_tiles if latent_only else ROWS // tm
    return pl.pallas_call(
        functools.partial(_ffn_kernel, tm=tm, n_tiles=n_tiles),
        grid=(n_tiles, nf),
        in_specs=[
            pl.BlockSpec(memory_space=pl.ANY),
            pl.BlockSpec((3, SUBLANES, D_MODEL),
                         lambda i, j: (sub, jnp.where(i < n_lat_tiles, 0, 1), 0)),
            pl.BlockSpec((1, D_MODEL), lambda i, j: (0, 0)),
            pl.BlockSpec((None, D_MODEL, tf), lambda i, j: (layer, 0, j)),
            pl.BlockSpec((None, D_MODEL, tf), lambda i, j: (layer, 0, j + nf)),
            pl.BlockSpec((None, tf, D_MODEL), lambda i, j: (layer, j, 0)),
        ],
        out_specs=pl.BlockSpec((tm, D_MODEL), lambda i, j: (i, 0)),
        out_shape=jax.ShapeDtypeStruct((n_tiles * tm, D_MODEL), F32),
        scratch_shapes=[pltpu.VMEM((tm, D_MODEL), BF16), pltpu.VMEM((tm, D_MODEL), F32),
                        pltpu.SemaphoreType.DMA(())],
        compiler_params=_cparams(("arbitrary", "arbitrary")),
        name="ffn",
    )(x, modtab, gain.reshape(1, D_MODEL), w_up, w_up, w_down)


def _inproj_kernel(x_hbm, mod_ref, gain_ref, w_ref, b_ref, uc_ref, us_ref, gt_ref, hn_ref, xbuf_ref, sem,
                   *, tm, n_tiles):
    i = pl.program_id(0)
    j = pl.program_id(1)
    x_copy = functools.partial(_row_tile_copy, x_hbm, xbuf_ref, sem, tm=tm)

    def proj():
        return jnp.dot(hn_ref[...], _mxu_operand(w_ref), preferred_element_type=F32) + b_ref[0]

    @pl.when(j == 0)
    def _():
        @pl.when(i == 0)
        def _():
            x_copy(0).start()

        x_copy(i).wait()
        hn_ref[...] = _norm_mod(xbuf_ref[...], gain_ref, mod_ref)
        uc_ref[...] = proj()

    @pl.when(j == 1)
    def _():
        @pl.when(i + 1 < n_tiles)
        def _():
            x_copy(i + 1).start()

        uc_ref[...] = uc_ref[...] * jax.nn.sigmoid(proj())

    @pl.when(j == 2)
    def _():
        us_ref[...] = proj()

    @pl.when(j >= 3)
    def _():
        gt_ref[...] = jax.nn.sigmoid(proj()).astype(BF16)


def _inproj(x, modtab, gain, w_in, b_in, layer):
    tm = STREAM_TILE_ROWS
    tn = PROJ_COLS
    n_lat_tiles = ROWS_LAT // tm
    nblk = N_IN // tn
    n_head = nblk - 2 * D_MODEL // tn
    n_tiles = ROWS // tm
    return pl.pallas_call(
        functools.partial(_inproj_kernel, tm=tm, n_tiles=n_tiles),
        grid=(n_tiles, nblk),
        in_specs=[
            pl.BlockSpec(memory_space=pl.ANY),
            pl.BlockSpec((3, SUBLANES, D_MODEL),
                         lambda i, j: (1, jnp.where(i < n_lat_tiles, 0, 1), 0)),
            pl.BlockSpec((1, D_MODEL), lambda i, j: (0, 0)),
            pl.BlockSpec((None, D_MODEL, tn), lambda i, j: (layer, 0, j)),
            pl.BlockSpec((1, 1, tn), lambda i, j: (j, 0, 0)),
        ],
        out_specs=[
            pl.BlockSpec((tm, tn), lambda i, j: (i, 0)),
            pl.BlockSpec((tm, tn), lambda i, j: (i, 0)),
            pl.BlockSpec((tm, tn), lambda i, j: (i, jnp.maximum(j - n_head, 0))),
        ],
        out_shape=[
            jax.ShapeDtypeStruct((ROWS, CONV_DIM), F32),
            jax.ShapeDtypeStruct((ROWS, SSM_DIM), F32),
            jax.ShapeDtypeStruct((ROWS, 2 * D_MODEL), BF16),
        ],
        scratch_shapes=[pltpu.VMEM((tm, D_MODEL), BF16), pltpu.VMEM((tm, D_MODEL), F32),
                        pltpu.SemaphoreType.DMA(())],
        compiler_params=_cparams(("arbitrary", "arbitrary")),
        name="mixer_inproj",
    )(x, modtab, gain.reshape(1, D_MODEL), w_in, b_in.reshape(nblk, 1, tn))


CONV_ROW_PAD = 64
CONV_SUB_ROWS = 64
CONV_HALF = CONV_DIM // 2


def _conv_seq_taps(u_ref, w_ref, b_ref, o_ref, pad_ref, pad4_ref, *, n, cb):
    zeros = jnp.zeros((CONV_ROW_PAD, cb), F32)
    pad_ref[0:CONV_ROW_PAD, :] = zeros
    pad_ref[CONV_ROW_PAD + n:, :] = zeros
    pad_ref[CONV_ROW_PAD:CONV_ROW_PAD + n, :] = u_ref[...]
    n_pad = n + 2 * CONV_ROW_PAD
    pad4_ref[0:n_pad - SUBLANES, :] = pad_ref[BATCH:n_pad - SUBLANES + BATCH, :]
    base = CONV_ROW_PAD - CONV_PAD * BATCH
    for c in range(cb // LANES):
        cs = slice(c * LANES, (c + 1) * LANES)
        for r in range(n // CONV_SUB_ROWS):
            r0 = r * CONV_SUB_ROWS
            acc = jnp.broadcast_to(b_ref[:, cs], (CONV_SUB_ROWS, LANES))
            for k in range(CONV_K):
                off = base + r0 + k * BATCH
                if off % SUBLANES == 0:
                    tap = pad_ref[off:off + CONV_SUB_ROWS, cs]
                else:
                    tap = pad4_ref[off - BATCH:off - BATCH + CONV_SUB_ROWS, cs]
                acc = acc + w_ref[k:k + 1, cs] * tap
            o_ref[r0:r0 + CONV_SUB_ROWS, cs] = acc


def _conv_rows_kernel(u_ref, w_ref, b_ref, o_ref, pad_ref, pad4_ref):
    i = pl.program_id(0)
    o_ref[:, CONV_HALF:] = jnp.zeros((ROWS_PER_GRID_ROW, CONV_DIM - CONV_HALF), F32)

    @pl.when(i < GRID_ROWS)
    def _():
        _conv_seq_taps(u_ref, w_ref, b_ref, o_ref, pad_ref, pad4_ref, n=ROWS_PER_GRID_ROW, cb=CONV_HALF)

    @pl.when(i >= GRID_ROWS)
    def _():
        o_ref[:, :CONV_HALF] = jnp.zeros((ROWS_PER_GRID_ROW, CONV_HALF), F32)


def _conv_rows(u, w, b, *, rows_out):
    n = ROWS_PER_GRID_ROW
    n_pad = n + 2 * CONV_ROW_PAD
    return pl.pallas_call(
        _conv_rows_kernel,
        grid=(rows_out // n,),
        in_specs=[
            pl.BlockSpec((n, CONV_HALF), lambda i: (jnp.minimum(i, GRID_ROWS - 1), 0)),
            pl.BlockSpec((CONV_K, CONV_HALF), lambda i: (0, 0)),
            pl.BlockSpec((1, CONV_HALF), lambda i: (0, 0)),
        ],
        out_specs=pl.BlockSpec((n, CONV_DIM), lambda i: (i, 0)),
        out_shape=jax.ShapeDtypeStruct((rows_out, CONV_DIM), F32),
        scratch_shapes=[pltpu.VMEM((n_pad, CONV_HALF), F32), pltpu.VMEM((n_pad, CONV_HALF), F32)],
        compiler_params=_cparams(("arbitrary",)),
        name="conv_rows",
    )(u, w, b.reshape(1, CONV_DIM))


def _conv_ctx_kernel(u_ref, w_ref, b_ref, prev_ref, o_ref, pad_ref, pad4_ref):
    del prev_ref
    _conv_seq_taps(u_ref, w_ref, b_ref, o_ref, pad_ref, pad4_ref, n=ROWS_CTX, cb=LANES)


def _conv_ctx(u, w, b, prev):
    blk0 = ROWS_LAT // ROWS_CTX
    n_pad = ROWS_CTX + 2 * CONV_ROW_PAD
    return pl.pallas_call(
        _conv_ctx_kernel,
        grid=(CONV_DIM // LANES,),
        in_specs=[
            pl.BlockSpec((ROWS_CTX, LANES), lambda c: (blk0, c)),
            pl.BlockSpec((CONV_K, LANES), lambda c: (0, c)),
            pl.BlockSpec((1, LANES), lambda c: (0, c)),
            pl.BlockSpec(memory_space=pl.ANY),
        ],
        out_specs=pl.BlockSpec((ROWS_CTX, LANES), lambda c: (blk0, c)),
        out_shape=jax.ShapeDtypeStruct((ROWS, CONV_DIM), F32),
        scratch_shapes=[pltpu.VMEM((n_pad, LANES), F32), pltpu.VMEM((n_pad, LANES), F32)],
        input_output_aliases={3: 0},
        compiler_params=_cparams(("arbitrary",)),
        name="conv_ctx",
    )(u, w, b.reshape(1, CONV_DIM), prev)


def _conv_vert_kernel(u_ref, w_ref, b_ref, prev_ref, o_ref, pad_ref):
    del prev_ref
    rpg = ROWS_PER_GRID_ROW
    zeros = jnp.zeros((CONV_PAD * rpg, LANES), F32)
    pad_ref[0:CONV_PAD * rpg, :] = zeros
    pad_ref[(CONV_PAD + GRID_ROWS) * rpg:, :] = zeros
    pad_ref[CONV_PAD * rpg:(CONV_PAD + GRID_ROWS) * rpg, :] = u_ref[...]
    bias = jnp.broadcast_to(b_ref[...], (CONV_SUB_ROWS, LANES))

    def body(r, carry):
        for s in range(rpg // CONV_SUB_ROWS):
            acc = bias
            for k in range(CONV_K):
                off = pl.multiple_of((r + k) * rpg + s * CONV_SUB_ROWS, CONV_SUB_ROWS)
                acc = acc + w_ref[k:k + 1, :] * pad_ref[pl.ds(off, CONV_SUB_ROWS), :]
            o0 = pl.multiple_of(r * rpg + s * CONV_SUB_ROWS, CONV_SUB_ROWS)
            o_ref[pl.ds(o0, CONV_SUB_ROWS), :] = acc
        return carry

    lax.fori_loop(0, GRID_ROWS, body, 0)


def _conv_vert(u, w, b, prev):
    half_blk = CONV_HALF // LANES
    return pl.pallas_call(
        _conv_vert_kernel,
        grid=(half_blk,),
        in_specs=[
            pl.BlockSpec((ROWS_LAT, LANES), lambda c: (0, c + half_blk)),
            pl.BlockSpec((CONV_K, LANES), lambda c: (0, c + half_blk)),
            pl.BlockSpec((1, LANES), lambda c: (0, c + half_blk)),
            pl.BlockSpec(memory_space=pl.ANY),
        ],
        out_specs=pl.BlockSpec((ROWS_LAT, LANES), lambda c: (0, c + half_blk)),
        out_shape=jax.ShapeDtypeStruct(prev.shape, F32),
        scratch_shapes=[pltpu.VMEM(((GRID_ROWS + 2 * CONV_PAD) * ROWS_PER_GRID_ROW, LANES), F32)],
        input_output_aliases={3: 0},
        compiler_params=_cparams(("arbitrary",)),
        name="conv_vert",
    )(u, w, b.reshape(1, CONV_DIM), prev)


def _ssm_kernel(uf_ref, ub_ref, b2_ref, c2_ref, lam_ref, yf_ref, yb_ref,
                lhs_ref, s_ref, y2_ref, hst_ref):
    k = pl.program_id(0)

    @pl.when(k == 0)
    def _():
        hst_ref[...] = jnp.zeros_like(hst_ref)

    n_pair = SSM_CHUNK_T // 2
    low = lax.broadcasted_iota(jnp.int32, (SUBLANES, SSM_DIM), 0) < BATCH
    zero = jnp.zeros((SUBLANES, SSM_DIM), F32)
    for m in range(n_pair):
        f = uf_ref[m * SUBLANES:(m + 1) * SUBLANES, :]
        mb = n_pair - 1 - m
        g = ub_ref[mb * SUBLANES:(mb + 1) * SUBLANES, :]
        fr = pltpu.roll(f, BATCH, 0)
        gr = pltpu.roll(g, BATCH, 0)
        fwd = jnp.concatenate([jnp.where(low, f, zero), jnp.where(low, fr, zero)], axis=0).astype(BF16)
        bwd = jnp.concatenate([jnp.where(low, zero, g), jnp.where(low, zero, gr)], axis=0).astype(BF16)
        r0 = m * 2 * SUBLANES
        for j in range(N_SLAB):
            cs = slice(j * LANES, (j + 1) * LANES)
            lhs_ref[r0:r0 + 2 * SUBLANES, 2 * j * LANES:(2 * j + 1) * LANES] = fwd[:, cs]
            lhs_ref[r0:r0 + 2 * SUBLANES, (2 * j + 1) * LANES:(2 * j + 2) * LANES] = bwd[:, cs]

    re = slice(0, LANES)
    im = slice(LANES, 2 * LANES)

    def project(j):
        for gp in range(j * GP_PER_SLAB, (j + 1) * GP_PER_SLAB):
            s_ref[gp] = jnp.dot(lhs_ref[:, 2 * j * LANES:(2 * j + 2) * LANES], b2_ref[gp],
                                preferred_element_type=F32)

    def scan(j):
        for gp in range(j * GP_PER_SLAB, (j + 1) * GP_PER_SLAB):
            lr, li = lam_ref[gp, 0], lam_ref[gp, 1]
            hre, him = hst_ref[gp, 0], hst_ref[gp, 1]
            for s in range(SSM_CHUNK_T):
                rows = slice(s * SUBLANES, (s + 1) * SUBLANES)
                nre = lr * hre - li * him + s_ref[gp, rows, re]
                nim = lr * him + li * hre + s_ref[gp, rows, im]
                s_ref[gp, rows, re] = nre
                s_ref[gp, rows, im] = nim
                hre, him = nre, nim
            hst_ref[gp, 0] = hre
            hst_ref[gp, 1] = him

    def read_out(j):
        acc = None
        for gp in range(j * GP_PER_SLAB, (j + 1) * GP_PER_SLAB):
            p = jnp.dot(s_ref[gp].astype(BF16), c2_ref[gp], preferred_element_type=F32)
            acc = p if acc is None else acc + p
        y2_ref[:, 2 * j * LANES:(2 * j + 2) * LANES] = acc

    project(0)
    for j in range(N_SLAB):
        if j + 1 < N_SLAB:
            project(j + 1)
        scan(j)
        if j >= 1:
            read_out(j - 1)
    read_out(N_SLAB - 1)

    low1 = lax.broadcasted_iota(jnp.int32, (SUBLANES, LANES), 0) < BATCH
    for m in range(n_pair):
        mb = n_pair - 1 - m
        r0 = m * 2 * SUBLANES
        for j in range(N_SLAB):
            cf = slice(2 * j * LANES, (2 * j + 1) * LANES)
            cb = slice((2 * j + 1) * LANES, (2 * j + 2) * LANES)
            cs = slice(j * LANES, (j + 1) * LANES)
            ef = y2_ref[r0:r0 + SUBLANES, cf]
            of = pltpu.roll(y2_ref[r0 + SUBLANES:r0 + 2 * SUBLANES, cf], BATCH, 0)
            yf_ref[m * SUBLANES:(m + 1) * SUBLANES, cs] = jnp.where(low1, ef, of)
            eb = y2_ref[r0:r0 + SUBLANES, cb]
            ob = pltpu.roll(y2_ref[r0 + SUBLANES:r0 + 2 * SUBLANES, cb], BATCH, 0)
            yb_ref[mb * SUBLANES:(mb + 1) * SUBLANES, cs] = jnp.where(low1, ob, eb)


def _ssm(us, b2, c2, lamtab, layer):
    n_lat = N_CHUNK - N_CHUNK_CTX

    def fwd_blk(k):
        return jnp.where(k < N_CHUNK_CTX, n_lat + k, k - N_CHUNK_CTX)

    def bwd_blk(k):
        return N_CHUNK - 1 - k

    blk = (SSM_CHUNK_ROWS, SSM_DIM)
    return pl.pallas_call(
        _ssm_kernel,
        grid=(N_CHUNK,),
        in_specs=[
            pl.BlockSpec(blk, lambda k: (fwd_blk(k), 0)),
            pl.BlockSpec(blk, lambda k: (bwd_blk(k), 0)),
            _layer_spec(b2.shape, layer),
            _layer_spec(c2.shape, layer),
            _layer_spec(lamtab.shape, layer),
        ],
        out_specs=[
            pl.BlockSpec(blk, lambda k: (fwd_blk(k), 0)),
            pl.BlockSpec(blk, lambda k: (bwd_blk(k), 0)),
        ],
        out_shape=[jax.ShapeDtypeStruct((ROWS, SSM_DIM), F32)] * 2,
        scratch_shapes=[
            pltpu.VMEM((SSM_STEP_ROWS, 2 * SSM_DIM), BF16),
            pltpu.VMEM((N_GP, SSM_STEP_ROWS, 2 * LANES), F32),
            pltpu.VMEM((SSM_STEP_ROWS, 2 * SSM_DIM), F32),
            pltpu.VMEM((N_GP, 2, SUBLANES, LANES), F32),
        ],
        compiler_params=_cparams(("arbitrary",)),
        name="s5_scan",
    )(us, us, b2, c2, lamtab)


def _ssm_tables(a_re, a_im, log_dt, b_re, b_im, c_re, c_im):
    lam = lax.complex(a_re.astype(F32), a_im.astype(F32))
    dt = jnp.exp(log_dt.astype(F32))[..., None]
    lam_bar = jnp.exp(lam * dt)
    bmat_c = lax.complex(b_re.astype(F32), b_im.astype(F32))
    b_bar = ((lam_bar - 1) / lam)[..., None] * bmat_c

    def pair_rows(v):
        v = v.reshape(2, N_GP, 2, SSM_GROUP, SSM_STATE).transpose(1, 0, 2, 3, 4)
        z = jnp.zeros_like(v[:, :, 0])
        q0 = jnp.concatenate([v[:, :, 0], z], axis=-1)
        q1 = jnp.concatenate([z, v[:, :, 1]], axis=-1)
        return jnp.stack([q0, q1], axis=2)

    slab_pos = (jnp.arange(N_GP) % GP_PER_SLAB)[:, None] == jnp.arange(GP_PER_SLAB)[None, :]
    place = slab_pos.astype(F32)[:, None, :, None, None, None]

    def place_rows(re_v, im_v):
        t = jnp.concatenate([pair_rows(re_v), pair_rows(im_v)], axis=-1)
        t = t[:, :, None] * place
        return t.reshape(N_GP, 2 * LANES, 2 * LANES)

    bt = jnp.swapaxes(b_bar, -1, -2)
    b2 = place_rows(jnp.real(bt), jnp.imag(bt))
    c2 = jnp.swapaxes(place_rows(c_re.astype(F32), -c_im.astype(F32)), 1, 2)
    lam_ri = jnp.stack([jnp.real(lam_bar), jnp.imag(lam_bar)], axis=0)
    lam_ri = lam_ri.reshape(2, 2, N_GP, 1, LANES).transpose(2, 0, 1, 3, 4)
    lamtab = jnp.broadcast_to(lam_ri, (N_GP, 2, 2, BATCH, LANES)).reshape(N_GP, 2, SUBLANES, LANES)
    return b2.astype(BF16), c2.astype(BF16), lamtab


MIX_COLS = 512


def _mix_kernel(cv_ref, yf_ref, yb_ref, us_ref, gt_ref, lng_ref, lnb_ref, sd_ref,
                cwo_ref, wglu_ref, bglu_ref, o_ref):
    cv = cv_ref[...]
    mu = jnp.mean(cv, axis=-1, keepdims=True)
    var = jnp.mean(jnp.square(cv - mu), axis=-1, keepdims=True)
    ln = (cv - mu) * lax.rsqrt(var + EPS) * lng_ref[...] + lnb_ref[...]
    s_conv = jax.nn.silu(ln).astype(BF16)
    y = yf_ref[...] + yb_ref[...] + sd_ref[...] * us_ref[...]
    s_ssm = jax.nn.gelu(y).astype(BF16)
    for c in range(D_MODEL // MIX_COLS):
        lo = c * MIX_COLS
        ca = slice(lo, lo + MIX_COLS)
        cg = slice(D_MODEL + lo, D_MODEL + lo + MIX_COLS)
        y_conv = jnp.dot(s_conv, cwo_ref[:, ca], preferred_element_type=F32)
        za = jnp.dot(s_ssm, wglu_ref[:, ca], preferred_element_type=F32) + bglu_ref[:, ca]
        zg = jnp.dot(s_ssm, wglu_ref[:, cg], preferred_element_type=F32) + bglu_ref[:, cg]
        y_ssm = za * jax.nn.sigmoid(zg)
        mixed = gt_ref[:, ca].astype(F32) * y_conv + gt_ref[:, cg].astype(F32) * y_ssm
        o_ref[:, ca] = mixed.astype(BF16)


def _mix(cv, yf, yb, us, gt, ln_g, ln_b, ssm_d, cwo, wglu, bglu, layer, *, latent_only):
    tm = MIX_TILE_ROWS
    n_tiles = (ROWS_LAT if latent_only else ROWS) // tm
    row = lambda w: pl.BlockSpec((tm, w), lambda i: (i, 0))
    vec = lambda w: pl.BlockSpec((1, w), lambda i: (0, 0), pipeline_mode=pl.Buffered(1))
    wgt = lambda r, c: pl.BlockSpec((None, r, c), lambda i: (layer, 0, 0), pipeline_mode=pl.Buffered(1))
    return pl.pallas_call(
        _mix_kernel,
        grid=(n_tiles,),
        in_specs=[
            row(CONV_DIM), row(SSM_DIM), row(SSM_DIM), row(SSM_DIM), row(2 * D_MODEL),
            vec(CONV_DIM), vec(CONV_DIM), vec(SSM_DIM),
            wgt(CONV_DIM, D_MODEL), wgt(SSM_DIM, 2 * D_MODEL), vec(2 * D_MODEL),
        ],
        out_specs=row(D_MODEL),
        out_shape=jax.ShapeDtypeStruct((n_tiles * tm, D_MODEL), BF16),
        compiler_params=_cparams(("arbitrary",)),
        name="mixer_mix",
    )(cv, yf, yb, us, gt, ln_g.reshape(1, -1), ln_b.reshape(1, -1), ssm_d.reshape(1, -1),
      cwo, wglu, bglu.reshape(1, -1))


def _outproj_kernel(x_ref, mix_ref, mod_ref, w_ref, o_ref):
    y = jnp.dot(mix_ref[...], w_ref[...], preferred_element_type=F32)
    o_ref[...] = x_ref[...] + _rows8(y, mod_ref[2], jnp.multiply)


def _outproj(x, mix, modtab, w_out, layer, *, latent_only):
    tm = MIX_TILE_ROWS
    n_lat_tiles = ROWS_LAT // tm
    n_tiles = n_lat_tiles if latent_only else ROWS // tm
    return pl.pallas_call(
        _outproj_kernel,
        grid=(n_tiles,),
        in_specs=[
            pl.BlockSpec((tm, D_MODEL), lambda i: (i, 0)),
            pl.BlockSpec((tm, D_MODEL), lambda i: (i, 0)),
            pl.BlockSpec((3, SUBLANES, D_MODEL),
                         lambda i: (1, jnp.where(i < n_lat_tiles, 0, 1), 0)),
            pl.BlockSpec((None, D_MODEL, D_MODEL), lambda i: (layer, 0, 0)),
        ],
        out_specs=pl.BlockSpec((tm, D_MODEL), lambda i: (i, 0)),
        out_shape=jax.ShapeDtypeStruct((ROWS, D_MODEL), F32),
        input_output_aliases={0: 0},
        compiler_params=_cparams(("arbitrary",)),
        name="mixer_outproj",
    )(x, mix, modtab, w_out)


def kernel(x, c, ctx, c_ctx, ada_w, ada_b, norm_ffn1, ffn1_w_up, ffn1_w_down, norm_mix, w_in, b_in, conv_dw, conv_db, conv_ln_g, conv_ln_b, conv_w_out, ssm_a_re, ssm_a_im, ssm_log_dt, ssm_b_re, ssm_b_im, ssm_c_re, ssm_c_im, ssm_d, ssm_w_glu, ssm_b_glu, w_out, norm_ffn2, ffn2_w_up, ffn2_w_down, norm_final):
    xs = _to_rows(x, ctx)

    lat_rows = jnp.arange(SUBLANES) % BATCH
    c_rows = jnp.concatenate([c[lat_rows], jnp.broadcast_to(c_ctx, (SUBLANES, D_MODEL))], axis=0)
    modtab = _ada(c_rows, ada_w, ada_b)
    b2, c2, lamtab = jax.vmap(_ssm_tables)(ssm_a_re, ssm_a_im, ssm_log_dt,
                                            ssm_b_re, ssm_b_im, ssm_c_re, ssm_c_im)

    cwo, wglu, wout = conv_w_out.astype(BF16), ssm_w_glu.astype(BF16), w_out.astype(BF16)

    for l in range(DEPTH):
        last = l == DEPTH - 1
        mt = modtab[l]
        xs = _ffn(xs, mt, 0, norm_ffn1[l], ffn1_w_up, ffn1_w_down, l)
        uc, us, gt = _inproj(xs, mt, norm_mix[l], w_in, b_in[l], l)

        cv = _conv_rows(uc, conv_dw[l], conv_db[l], rows_out=ROWS_LAT if last else ROWS)
        cv = _conv_vert(uc, conv_dw[l], conv_db[l], cv)
        if not last:
            cv = _conv_ctx(uc, conv_dw[l], conv_db[l], cv)

        yf, yb = _ssm(us, b2, c2, lamtab, l)

        mix = _mix(cv, yf, yb, us, gt, conv_ln_g[l], conv_ln_b[l], ssm_d[l], cwo, wglu, ssm_b_glu[l], l,
                   latent_only=last)
        xs = _outproj(xs, mix, mt, wout, l, latent_only=last)
        xs = _ffn(xs, mt, 2, norm_ffn2[l], ffn2_w_up, ffn2_w_down, l, latent_only=last)

    return _final_norm(xs, norm_final)
```
